```python
import math
import jax, jax.numpy as jnp
from jax import lax
import numpy as np

D_MODEL = 4096
BATCH = 2
SEQ = 8192
DEPTH = 2

BRANCH_W = D_MODEL // 2
N_BRANCH = 3
RET_HEAD_DIM = 256
RET_W = BRANCH_W
RET_HEADS = RET_W // RET_HEAD_DIM
RET_CHUNK = 128
DSA_HEAD_DIM = 128
DSA_W = BRANCH_W
DSA_HEADS = DSA_W // DSA_HEAD_DIM
DSA_KV_HEADS = 4
DSA_KV_W = DSA_KV_HEADS * DSA_HEAD_DIM
IDX_HEADS = 32
IDX_DIM = 128
TOPK_MAX = 256
Q_BLOCK = 128
GM_W = BRANCH_W
GM_GROUPS = 16
GM_GROUP_DIM = GM_W // GM_GROUPS
GM_CHUNK = 128

ROPE_THETA = 10000.0
EPS = 1e-6

IN_SPLITS = (RET_W, RET_W, RET_W, RET_W,
             DSA_W, DSA_KV_W, DSA_KV_W,
             IDX_HEADS * IDX_DIM, IDX_DIM, IDX_HEADS,
             DSA_W,
             GM_W, GM_W, GM_W,
             N_BRANCH * D_MODEL)
N_IN = sum(IN_SPLITS)

kernel_name = "hybrid_retention_dsa_gmlp_gated"


def rms_norm(x, g):
    xf = x.astype(jnp.float32)
    y = xf * lax.rsqrt(jnp.mean(xf * xf, axis=-1, keepdims=True) + EPS)
    return (y * g.astype(jnp.float32)).astype(x.dtype)


def layer_norm(x, g):
    xf = x.astype(jnp.float32)
    mu = jnp.mean(xf, axis=-1, keepdims=True)
    xc = xf - mu
    y = xc * lax.rsqrt(jnp.mean(xc * xc, axis=-1, keepdims=True) + EPS)
    return (y * g.astype(jnp.float32)).astype(x.dtype)


def rope(x, pos):
    d = x.shape[-1]
    half = d // 2
    inv = 1.0 / (ROPE_THETA ** (jnp.arange(half, dtype=jnp.float32) * 2.0 / d))
    ang = pos.astype(jnp.float32)[..., None] * inv
    cos = jnp.cos(ang)[:, :, None, :]
    sin = jnp.sin(ang)[:, :, None, :]
    x1 = x[..., :half].astype(jnp.float32)
    x2 = x[..., half:].astype(jnp.float32)
    return jnp.concatenate([x1 * cos - x2 * sin, x2 * cos + x1 * sin], axis=-1).astype(x.dtype)


def retention(q, k, v, pos):
    B, S, H, dk = q.shape
    dv = v.shape[-1]
    C = RET_CHUNK
    N = S // C
    q = rope(q, pos).astype(jnp.float32)
    k = rope(k, pos).astype(jnp.float32) * (dk ** -0.5)
    v = v.astype(jnp.float32)
    log_g = jnp.log(1.0 - jnp.power(2.0, -5.0 - jnp.arange(H, dtype=jnp.float32)))
    n = jnp.arange(C, dtype=jnp.float32)
    diff = n[:, None] - n[None, :]
    decay = jnp.where(diff >= 0, jnp.exp(log_g[:, None, None] * jnp.maximum(diff, 0.0)), 0.0)
    xi = jnp.exp(log_g[:, None] * (n + 1.0))
    zeta = jnp.exp(log_g[:, None] * (C - 1.0 - n))
    chunk_decay = jnp.exp(log_g * C)

    def to_chunks(a):
        return a.reshape(B, N, C, H, a.shape[-1]).transpose(1, 0, 3, 2, 4)

    def step(R, qkv):
        qc, kc, vc = qkv
        inner = jnp.einsum('bhnm,bhme->bhne', jnp.einsum('bhnd,bhmd->bhnm', qc, kc) * decay, vc)
        cross = jnp.einsum('bhnd,bhde->bhne', qc * xi[..., None], R)
        R = R * chunk_decay[:, None, None] + jnp.einsum('bhmd,bhme->bhde', kc * zeta[..., None], vc)
        return R, inner + cross

    R0 = jnp.zeros((B, H, dk, dv), jnp.float32)
    _, out = lax.scan(step, R0, (to_chunks(q), to_chunks(k), to_chunks(v)))
    return out.transpose(1, 0, 3, 2, 4).reshape(B, S, H, dv)


def dsa_attention(q, k, v, q_idx, k_idx, w_idx, topk):
    B, S, H, dh = q.shape
    KVH = k.shape[2]
    G = H // KVH
    nb = S // Q_BLOCK
    key_pos = jnp.arange(S)

    def block(args):
        qb, qib, wb, t0 = args
        t_pos = t0 + jnp.arange(Q_BLOCK)
        s = jnp.einsum('bthd,bsd->bths', qib, k_idx).astype(jnp.float32) * (IDX_DIM ** -0.5)
        w = wb.astype(jnp.float32) * (IDX_HEADS ** -0.5)
        score = jnp.einsum('bth,bths->bts', w, jax.nn.relu(s))
        causal = key_pos[None, :] <= t_pos[:, None]
        score = jnp.where(causal[None], score, -jnp.inf)
        _, sel = lax.top_k(score, topk)
        k_sel = jax.vmap(lambda kb, ib: kb[ib])(k, sel)
        v_sel = jax.vmap(lambda vb, ib: vb[ib])(v, sel)
        valid = sel <= t_pos[None, :, None]
        qg = qb.reshape(B, Q_BLOCK, KVH, G, dh)
        logits = jnp.einsum('btkgd,btskd->btkgs', qg, k_sel).astype(jnp.float32) * (dh ** -0.5)
        logits = jnp.where(valid[:, :, None, None, :], logits, -jnp.inf)
        p = jax.nn.softmax(logits, axis=-1)
        o = jnp.einsum('btkgs,btskd->btkgd', p.astype(v.dtype), v_sel)
        return o.reshape(B, Q_BLOCK, H, dh)

    def to_blocks(a):
        a = a.reshape((B, nb, Q_BLOCK) + a.shape[2:])
        return jnp.moveaxis(a, 1, 0)

    t0s = jnp.arange(nb, dtype=jnp.int32) * Q_BLOCK
    out = lax.map(block, (to_blocks(q), to_blocks(q_idx), to_blocks(w_idx), t0s))
    return jnp.moveaxis(out, 0, 1).reshape(B, S, H, dh)


def spatial_gating(u, v, g_norm, w_s, b_s):
    B, S, _ = u.shape
    C = GM_CHUNK
    N = S // C
    v = layer_norm(v, g_norm)
    vc = v.reshape(B, N, C, GM_GROUPS, GM_GROUP_DIM)
    uc = u.reshape(B, N, C, GM_GROUPS, GM_GROUP_DIM)
    w_masked = w_s * jnp.tril(jnp.ones((C, C), w_s.dtype))[None]
    mixed = jnp.einsum('gts,bnsgc->bntgc', w_masked, vc) + b_s.T[None, None, :, :, None]
    return (uc * mixed).reshape(B, S, GM_W)


def setup_inputs(seed: int = 0) -> dict:
    key = jax.random.key(seed)
    ks = jax.random.split(key, 12)
    x = jax.random.normal(ks[0], (BATCH, SEQ, D_MODEL), jnp.float32)
    offset = jax.random.randint(ks[1], (BATCH, 1), 0, 1024, dtype=jnp.int32)
    positions = (offset + jnp.arange(SEQ, dtype=jnp.int32)[None, :]).astype(jnp.int32)
    norm_gain = 1.0 + 0.02 * jax.random.normal(ks[2], (DEPTH, D_MODEL), jnp.float32)
    w_in = jax.random.normal(ks[3], (DEPTH, D_MODEL, N_IN), jnp.float32) * (D_MODEL ** -0.5)
    ret_norm_gain = 1.0 + 0.02 * jax.random.normal(ks[4], (DEPTH, RET_W), jnp.float32)
    q_norm_gain = 1.0 + 0.02 * jax.random.normal(ks[5], (DEPTH, DSA_HEAD_DIM), jnp.float32)
    k_norm_gain = 1.0 + 0.02 * jax.random.normal(ks[6], (DEPTH, DSA_HEAD_DIM), jnp.float32)
    gm_norm_gain = 1.0 + 0.02 * jax.random.normal(ks[7], (DEPTH, GM_W), jnp.float32)
    w_spatial = jax.random.normal(ks[8], (DEPTH, GM_GROUPS, GM_CHUNK, GM_CHUNK), jnp.float32) * (GM_CHUNK ** -0.5)
    b_spatial = 1.0 + 0.02 * jax.random.normal(ks[9], (DEPTH, GM_GROUPS, GM_CHUNK), jnp.float32)
    w_branch = jax.random.normal(ks[10], (DEPTH, N_BRANCH, BRANCH_W, D_MODEL), jnp.float32) * (BRANCH_W ** -0.5)
    w_out = jax.random.normal(ks[11], (DEPTH, D_MODEL, D_MODEL), jnp.float32) * (D_MODEL ** -0.5)
    return {"x": x, "positions": positions, "norm_gain": norm_gain, "w_in": w_in,
            "ret_norm_gain": ret_norm_gain, "q_norm_gain": q_norm_gain, "k_norm_gain": k_norm_gain,
            "gm_norm_gain": gm_norm_gain, "w_spatial": w_spatial, "b_spatial": b_spatial,
            "w_branch": w_branch, "w_out": w_out}


def reference(x, positions, norm_gain, w_in, ret_norm_gain, q_norm_gain, k_norm_gain,
              gm_norm_gain, w_spatial, b_spatial, w_branch, w_out):
    B, S, _ = x.shape
    topk = min(TOPK_MAX, S // 4)
    split_points = np.cumsum(IN_SPLITS)[:-1].tolist()
    for l in range(DEPTH):
        xn = rms_norm(x, norm_gain[l])
        h = jnp.einsum('bsd,dn->bsn', xn, w_in[l])
        (rq, rk, rv, rg, dq, dk, dv, iq, ik, iw, dg, gu, gv, gg, mg) = jnp.split(h, split_points, axis=-1)

        ret = retention(rq.reshape(B, S, RET_HEADS, RET_HEAD_DIM),
                        rk.reshape(B, S, RET_HEADS, RET_HEAD_DIM),
                        rv.reshape(B, S, RET_HEADS, RET_HEAD_DIM), positions)
        ret = layer_norm(ret, ret_norm_gain[l].reshape(RET_HEADS, RET_HEAD_DIM)).astype(x.dtype)
        y_ret = jax.nn.silu(rg) * ret.reshape(B, S, RET_W)

        q = rope(rms_norm(dq.reshape(B, S, DSA_HEADS, DSA_HEAD_DIM), q_norm_gain[l]), positions)
        k = rope(rms_norm(dk.reshape(B, S, DSA_KV_HEADS, DSA_HEAD_DIM), k_norm_gain[l]), positions)
        v = dv.reshape(B, S, DSA_KV_HEADS, DSA_HEAD_DIM)
        qi = rope(iq.reshape(B, S, IDX_HEADS, IDX_DIM), positions)
        ki = rope(ik[:, :, None, :], positions)[:, :, 0]
        att = dsa_attention(q, k, v, qi, ki, iw, topk)
        y_dsa = jax.nn.silu(dg) * att.reshape(B, S, DSA_W)

        sg = spatial_gating(jax.nn.gelu(gu), jax.nn.gelu(gv), gm_norm_gain[l], w_spatial[l], b_spatial[l])
        y_gm = jax.nn.silu(gg) * sg

        gates = jax.nn.sigmoid(mg).reshape(B, S, N_BRANCH, D_MODEL)
        merged = (gates[:, :, 0] * jnp.einsum('bsw,wd->bsd', y_ret, w_branch[l, 0])
                  + gates[:, :, 1] * jnp.einsum('bsw,wd->bsd', y_dsa, w_branch[l, 1])
                  + gates[:, :, 2] * jnp.einsum('bsw,wd->bsd', y_gm, w_branch[l, 2]))
        x = x + jnp.einsum('bsd,de->bse', merged, w_out[l])
    return x
```

```python
import functools

import numpy as np
import jax
import jax.numpy as jnp
from jax import lax
from jax.experimental import pallas as pl
from jax.experimental.pallas import tpu as pltpu

D_MODEL = 4096
BRANCH_W = D_MODEL // 2
N_BRANCH = 3
RET_HEAD_DIM = 256
RET_HEADS = BRANCH_W // RET_HEAD_DIM
RET_CHUNK = 128
DSA_HEAD_DIM = 128
DSA_HEADS = BRANCH_W // DSA_HEAD_DIM
DSA_KV_HEADS = 4
DSA_GROUP = DSA_HEADS // DSA_KV_HEADS
DSA_KV_W = DSA_KV_HEADS * DSA_HEAD_DIM
IDX_HEADS = 32
IDX_DIM = 128
TOPK_MAX = 256
Q_BLOCK = 128
GM_GROUPS = 16
GM_GROUP_DIM = BRANCH_W // GM_GROUPS
GM_CHUNK = 128
ROPE_THETA = 10000.0
EPS = 1e-6

LANES = 128
VMEM_LIMIT = 56 * 1024 * 1024

_REF_SEGS = (("rq", BRANCH_W), ("rk", BRANCH_W), ("rv", BRANCH_W), ("rg", BRANCH_W),
             ("dq", BRANCH_W), ("dk", DSA_KV_W), ("dv", DSA_KV_W),
             ("iq", IDX_HEADS * IDX_DIM), ("ik", IDX_DIM), ("iw", IDX_HEADS),
             ("dg", BRANCH_W), ("gu", BRANCH_W), ("gv", BRANCH_W), ("gg", BRANCH_W),
             ("mg", N_BRANCH * D_MODEL))
_MY_ORDER = ("mg", "iq", "rq", "rk", "rv", "rg", "dq", "dg", "gu", "gv", "gg", "dk", "dv", "ik", "iw")
N_TILE = 1024


def _layout():
    ref_off, o = {}, 0
    for name, w in _REF_SEGS:
        ref_off[name] = (o, w)
        o += w
    my_off, o = {}, 0
    for name in _MY_ORDER:
        w = ref_off[name][1]
        my_off[name] = o
        o += -(-w // LANES) * LANES
    total = -(-o // N_TILE) * N_TILE
    return ref_off, my_off, total


_REF_OFF, SEG, N_IN_PAD = _layout()

_NT = (((1,), (1,)), ((), ()))


def _cparams(*sem):
    return pltpu.CompilerParams(dimension_semantics=sem, vmem_limit_bytes=VMEM_LIMIT)


def _rmsnorm_kernel(x_ref, g_ref, o_ref):
    x = x_ref[...]
    ms = jnp.mean(x * x, axis=-1, keepdims=True)
    o_ref[...] = (x * lax.rsqrt(ms + EPS) * g_ref[...]).astype(o_ref.dtype)


def _rmsnorm(x2, gain):
    m, d = x2.shape
    tm = min(256, m)
    return pl.pallas_call(
        _rmsnorm_kernel,
        grid=(m // tm,),
        in_specs=[pl.BlockSpec((tm, d), lambda i: (i, 0)), pl.BlockSpec((1, d), lambda i: (0, 0))],
        out_specs=pl.BlockSpec((tm, d), lambda i: (i, 0)),
        out_shape=jax.ShapeDtypeStruct((m, d), jnp.bfloat16),
        compiler_params=_cparams("parallel"),
        name="rmsnorm",
    )(x2, gain.reshape(1, d))


def _matmul_kernel(a_ref, w_ref, o_ref):
    o_ref[...] = jnp.dot(a_ref[...], w_ref[...], preferred_element_type=jnp.float32).astype(o_ref.dtype)


def _in_proj(xn, w):
    m, k = xn.shape
    n = w.shape[1]
    tm, tn = min(1024, m), N_TILE
    return pl.pallas_call(
        _matmul_kernel,
        grid=(m // tm, n // tn),
        in_specs=[pl.BlockSpec((tm, k), lambda i, j: (i, 0)), pl.BlockSpec((k, tn), lambda i, j: (0, j))],
        out_specs=pl.BlockSpec((tm, tn), lambda i, j: (i, j)),
        out_shape=jax.ShapeDtypeStruct((m, n), jnp.bfloat16),
        compiler_params=_cparams("parallel", "arbitrary"),
        name="in_proj",
    )(xn, w)


def _rope_tables_kernel(pos_ref, inv_a_ref, inv_b_ref, sign_ref, ca_ref, sa_ref, cb_ref, sb_ref):
    p = pos_ref[...].astype(jnp.float32)
    ang_a = p * inv_a_ref[...]
    ca_ref[...] = jnp.cos(ang_a)
    sa_ref[...] = jnp.sin(ang_a)
    ang_b = p * inv_b_ref[...]
    cb_ref[...] = jnp.cos(ang_b)
    sb_ref[...] = jnp.sin(ang_b) * sign_ref[...]


def _rope_tables(positions):
    m = positions.size
    tr = min(1024, m)

    def inv_freq(d):
        half = d // 2
        return 1.0 / (ROPE_THETA ** (jnp.arange(half, dtype=jnp.float32) * 2.0 / d))

    inv_a = inv_freq(RET_HEAD_DIM).reshape(1, LANES)
    inv_b = jnp.tile(inv_freq(DSA_HEAD_DIM), 2).reshape(1, LANES)
    sign = jnp.concatenate([-jnp.ones((LANES // 2,), jnp.float32), jnp.ones((LANES // 2,), jnp.float32)]).reshape(1, LANES)
    row = pl.BlockSpec((1, LANES), lambda i: (0, 0))
    tab = pl.BlockSpec((tr, LANES), lambda i: (i, 0))
    sds = jax.ShapeDtypeStruct((m, LANES), jnp.float32)
    return pl.pallas_call(
        _rope_tables_kernel,
        grid=(m // tr,),
        in_specs=[pl.BlockSpec((tr, 1), lambda i: (i, 0)), row, row, row],
        out_specs=[tab, tab, tab, tab],
        out_shape=[sds, sds, sds, sds],
        compiler_params=_cparams("parallel"),
        name="rope_tables",
    )(positions.reshape(m, 1), inv_a, inv_b, sign)


def _retention_kernel(q_ref, k_ref, v_ref, g_ref, cos_ref, sin_ref, decay_ref, xi_ref, zeta_ref, cd_ref,
                      gain_ref, o_ref, state_ref, *, n_chunk):
    @pl.when(pl.program_id(2) == 0)
    def _():
        state_ref[...] = jnp.zeros_like(state_ref)

    half = RET_HEAD_DIM // 2
    decay = decay_ref[...]
    xi = xi_ref[...]
    zeta = zeta_ref[...]
    cd = cd_ref[0:1, :]
    gain = gain_ref[...]
    k_scale = RET_HEAD_DIM ** -0.5

    def rot(x, cos, sin):
        x1, x2 = x[:, :half], x[:, half:]
        return x1 * cos - x2 * sin, x2 * cos + x1 * sin

    for c in range(n_chunk):
        rows = slice(c * RET_CHUNK, (c + 1) * RET_CHUNK)
        cos, sin = cos_ref[rows, :], sin_ref[rows, :]
        q1, q2 = rot(q_ref[rows, :].astype(jnp.float32), cos, sin)
        k1, k2 = rot(k_ref[rows, :].astype(jnp.float32), cos, sin)
        k1, k2 = k1 * k_scale, k2 * k_scale
        v = v_ref[rows, :]
        qb = jnp.concatenate([q1, q2], axis=1).astype(jnp.bfloat16)
        kb = jnp.concatenate([k1, k2], axis=1).astype(jnp.bfloat16)
        sc = lax.dot_general(qb, kb, _NT, preferred_element_type=jnp.float32) * decay
        inner = jnp.dot(sc.astype(jnp.bfloat16), v, preferred_element_type=jnp.float32)
        state = state_ref[...]
        qx = jnp.concatenate([q1 * xi, q2 * xi], axis=1).astype(jnp.bfloat16)
        cross = jnp.dot(qx, state.astype(jnp.bfloat16), preferred_element_type=jnp.float32)
        out = inner + cross
        kz_t = jnp.concatenate([(k1 * zeta).T, (k2 * zeta).T], axis=0).astype(jnp.bfloat16)
        state_ref[...] = (state * jnp.concatenate([cd, cd], axis=1)
                          + jnp.dot(kz_t, v, preferred_element_type=jnp.float32))
        mu = jnp.mean(out, axis=-1, keepdims=True)
        oc = out - mu
        y = oc * lax.rsqrt(jnp.mean(oc * oc, axis=-1, keepdims=True) + EPS) * gain
        g = g_ref[rows, :].astype(jnp.float32)
        o_ref[rows, :] = (g * jax.nn.sigmoid(g) * y).astype(o_ref.dtype)


def _retention(h, cos_a, sin_a, ret_gain, batch, seq):
    m = batch * seq
    hd = RET_HEAD_DIM
    rb = min(1024, seq)
    nrb = seq // rb
    c = RET_CHUNK
    log_g = jnp.log(1.0 - jnp.power(2.0, -5.0 - jnp.arange(RET_HEADS, dtype=jnp.float32)))
    n = jnp.arange(c, dtype=jnp.float32)
    diff = n[:, None] - n[None, :]
    decay = jnp.where(diff >= 0, jnp.exp(log_g[:, None, None] * jnp.maximum(diff, 0.0)), 0.0)
    xi = jnp.broadcast_to(jnp.exp(log_g[:, None] * (n + 1.0))[:, :, None], (RET_HEADS, c, LANES))
    zeta = jnp.broadcast_to(jnp.exp(log_g[:, None] * (c - 1.0 - n))[:, :, None], (RET_HEADS, c, LANES))
    cd = jnp.broadcast_to(jnp.exp(log_g * c)[:, None, None], (RET_HEADS, 8, LANES))

    def seg(name):
        base = SEG[name] // hd
        return pl.BlockSpec((rb, hd), lambda b, hh, r: (b * nrb + r, base + hh))

    tab = pl.BlockSpec((rb, LANES), lambda b, hh, r: (b * nrb + r, 0))
    per_head = lambda rows: pl.BlockSpec((None, rows, LANES), lambda b, hh, r: (hh, 0, 0))
    return pl.pallas_call(
        functools.partial(_retention_kernel, n_chunk=rb // c),
        grid=(batch, RET_HEADS, nrb),
        in_specs=[seg("rq"), seg("rk"), seg("rv"), seg("rg"), tab, tab,
                  per_head(c), per_head(c), per_head(c), per_head(8),
                  pl.BlockSpec((1, hd), lambda b, hh, r: (0, hh))],
        out_specs=pl.BlockSpec((rb, hd), lambda b, hh, r: (b * nrb + r, hh)),
        out_shape=jax.ShapeDtypeStruct((m, BRANCH_W), jnp.bfloat16),
        scratch_shapes=[pltpu.VMEM((hd, hd), jnp.float32)],
        compiler_params=_cparams("parallel", "parallel", "arbitrary"),
        name="retention",
    )(h, h, h, h, cos_a, sin_a, decay, xi, zeta, cd, ret_gain.reshape(1, BRANCH_W))


def _dsa_prep_kernel(dq_ref, dk_ref, dv_ref, iq_ref, ik_ref, iw_ref, cos_ref, sin_ref, qgain_ref, kgain_ref,
                     qg_ref, k_ref, vt_ref, qi_ref, kidx_ref, w_ref):
    cos, sin = cos_ref[...], sin_ref[...]
    d = DSA_HEAD_DIM

    def rope(x):
        return x * cos + pltpu.roll(x, d // 2, 1) * sin

    def rms(x, g):
        return x * lax.rsqrt(jnp.mean(x * x, axis=-1, keepdims=True) + EPS) * g

    qgain, kgain = qgain_ref[...], kgain_ref[...]
    for kvh in range(DSA_KV_HEADS):
        for g in range(DSA_GROUP):
            hh = kvh * DSA_GROUP + g
            x = dq_ref[:, hh * d:(hh + 1) * d].astype(jnp.float32)
            qg_ref[kvh, g * Q_BLOCK:(g + 1) * Q_BLOCK, :] = rope(rms(x, qgain)).astype(qg_ref.dtype)
        x = dk_ref[:, kvh * d:(kvh + 1) * d].astype(jnp.float32)
        k_ref[kvh] = rope(rms(x, kgain)).astype(k_ref.dtype)
        vt_ref[kvh] = dv_ref[:, kvh * d:(kvh + 1) * d].astype(jnp.float32).T.astype(vt_ref.dtype)
    for hh in range(IDX_HEADS):
        x = iq_ref[:, hh * IDX_DIM:(hh + 1) * IDX_DIM].astype(jnp.float32)
        qi_ref[hh * Q_BLOCK:(hh + 1) * Q_BLOCK, :] = rope(x).astype(qi_ref.dtype)
    kidx_ref[...] = rope(ik_ref[...].astype(jnp.float32)).astype(kidx_ref.dtype)
    w = iw_ref[...].astype(jnp.float32) * (IDX_DIM ** -0.5 * IDX_HEADS ** -0.5)
    w_ref[...] = w.T[:IDX_HEADS, :]


def _dsa_prep(h, cos_b, sin_b, q_gain, k_gain, batch, seq):
    nq = seq // Q_BLOCK
    d = DSA_HEAD_DIM

    def seg(name, width):
        base = SEG[name] // width
        return pl.BlockSpec((Q_BLOCK, width), lambda b, j: (b * nq + j, base))

    tab = pl.BlockSpec((Q_BLOCK, LANES), lambda b, j: (b * nq + j, 0))
    gain = pl.BlockSpec((1, d), lambda b, j: (0, 0))
    bf = jnp.bfloat16
    return pl.pallas_call(
        _dsa_prep_kernel,
        grid=(batch, nq),
        in_specs=[seg("dq", BRANCH_W), seg("dk", DSA_KV_W), seg("dv", DSA_KV_W), seg("iq", IDX_HEADS * IDX_DIM),
                  seg("ik", IDX_DIM), seg("iw", LANES), tab, tab, gain, gain],
        out_specs=[
            pl.BlockSpec((None, DSA_KV_HEADS, None, DSA_GROUP * Q_BLOCK, d), lambda b, j: (b, 0, j, 0, 0)),
            pl.BlockSpec((None, DSA_KV_HEADS, Q_BLOCK, d), lambda b, j: (b, 0, j, 0)),
            pl.BlockSpec((None, DSA_KV_HEADS, None, d, Q_BLOCK), lambda b, j: (b, 0, j, 0, 0)),
            pl.BlockSpec((None, None, IDX_HEADS * Q_BLOCK, IDX_DIM), lambda b, j: (b, j, 0, 0)),
            pl.BlockSpec((None, Q_BLOCK, IDX_DIM), lambda b, j: (b, j, 0)),
            pl.BlockSpec((None, None, IDX_HEADS, Q_BLOCK), lambda b, j: (b, j, 0, 0)),
        ],
        out_shape=[
            jax.ShapeDtypeStruct((batch, DSA_KV_HEADS, nq, DSA_GROUP * Q_BLOCK, d), bf),
            jax.ShapeDtypeStruct((batch, DSA_KV_HEADS, seq, d), bf),
            jax.ShapeDtypeStruct((batch, DSA_KV_HEADS, nq, d, Q_BLOCK), bf),
            jax.ShapeDtypeStruct((batch, nq, IDX_HEADS * Q_BLOCK, IDX_DIM), bf),
            jax.ShapeDtypeStruct((batch, seq, IDX_DIM), bf),
            jax.ShapeDtypeStruct((batch, nq, IDX_HEADS, Q_BLOCK), jnp.float32),
        ],
        compiler_params=_cparams("parallel", "parallel"),
        name="dsa_prep",
    )(h, h, h, h, h, h, cos_b, sin_b, q_gain.reshape(1, d), k_gain.reshape(1, d))


def _dsa_select_kernel(qi_ref, w_ref, kidx_ref, mask_ref, keys_ref, *, topk, nq):
    j = pl.program_id(1)
    c_ = Q_BLOCK
    sub = lax.broadcasted_iota(jnp.int32, (c_, c_), 0)
    lane = lax.broadcasted_iota(jnp.int32, (c_, c_), 1)
    int_min = jnp.int32(-2 ** 31)
    n_live = j + 1

    def causal(c):
        return (c * c_ + sub) <= (j * c_ + lane)

    def score_chunk(c, carry):
        r0 = pl.multiple_of(c * c_, c_)
        kt = kidx_ref[pl.ds(r0, c_), :]
        acc = jnp.zeros((c_, c_), jnp.float32)
        for hg in range(IDX_HEADS // 4):
            s = lax.dot_general(kt, qi_ref[hg * 4 * c_:(hg + 1) * 4 * c_, :], _NT,
                                preferred_element_type=jnp.float32)
            for hh in range(4):
                head = hg * 4 + hh
                acc = acc + jnp.maximum(s[:, hh * c_:(hh + 1) * c_], 0.0) * w_ref[head:head + 1, :]
        bits = lax.bitcast_convert_type(acc, jnp.int32)
        key = bits ^ ((bits >> 31) & jnp.int32(0x7FFFFFFF))
        keys_ref[pl.ds(r0, c_), :] = jnp.where(causal(c), key, int_min)
        return carry

    lax.fori_loop(0, n_live, score_chunk, 0)

    def count(pred):
        def body(c, acc8):
            r0 = pl.multiple_of(c * c_, c_)
            hit = jnp.where(pred(keys_ref[pl.ds(r0, c_), :]), 1.0, 0.0)
            return acc8 + jnp.sum(hit.reshape(c_ // 8, 8, c_), axis=0)
        acc8 = lax.fori_loop(0, n_live, body, jnp.zeros((8, c_), jnp.float32))
        return jnp.sum(acc8, axis=0, keepdims=True)

    kf = jnp.float32(topk)
    zero = jnp.zeros((1, c_), jnp.int32)
    thr = jnp.where(count(lambda kc: kc >= zero) >= kf, zero, int_min)

    def bit_step(i, thr):
        cand = thr | jnp.left_shift(jnp.int32(1), 30 - i)
        return jnp.where(count(lambda kc: kc >= cand) >= kf, cand, thr)

    thr = lax.fori_loop(0, 31, bit_step, thr)
    room = kf - count(lambda kc: kc > thr)
    tri = jnp.where(sub > lane, 1.0, 0.0).astype(jnp.bfloat16)

    def mask_chunk(c, seen):
        r0 = pl.multiple_of(c * c_, c_)
        kc = keys_ref[pl.ds(r0, c_), :]
        eq = jnp.where(kc == thr, 1.0, 0.0)
        before = jnp.dot(tri, eq.astype(jnp.bfloat16), preferred_element_type=jnp.float32) + seen
        take = jnp.where(kc > thr, 1.0, jnp.where(before < room, eq, 0.0))
        mask_ref[pl.ds(r0, c_), :] = jnp.where(causal(c), take, 0.0).astype(mask_ref.dtype)
        return seen + jnp.sum(eq, axis=0, keepdims=True)

    lax.fori_loop(0, n_live, mask_chunk, jnp.zeros((1, c_), jnp.float32))

    def zero_chunk(c, carry):
        r0 = pl.multiple_of(c * c_, c_)
        mask_ref[pl.ds(r0, c_), :] = jnp.zeros((c_, c_), mask_ref.dtype)
        return carry

    lax.fori_loop(n_live, nq, zero_chunk, 0)


def _dsa_select(qi, w, kidx, topk):
    batch, nq = qi.shape[0], qi.shape[1]
    seq = kidx.shape[1]
    return pl.pallas_call(
        functools.partial(_dsa_select_kernel, topk=topk, nq=nq),
        grid=(batch, nq),
        in_specs=[pl.BlockSpec((None, None, IDX_HEADS * Q_BLOCK, IDX_DIM), lambda b, j: (b, j, 0, 0)),
                  pl.BlockSpec((None, None, IDX_HEADS, Q_BLOCK), lambda b, j: (b, j, 0, 0)),
                  pl.BlockSpec((None, seq, IDX_DIM), lambda b, j: (b, 0, 0))],
        out_specs=pl.BlockSpec((None, seq, Q_BLOCK), lambda b, j: (b, 0, j)),
        out_shape=jax.ShapeDtypeStruct((batch, seq, seq), jnp.bfloat16),
        scratch_shapes=[pltpu.VMEM((seq, Q_BLOCK), jnp.int32)],
        compiler_params=_cparams("parallel", "arbitrary"),
        name="dsa_select",
    )(qi, w, kidx)


def _dsa_attn_kernel(qg_ref, k_ref, vt_ref, mask_ref, dg_ref, o_ref, m_ref, l_ref, acc_ref, *, tk):
    j = pl.program_id(2)
    per = tk // Q_BLOCK
    n_tiles = (j + per) // per
    q = qg_ref[...]
    scale = DSA_HEAD_DIM ** -0.5
    neg_inf = jnp.float32(-jnp.inf)
    m_ref[...] = jnp.full_like(m_ref, neg_inf)
    l_ref[...] = jnp.zeros_like(l_ref)
    acc_ref[...] = jnp.zeros_like(acc_ref)

    def tile(c, carry):
        r0 = pl.multiple_of(c * tk, tk)
        s = lax.dot_general(k_ref[pl.ds(r0, tk), :], q, _NT, preferred_element_type=jnp.float32) * scale
        sel = mask_ref[pl.ds(r0, tk), :].astype(jnp.float32) > 0.5
        s = jnp.concatenate([jnp.where(sel, s[:, g * Q_BLOCK:(g + 1) * Q_BLOCK], neg_inf)
                             for g in range(DSA_GROUP)], axis=1)
        m_old = m_ref[...]
        m_new = jnp.maximum(m_old, jnp.max(s, axis=0, keepdims=True))
        m_safe = jnp.where(m_new == neg_inf, 0.0, m_new)
        p = jnp.exp(s - m_safe)
        alpha = jnp.exp(m_old - m_safe)
        l_ref[...] = alpha * l_ref[...] + jnp.sum(p, axis=0, keepdims=True)
        vt = jnp.concatenate([vt_ref[c * per + i] for i in range(per)], axis=1)
        acc_ref[...] = acc_ref[...] * alpha + jnp.dot(vt, p.astype(jnp.bfloat16),
                                                      preferred_element_type=jnp.float32)
        m_ref[...] = m_new
        return carry

    lax.fori_loop(0, n_tiles, tile, 0)
    out_t = acc_ref[...] / l_ref[...]
    for g in range(DSA_GROUP):
        cols = slice(g * Q_BLOCK, (g + 1) * Q_BLOCK)
        gate = dg_ref[:, cols].astype(jnp.float32)
        o_ref[:, cols] = (gate * jax.nn.sigmoid(gate) * out_t[:, cols].T).astype(o_ref.dtype)


def _dsa_attention(qg, k, vt, mask, h, batch, seq):
    nq = seq // Q_BLOCK
    d = DSA_HEAD_DIM
    gw = DSA_GROUP * d
    tk = min(512, seq)
    dg_base = SEG["dg"] // gw
    return pl.pallas_call(
        functools.partial(_dsa_attn_kernel, tk=tk),
        grid=(batch, DSA_KV_HEADS, nq),
        in_specs=[pl.BlockSpec((None, None, None, DSA_GROUP * Q_BLOCK, d), lambda b, kv, j: (b, kv, j, 0, 0)),
                  pl.BlockSpec((None, None, seq, d), lambda b, kv, j: (b, kv, 0, 0)),
                  pl.BlockSpec((None, None, nq, d, Q_BLOCK), lambda b, kv, j: (b, kv, 0, 0, 0)),
                  pl.BlockSpec((None, seq, Q_BLOCK), lambda b, kv, j: (b, 0, j)),
                  pl.BlockSpec((Q_BLOCK, gw), lambda b, kv, j: (b * nq + j, dg_base + kv))],
        out_specs=pl.BlockSpec((Q_BLOCK, gw), lambda b, kv, j: (b * nq + j, kv)),
        out_shape=jax.ShapeDtypeStruct((batch * seq, BRANCH_W), jnp.bfloat16),
        scratch_shapes=[pltpu.VMEM((1, gw), jnp.float32), pltpu.VMEM((1, gw), jnp.float32),
                        pltpu.VMEM((d, gw), jnp.float32)],
        compiler_params=_cparams("parallel", "parallel", "arbitrary"),
        name="dsa_attn",
    )(qg, k, vt, mask, h)


def _gelu_tanh(x):
    return 0.5 * x * (1.0 + jnp.tanh(np.sqrt(2.0 / np.pi).astype(np.float32) * (x + 0.044715 * (x * x * x))))


def _gmlp_kernel(u_ref, v_ref, g_ref, gain_ref, w_ref, b_ref, o_ref):
    c_ = GM_CHUNK
    v = _gelu_tanh(v_ref[...].astype(jnp.float32))
    mu = jnp.mean(v, axis=-1, keepdims=True)
    vc = v - mu
    vn = (vc * lax.rsqrt(jnp.mean(vc * vc, axis=-1, keepdims=True) + EPS) * gain_ref[...]).astype(jnp.bfloat16)
    sub = lax.broadcasted_iota(jnp.int32, (c_, c_), 0)
    lane = lax.broadcasted_iota(jnp.int32, (c_, c_), 1)
    tril = sub >= lane
    for g in range(GM_GROUPS):
        cols = slice(g * GM_GROUP_DIM, (g + 1) * GM_GROUP_DIM)
        w = jnp.where(tril, w_ref[g], jnp.zeros((), w_ref.dtype))
        mixed = jnp.dot(w, vn[:, cols], preferred_element_type=jnp.float32) + b_ref[g]
        u = _gelu_tanh(u_ref[:, cols].astype(jnp.float32))
        gate = g_ref[:, cols].astype(jnp.float32)
        o_ref[:, cols] = (gate * jax.nn.sigmoid(gate) * (u * mixed)).astype(o_ref.dtype)


def _gmlp(h, gm_gain, w_spatial, b_spatial, m):
    c_ = GM_CHUNK
    w = BRANCH_W

    def seg(name):
        base = SEG[name] // w
        return pl.BlockSpec((c_, w), lambda i: (i, base))

    b_b = jnp.broadcast_to(b_spatial[:, :, None], (GM_GROUPS, c_, GM_GROUP_DIM))
    whole = lambda shape: pl.BlockSpec(shape, lambda i: (0,) * len(shape))
    return pl.pallas_call(
        _gmlp_kernel,
        grid=(m // c_,),
        in_specs=[seg("gu"), seg("gv"), seg("gg"), whole((1, w)), whole((GM_GROUPS, c_, c_)),
                  whole((GM_GROUPS, c_, GM_GROUP_DIM))],
        out_specs=pl.BlockSpec((c_, w), lambda i: (i, 0)),
        out_shape=jax.ShapeDtypeStruct((m, w), jnp.bfloat16),
        compiler_params=_cparams("parallel"),
        name="gmlp",
    )(h, h, h, gm_gain.reshape(1, w), w_spatial.astype(jnp.bfloat16), b_b)


def _merge_kernel(y0_ref, y1_ref, y2_ref, w_ref, g0_ref, g1_ref, g2_ref, o_ref):
    acc = None
    for b, (y_ref, g_ref) in enumerate(((y0_ref, g0_ref), (y1_ref, g1_ref), (y2_ref, g2_ref))):
        proj = jnp.dot(y_ref[...], w_ref[b], preferred_element_type=jnp.float32)
        term = jax.nn.sigmoid(g_ref[...].astype(jnp.float32)) * proj
        acc = term if acc is None else acc + term
    o_ref[...] = acc.astype(o_ref.dtype)


def _merge(ys, w_branch, h):
    m = h.shape[0]
    tm, tn = min(1024, m), 512
    ybs = pl.BlockSpec((tm, BRANCH_W), lambda i, j: (i, 0))

    def gate(b):
        base = (SEG["mg"] + b * D_MODEL) // tn
        return pl.BlockSpec((tm, tn), lambda i, j: (i, base + j))

    return pl.pallas_call(
        _merge_kernel,
        grid=(m // tm, D_MODEL // tn),
        in_specs=[ybs, ybs, ybs, pl.BlockSpec((N_BRANCH, BRANCH_W, tn), lambda i, j: (0, 0, j)),
                  gate(0), gate(1), gate(2)],
        out_specs=pl.BlockSpec((tm, tn), lambda i, j: (i, j)),
        out_shape=jax.ShapeDtypeStruct((m, D_MODEL), jnp.bfloat16),
        compiler_params=_cparams("parallel", "arbitrary"),
        name="merge",
    )(*ys, w_branch, h, h, h)


def _out_proj_kernel(a_ref, w_ref, x_ref, o_ref):
    o_ref[...] = x_ref[...] + jnp.dot(a_ref[...], w_ref[...], preferred_element_type=jnp.float32)


def _out_proj(merged, w_out, x2):
    m, d = x2.shape
    tm, tn = min(1024, m), 512
    return pl.pallas_call(
        _out_proj_kernel,
        grid=(m // tm, d // tn),
        in_specs=[pl.BlockSpec((tm, d), lambda i, j: (i, 0)), pl.BlockSpec((d, tn), lambda i, j: (0, j)),
                  pl.BlockSpec((tm, tn), lambda i, j: (i, j))],
        out_specs=pl.BlockSpec((tm, tn), lambda i, j: (i, j)),
        out_shape=jax.ShapeDtypeStruct((m, d), jnp.float32),
        compiler_params=_cparams("parallel", "arbitrary"),
        name="out_proj",
    )(merged, w_out, x2)


def _relayout_w_in(w):
    wb = w.astype(jnp.bfloat16)
    parts, o = [], 0
    for name in _MY_ORDER:
        off, width = _REF_OFF[name]
        assert SEG[name] == o
        parts.append(wb[:, off:off + width])
        padded = -(-width // LANES) * LANES
        if padded != width:
            parts.append(jnp.zeros((w.shape[0], padded - width), jnp.bfloat16))
        o += padded
    if N_IN_PAD != o:
        parts.append(jnp.zeros((w.shape[0], N_IN_PAD - o), jnp.bfloat16))
    return jnp.concatenate(parts, axis=1)


def kernel(x, positions, norm_gain, w_in, ret_norm_gain, q_norm_gain, k_norm_gain, gm_norm_gain, w_spatial,
           b_spatial, w_branch, w_out):
    batch, seq, d = x.shape
    assert d == D_MODEL and seq % Q_BLOCK == 0
    depth = w_in.shape[0]
    m = batch * seq
    topk = min(TOPK_MAX, seq // 4)
    cos_a, sin_a, cos_b, sin_b = _rope_tables(positions)
    x2 = x.reshape(m, d)
    for l in range(depth):
        xn = _rmsnorm(x2, norm_gain[l])
        h = _in_proj(xn, _relayout_w_in(w_in[l]))
        y_ret = _retention(h, cos_a, sin_a, ret_norm_gain[l], batch, seq)
        qg, k, vt, qi, kidx, w_idx = _dsa_prep(h, cos_b, sin_b, q_norm_gain[l], k_norm_gain[l], batch, seq)
        mask = _dsa_select(qi, w_idx, kidx, topk)
        y_dsa = _dsa_attention(qg, k, vt, mask, h, batch, seq)
        y_gm = _gmlp(h, gm_norm_gain[l], w_spatial[l], b_spatial[l], m)
        merged = _merge((y_ret, y_dsa, y_gm), w_branch[l].astype(jnp.bfloat16), h)
        x2 = _out_proj(merged, w_out[l].astype(jnp.bfloat16), x2)
    return x2.reshape(batch, seq, d)
```

```python
import functools

import numpy as np
import jax
import jax.numpy as jnp
from jax import lax
from jax.experimental import pallas as pl
from jax.experimental.pallas import tpu as pltpu

D_MODEL = 4096
BRANCH_W = D_MODEL // 2
N_BRANCH = 3
RET_HEAD_DIM = 256
RET_HEADS = BRANCH_W // RET_HEAD_DIM
RET_CHUNK = 128
DSA_HEAD_DIM = 128
DSA_HEADS = BRANCH_W // DSA_HEAD_DIM
DSA_KV_HEADS = 4
DSA_GROUP = DSA_HEADS // DSA_KV_HEADS
DSA_KV_W = DSA_KV_HEADS * DSA_HEAD_DIM
IDX_HEADS = 32
IDX_DIM = 128
TOPK_MAX = 256
Q_BLOCK = 128
GM_GROUPS = 16
GM_GROUP_DIM = BRANCH_W // GM_GROUPS
GM_CHUNK = 128
ROPE_THETA = 10000.0
EPS = 1e-6

LANES = 128
VMEM_LIMIT = 56 * 1024 * 1024

_REF_SEGS = (("rq", BRANCH_W), ("rk", BRANCH_W), ("rv", BRANCH_W), ("rg", BRANCH_W),
             ("dq", BRANCH_W), ("dk", DSA_KV_W), ("dv", DSA_KV_W),
             ("iq", IDX_HEADS * IDX_DIM), ("ik", IDX_DIM), ("iw", IDX_HEADS),
             ("dg", BRANCH_W), ("gu", BRANCH_W), ("gv", BRANCH_W), ("gg", BRANCH_W),
             ("mg", N_BRANCH * D_MODEL))
_MY_ORDER = ("dg", "gu", "gv", "gg", "mg", "rq", "rk", "rv", "rg", "dq", "dk", "dv", "iq", "ik", "iw")
N_TILE = 768
IQ_BLOCK = 1024


def _layout():
    ref_off, o = {}, 0
    for name, w in _REF_SEGS:
        ref_off[name] = (o, w)
        o += w
    my_off, o = {}, 0
    for name in _MY_ORDER:
        w = ref_off[name][1]
        my_off[name] = o
        o += -(-w // LANES) * LANES
    total = -(-o // N_TILE) * N_TILE
    return ref_off, my_off, total


_REF_OFF, SEG, N_IN_PAD = _layout()

_NT = (((1,), (1,)), ((), ()))


def _cparams(*sem):
    return pltpu.CompilerParams(dimension_semantics=sem, vmem_limit_bytes=VMEM_LIMIT)


def _rmsnorm_kernel(x_ref, g_ref, o_ref):
    x = x_ref[...]
    ms = jnp.mean(x * x, axis=-1, keepdims=True)
    o_ref[...] = (x * lax.rsqrt(ms + EPS) * g_ref[...]).astype(o_ref.dtype)


def _rmsnorm(x2, gain):
    m, d = x2.shape
    tm = min(256, m)
    return pl.pallas_call(
        _rmsnorm_kernel,
        grid=(m // tm,),
        in_specs=[pl.BlockSpec((tm, d), lambda i: (i, 0)), pl.BlockSpec((1, d), lambda i: (0, 0))],
        out_specs=pl.BlockSpec((tm, d), lambda i: (i, 0)),
        out_shape=jax.ShapeDtypeStruct((m, d), jnp.bfloat16),
        compiler_params=_cparams("parallel"),
        name="rmsnorm",
    )(x2, gain.reshape(1, d))


def _matmul_kernel(a_ref, w_ref, o_ref):
    o_ref[...] = jnp.dot(a_ref[...], w_ref[...], preferred_element_type=jnp.float32).astype(o_ref.dtype)


def _in_proj(xn, w):
    m, k = xn.shape
    n = w.shape[1]
    tm, tn = min(1024, m), N_TILE
    return pl.pallas_call(
        _matmul_kernel,
        grid=(m // tm, n // tn),
        in_specs=[pl.BlockSpec((tm, k), lambda i, j: (i, 0)), pl.BlockSpec((k, tn), lambda i, j: (0, j))],
        out_specs=pl.BlockSpec((tm, tn), lambda i, j: (i, j)),
        out_shape=jax.ShapeDtypeStruct((m, n), jnp.bfloat16),
        compiler_params=_cparams("parallel", "arbitrary"),
        name="in_proj",
    )(xn, w)


def _rope_tables_kernel(pos_ref, inv_a_ref, inv_b_ref, sign_ref, ca_ref, sa_ref, cb_ref, sb_ref):
    p = pos_ref[...].astype(jnp.float32)
    ang_a = p * inv_a_ref[...]
    ca_ref[...] = jnp.cos(ang_a)
    sa_ref[...] = jnp.sin(ang_a)
    ang_b = p * inv_b_ref[...]
    cb_ref[...] = jnp.cos(ang_b)
    sb_ref[...] = jnp.sin(ang_b) * sign_ref[...]


def _rope_tables(positions):
    m = positions.size
    tr = min(1024, m)

    def inv_freq(d):
        half = d // 2
        return 1.0 / (ROPE_THETA ** (jnp.arange(half, dtype=jnp.float32) * 2.0 / d))

    inv_a = inv_freq(RET_HEAD_DIM).reshape(1, LANES)
    inv_b = jnp.tile(inv_freq(DSA_HEAD_DIM), 2).reshape(1, LANES)
    sign = jnp.concatenate([-jnp.ones((LANES // 2,), jnp.float32), jnp.ones((LANES // 2,), jnp.float32)]).reshape(1, LANES)
    row = pl.BlockSpec((1, LANES), lambda i: (0, 0))
    tab = pl.BlockSpec((tr, LANES), lambda i: (i, 0))
    sds = jax.ShapeDtypeStruct((m, LANES), jnp.float32)
    return pl.pallas_call(
        _rope_tables_kernel,
        grid=(m // tr,),
        in_specs=[pl.BlockSpec((tr, 1), lambda i: (i, 0)), row, row, row],
        out_specs=[tab, tab, tab, tab],
        out_shape=[sds, sds, sds, sds],
        compiler_params=_cparams("parallel"),
        name="rope_tables",
    )(positions.reshape(m, 1), inv_a, inv_b, sign)


def _retention_kernel(q_ref, k_ref, v_ref, g_ref, cos_ref, sin_ref, decay_ref, xi_ref, zeta_ref, cd_ref,
                      gain_ref, o_ref, state_ref, *, n_chunk):
    @pl.when(pl.program_id(2) == 0)
    def _():
        state_ref[...] = jnp.zeros_like(state_ref)

    half = RET_HEAD_DIM // 2
    decay = decay_ref[...]
    xi = xi_ref[...]
    zeta = zeta_ref[...]
    cd = cd_ref[0:1, :]
    gain = gain_ref[...]
    k_scale = RET_HEAD_DIM ** -0.5

    def rot(x, cos, sin):
        x1, x2 = x[:, :half], x[:, half:]
        return x1 * cos - x2 * sin, x2 * cos + x1 * sin

    for c in range(n_chunk):
        rows = slice(c * RET_CHUNK, (c + 1) * RET_CHUNK)
        cos, sin = cos_ref[rows, :], sin_ref[rows, :]
        q1, q2 = rot(q_ref[rows, :].astype(jnp.float32), cos, sin)
        k1, k2 = rot(k_ref[rows, :].astype(jnp.float32), cos, sin)
        k1, k2 = k1 * k_scale, k2 * k_scale
        v = v_ref[rows, :]
        qb = jnp.concatenate([q1, q2], axis=1).astype(jnp.bfloat16)
        kb = jnp.concatenate([k1, k2], axis=1).astype(jnp.bfloat16)
        sc = lax.dot_general(qb, kb, _NT, preferred_element_type=jnp.float32) * decay
        inner = jnp.dot(sc.astype(jnp.bfloat16), v, preferred_element_type=jnp.float32)
        state = state_ref[...]
        qx = jnp.concatenate([q1 * xi, q2 * xi], axis=1).astype(jnp.bfloat16)
        cross = jnp.dot(qx, state.astype(jnp.bfloat16), preferred_element_type=jnp.float32)
        out = inner + cross
        kz_t = jnp.concatenate([(k1 * zeta).T, (k2 * zeta).T], axis=0).astype(jnp.bfloat16)
        state_ref[...] = (state * jnp.concatenate([cd, cd], axis=1)
                          + jnp.dot(kz_t, v, preferred_element_type=jnp.float32))
        mu = jnp.mean(out, axis=-1, keepdims=True)
        oc = out - mu
        y = oc * lax.rsqrt(jnp.mean(oc * oc, axis=-1, keepdims=True) + EPS) * gain
        g = g_ref[rows, :].astype(jnp.float32)
        o_ref[rows, :] = (g * jax.nn.sigmoid(g) * y).astype(o_ref.dtype)


def _retention(h, cos_a, sin_a, ret_gain, batch, seq):
    m = batch * seq
    hd = RET_HEAD_DIM
    rb = min(1024, seq)
    nrb = seq // rb
    c = RET_CHUNK
    log_g = jnp.log(1.0 - jnp.power(2.0, -5.0 - jnp.arange(RET_HEADS, dtype=jnp.float32)))
    n = jnp.arange(c, dtype=jnp.float32)
    diff = n[:, None] - n[None, :]
    decay = jnp.where(diff >= 0, jnp.exp(log_g[:, None, None] * jnp.maximum(diff, 0.0)), 0.0)
    xi = jnp.broadcast_to(jnp.exp(log_g[:, None] * (n + 1.0))[:, :, None], (RET_HEADS, c, LANES))
    zeta = jnp.broadcast_to(jnp.exp(log_g[:, None] * (c - 1.0 - n))[:, :, None], (RET_HEADS, c, LANES))
    cd = jnp.broadcast_to(jnp.exp(log_g * c)[:, None, None], (RET_HEADS, 8, LANES))

    def seg(name):
        assert SEG[name] % hd == 0
        base = SEG[name] // hd
        return pl.BlockSpec((rb, hd), lambda b, hh, r: (b * nrb + r, base + hh))

    tab = pl.BlockSpec((rb, LANES), lambda b, hh, r: (b * nrb + r, 0))
    per_head = lambda rows: pl.BlockSpec((None, rows, LANES), lambda b, hh, r: (hh, 0, 0))
    return pl.pallas_call(
        functools.partial(_retention_kernel, n_chunk=rb // c),
        grid=(batch, RET_HEADS, nrb),
        in_specs=[seg("rq"), seg("rk"), seg("rv"), seg("rg"), tab, tab,
                  per_head(c), per_head(c), per_head(c), per_head(8),
                  pl.BlockSpec((1, hd), lambda b, hh, r: (0, hh))],
        out_specs=pl.BlockSpec((rb, hd), lambda b, hh, r: (b * nrb + r, hh)),
        out_shape=jax.ShapeDtypeStruct((m, BRANCH_W), jnp.bfloat16),
        scratch_shapes=[pltpu.VMEM((hd, hd), jnp.float32)],
        compiler_params=_cparams("parallel", "parallel", "arbitrary"),
        name="retention",
    )(h, h, h, h, cos_a, sin_a, decay, xi, zeta, cd, ret_gain.reshape(1, BRANCH_W))


def _dsa_prep_kernel(dq_ref, dk_ref, dv_ref, iq0_ref, iq1_ref, iq2_ref, iq3_ref, ik_ref, iw_ref, cos_ref, sin_ref,
                     qgain_ref, kgain_ref, qg_ref, k_ref, vt_ref, qi_ref, kidx_ref, w_ref):
    iq_refs = (iq0_ref, iq1_ref, iq2_ref, iq3_ref)
    cos, sin = cos_ref[...], sin_ref[...]
    d = DSA_HEAD_DIM

    def rope(x):
        return x * cos + pltpu.roll(x, d // 2, 1) * sin

    def rms(x, g):
        return x * lax.rsqrt(jnp.mean(x * x, axis=-1, keepdims=True) + EPS) * g

    qgain, kgain = qgain_ref[...], kgain_ref[...]
    q_scale = DSA_HEAD_DIM ** -0.5 * np.log2(np.e).astype(np.float32)
    for kvh in range(DSA_KV_HEADS):
        for g in range(DSA_GROUP):
            hh = kvh * DSA_GROUP + g
            x = dq_ref[:, hh * d:(hh + 1) * d].astype(jnp.float32)
            qg_ref[kvh, g * Q_BLOCK:(g + 1) * Q_BLOCK, :] = (rope(rms(x, qgain)) * q_scale).astype(qg_ref.dtype)
        x = dk_ref[:, kvh * d:(kvh + 1) * d].astype(jnp.float32)
        k_ref[kvh] = rope(rms(x, kgain)).astype(k_ref.dtype)
        vt_ref[kvh] = dv_ref[:, kvh * d:(kvh + 1) * d].astype(jnp.float32).T.astype(vt_ref.dtype)
    for hh in range(IDX_HEADS):
        part, col = divmod(hh * IDX_DIM, IQ_BLOCK)
        x = iq_refs[part][:, col:col + IDX_DIM].astype(jnp.float32)
        qi_ref[hh * Q_BLOCK:(hh + 1) * Q_BLOCK, :] = rope(x).astype(qi_ref.dtype)
    kidx_ref[...] = rope(ik_ref[...].astype(jnp.float32)).astype(kidx_ref.dtype)
    w = iw_ref[...].astype(jnp.float32) * (IDX_DIM ** -0.5 * IDX_HEADS ** -0.5)
    w_ref[...] = w.T[:IDX_HEADS, :]


def _dsa_prep(h, cos_b, sin_b, q_gain, k_gain, batch, seq):
    nq = seq // Q_BLOCK
    d = DSA_HEAD_DIM

    def seg(name, width, part=0):
        assert SEG[name] % width == 0
        base = SEG[name] // width + part
        return pl.BlockSpec((Q_BLOCK, width), lambda b, j: (b * nq + j, base))

    tab = pl.BlockSpec((Q_BLOCK, LANES), lambda b, j: (b * nq + j, 0))
    gain = pl.BlockSpec((1, d), lambda b, j: (0, 0))
    bf = jnp.bfloat16
    n_iq = IDX_HEADS * IDX_DIM // IQ_BLOCK
    return pl.pallas_call(
        _dsa_prep_kernel,
        grid=(batch, nq),
        in_specs=[seg("dq", BRANCH_W), seg("dk", DSA_KV_W), seg("dv", DSA_KV_W)]
                 + [seg("iq", IQ_BLOCK, part) for part in range(n_iq)]
                 + [seg("ik", IDX_DIM), seg("iw", LANES), tab, tab, gain, gain],
        out_specs=[
            pl.BlockSpec((None, DSA_KV_HEADS, None, DSA_GROUP * Q_BLOCK, d), lambda b, j: (b, 0, j, 0, 0)),
            pl.BlockSpec((None, DSA_KV_HEADS, Q_BLOCK, d), lambda b, j: (b, 0, j, 0)),
            pl.BlockSpec((None, DSA_KV_HEADS, None, d, Q_BLOCK), lambda b, j: (b, 0, j, 0, 0)),
            pl.BlockSpec((None, None, IDX_HEADS * Q_BLOCK, IDX_DIM), lambda b, j: (b, j, 0, 0)),
            pl.BlockSpec((None, Q_BLOCK, IDX_DIM), lambda b, j: (b, j, 0)),
            pl.BlockSpec((None, None, IDX_HEADS, Q_BLOCK), lambda b, j: (b, j, 0, 0)),
        ],
        out_shape=[
            jax.ShapeDtypeStruct((batch, DSA_KV_HEADS, nq, DSA_GROUP * Q_BLOCK, d), bf),
            jax.ShapeDtypeStruct((batch, DSA_KV_HEADS, seq, d), bf),
            jax.ShapeDtypeStruct((batch, DSA_KV_HEADS, nq, d, Q_BLOCK), bf),
            jax.ShapeDtypeStruct((batch, nq, IDX_HEADS * Q_BLOCK, IDX_DIM), bf),
            jax.ShapeDtypeStruct((batch, seq, IDX_DIM), bf),
            jax.ShapeDtypeStruct((batch, nq, IDX_HEADS, Q_BLOCK), jnp.float32),
        ],
        compiler_params=_cparams("parallel", "parallel"),
        name="dsa_prep",
    )(*([h] * (5 + n_iq)), cos_b, sin_b, q_gain.reshape(1, d), k_gain.reshape(1, d))


def _dsa_select_kernel(qi_ref, w_ref, kidx_ref, mask_ref, keys_ref, *, topk, nq, per):
    j = pl.program_id(1)
    c_ = Q_BLOCK
    tk = per * c_
    sub = lax.broadcasted_iota(jnp.int32, (c_, c_), 0)
    lane = lax.broadcasted_iota(jnp.int32, (c_, c_), 1)
    int_min = jnp.int32(-2 ** 31)
    n_tiles = (j + per) // per

    def causal(chunk):
        return (chunk * c_ + sub) <= (j * c_ + lane)

    def chunk_rows(t, u):
        chunk = t * per + u
        return chunk, pl.ds(pl.multiple_of(chunk * c_, c_), c_)

    def score_tile(t, carry):
        for u in range(per):
            chunk, rows = chunk_rows(t, u)
            kt = kidx_ref[rows, :]
            acc = jnp.zeros((c_, c_), jnp.float32)
            for hg in range(IDX_HEADS // 4):
                s = lax.dot_general(kt, qi_ref[hg * 4 * c_:(hg + 1) * 4 * c_, :], _NT,
                                    preferred_element_type=jnp.float32)
                for hh in range(4):
                    head = hg * 4 + hh
                    acc = acc + jnp.maximum(s[:, hh * c_:(hh + 1) * c_], 0.0) * w_ref[head:head + 1, :]
            bits = lax.bitcast_convert_type(acc, jnp.int32)
            key = bits ^ ((bits >> 31) & jnp.int32(0x7FFFFFFF))
            keys_ref[rows, :] = jnp.where(causal(chunk), key, int_min)
        return carry

    lax.fori_loop(0, n_tiles, score_tile, 0)

    def count(*preds):
        def body(t, accs):
            accs = list(accs)
            for u in range(per):
                _, rows = chunk_rows(t, u)
                kc = keys_ref[rows, :]
                for i, pred in enumerate(preds):
                    hit = jnp.where(pred(kc), 1.0, 0.0)
                    accs[i * per + u] = accs[i * per + u] + jnp.sum(hit.reshape(c_ // 8, 8, c_), axis=0)
            return tuple(accs)
        zero8 = jnp.zeros((8, c_), jnp.float32)
        accs = lax.fori_loop(0, n_tiles, body, (zero8,) * (per * len(preds)))
        return [jnp.sum(sum(accs[i * per:(i + 1) * per]), axis=0, keepdims=True) for i in range(len(preds))]

    kf = jnp.float32(topk)
    zero = jnp.zeros((1, c_), jnp.int32)
    thr = jnp.where(count(lambda kc: kc >= zero)[0] >= kf, zero, int_min)

    def bit_step(i, thr):
        cand = thr | jnp.left_shift(jnp.int32(1), 30 - i)
        return jnp.where(count(lambda kc: kc >= cand)[0] >= kf, cand, thr)

    thr = lax.fori_loop(0, 31, bit_step, thr)
    n_ge, n_gt = count(lambda kc: kc >= thr, lambda kc: kc > thr)
    room = kf - n_gt
    some_tie_left_out = jnp.max(jnp.where((thr > int_min) & (n_ge > kf), 1.0, 0.0)) > 0.0

    @pl.when(jnp.logical_not(some_tie_left_out))
    def _():
        def mask_tile(t, carry):
            for u in range(per):
                chunk, rows = chunk_rows(t, u)
                take = (keys_ref[rows, :] >= thr) & causal(chunk)
                mask_ref[rows, :] = jnp.where(take, 1.0, 0.0).astype(mask_ref.dtype)
            return carry
        lax.fori_loop(0, n_tiles, mask_tile, 0)

    @pl.when(some_tie_left_out)
    def _():
        tri = jnp.where(sub > lane, 1.0, 0.0).astype(jnp.bfloat16)

        def mask_tile(t, seen):
            for u in range(per):
                chunk, rows = chunk_rows(t, u)
                kc = keys_ref[rows, :]
                eq = jnp.where(kc == thr, 1.0, 0.0)
                before = jnp.dot(tri, eq.astype(jnp.bfloat16), preferred_element_type=jnp.float32) + seen
                take = jnp.where(kc > thr, 1.0, jnp.where(before < room, eq, 0.0))
                mask_ref[rows, :] = jnp.where(causal(chunk), take, 0.0).astype(mask_ref.dtype)
                seen = seen + jnp.sum(eq, axis=0, keepdims=True)
            return seen
        lax.fori_loop(0, n_tiles, mask_tile, jnp.zeros((1, c_), jnp.float32))

    def zero_tile(t, carry):
        mask_ref[pl.ds(pl.multiple_of(t * tk, tk), tk), :] = jnp.zeros((tk, c_), mask_ref.dtype)
        return carry

    lax.fori_loop(n_tiles, nq // per, zero_tile, 0)


def _dsa_select(qi, w, kidx, topk):
    batch, nq = qi.shape[0], qi.shape[1]
    seq = kidx.shape[1]
    per = min(4, nq)
    assert nq % per == 0
    return pl.pallas_call(
        functools.partial(_dsa_select_kernel, topk=topk, nq=nq, per=per),
        grid=(batch, nq),
        in_specs=[pl.BlockSpec((None, None, IDX_HEADS * Q_BLOCK, IDX_DIM), lambda b, j: (b, j, 0, 0)),
                  pl.BlockSpec((None, None, IDX_HEADS, Q_BLOCK), lambda b, j: (b, j, 0, 0)),
                  pl.BlockSpec((None, seq, IDX_DIM), lambda b, j: (b, 0, 0))],
        out_specs=pl.BlockSpec((None, seq, Q_BLOCK), lambda b, j: (b, 0, j)),
        out_shape=jax.ShapeDtypeStruct((batch, seq, seq), jnp.bfloat16),
        scratch_shapes=[pltpu.VMEM((seq, Q_BLOCK), jnp.int32)],
        compiler_params=_cparams("parallel", "arbitrary"),
        name="dsa_select",
    )(qi, w, kidx)


def _dsa_attn_kernel(qg_ref, k_ref, vt_ref, mask_ref, dg_ref, o_ref, m_ref, acc_ref, s_ref, p_ref, alpha_ref,
                     *, tk, n_key_tiles):
    j = pl.program_id(1)
    per = tk // Q_BLOCK
    d = DSA_HEAD_DIM
    gw = DSA_GROUP * Q_BLOCK
    heads = range(DSA_KV_HEADS)
    n_live = (j + per) // per
    n_pairs = (n_live + 1) // 2
    neg_inf = jnp.float32(-jnp.inf)
    m_ref[...] = jnp.full_like(m_ref, neg_inf)
    acc_ref[...] = jnp.zeros_like(acc_ref)
    p_ref[:, 1] = jnp.zeros((DSA_KV_HEADS,) + p_ref.shape[2:], p_ref.dtype)
    alpha_ref[:, 1] = jnp.ones((DSA_KV_HEADS,) + alpha_ref.shape[2:], alpha_ref.dtype)
    ones_rows = jnp.ones((acc_ref.shape[1] - d, tk), jnp.bfloat16)

    def logits(kv, t):
        r0 = pl.multiple_of(t * tk, tk)
        return lax.dot_general(k_ref[kv, pl.ds(r0, tk), :], qg_ref[kv], _NT, preferred_element_type=jnp.float32)

    def softmax_stage(kv, sel, slot):
        s = s_ref[kv, slot]
        s = jnp.concatenate([jnp.where(sel, s[:, g * Q_BLOCK:(g + 1) * Q_BLOCK], neg_inf)
                             for g in range(DSA_GROUP)], axis=1)
        m_old = m_ref[kv]
        m_new = jnp.maximum(m_old, jnp.max(s, axis=0, keepdims=True))
        m_safe = jnp.where(m_new == neg_inf, 0.0, m_new)
        p_ref[kv, slot] = jnp.exp2(s - m_safe).astype(p_ref.dtype)
        alpha_ref[kv, slot] = jnp.exp2(m_old - m_safe)
        m_ref[kv] = m_new

    def pv_stage(kv, t, slot):
        vt = jnp.concatenate([vt_ref[kv, t * per + i] for i in range(per)], axis=1)
        lhs = jnp.concatenate([vt, ones_rows], axis=0)
        acc_ref[kv] = acc_ref[kv] * alpha_ref[kv, slot] + jnp.dot(lhs, p_ref[kv, slot],
                                                                  preferred_element_type=jnp.float32)

    for kv in heads:
        s_ref[kv, 0] = logits(kv, 0)

    def pair(i, carry):
        for u in range(2):
            t = 2 * i + u
            r0 = pl.multiple_of(t * tk, tk)
            sel = mask_ref[pl.ds(r0, tk), :].astype(jnp.float32) > 0.5
            for kv in heads:
                s_ref[kv, 1 - u] = logits(kv, jnp.minimum(t + 1, n_key_tiles - 1))
                softmax_stage(kv, sel, u)
                pv_stage(kv, jnp.maximum(t - 1, 0), 1 - u)
        return carry

    lax.fori_loop(0, n_pairs, pair, 0)
    for kv in heads:
        pv_stage(kv, 2 * n_pairs - 1, 1)
        out_t = acc_ref[kv, :d, :] / acc_ref[kv, d:d + 1, :]
        for g in range(DSA_GROUP):
            cols = slice(kv * gw + g * Q_BLOCK, kv * gw + (g + 1) * Q_BLOCK)
            gate = dg_ref[:, cols].astype(jnp.float32)
            o_ref[:, cols] = (gate * jax.nn.sigmoid(gate)
                              * out_t[:, g * Q_BLOCK:(g + 1) * Q_BLOCK].T).astype(o_ref.dtype)


def _dsa_attention(qg, k, vt, mask, h, batch, seq):
    nq = seq // Q_BLOCK
    d = DSA_HEAD_DIM
    gw = DSA_GROUP * d
    tk = min(256, seq // 2)
    assert seq % (2 * tk) == 0
    assert SEG["dg"] % BRANCH_W == 0
    kvh = DSA_KV_HEADS
    once = pl.Buffered(1)
    return pl.pallas_call(
        functools.partial(_dsa_attn_kernel, tk=tk, n_key_tiles=seq // tk),
        grid=(batch, nq),
        in_specs=[pl.BlockSpec((None, kvh, None, gw, d), lambda b, j: (b, 0, j, 0, 0)),
                  pl.BlockSpec((None, kvh, seq, d), lambda b, j: (b, 0, 0, 0), pipeline_mode=once),
                  pl.BlockSpec((None, kvh, nq, d, Q_BLOCK), lambda b, j: (b, 0, 0, 0, 0), pipeline_mode=once),
                  pl.BlockSpec((None, seq, Q_BLOCK), lambda b, j: (b, 0, j)),
                  pl.BlockSpec((Q_BLOCK, BRANCH_W), lambda b, j: (b * nq + j, SEG["dg"] // BRANCH_W))],
        out_specs=pl.BlockSpec((Q_BLOCK, BRANCH_W), lambda b, j: (b * nq + j, 0)),
        out_shape=jax.ShapeDtypeStruct((batch * seq, BRANCH_W), jnp.bfloat16),
        scratch_shapes=[pltpu.VMEM((kvh, 1, gw), jnp.float32),
                        pltpu.VMEM((kvh, d + 16, gw), jnp.float32),
                        pltpu.VMEM((kvh, 2, tk, gw), jnp.float32),
                        pltpu.VMEM((kvh, 2, tk, gw), jnp.bfloat16),
                        pltpu.VMEM((kvh, 2, 1, gw), jnp.float32)],
        compiler_params=_cparams("parallel", "arbitrary"),
        name="dsa_attn",
    )(qg, k, vt, mask, h)


def _gelu_tanh(x):
    return 0.5 * x * (1.0 + jnp.tanh(np.sqrt(2.0 / np.pi).astype(np.float32) * (x + 0.044715 * (x * x * x))))


def _gmlp_kernel(u_ref, v_ref, g_ref, gain_ref, w_ref, b_ref, o_ref):
    c_ = GM_CHUNK
    v = _gelu_tanh(v_ref[...].astype(jnp.float32))
    mu = jnp.mean(v, axis=-1, keepdims=True)
    vc = v - mu
    vn = (vc * lax.rsqrt(jnp.mean(vc * vc, axis=-1, keepdims=True) + EPS) * gain_ref[...]).astype(jnp.bfloat16)
    sub = lax.broadcasted_iota(jnp.int32, (c_, c_), 0)
    lane = lax.broadcasted_iota(jnp.int32, (c_, c_), 1)
    tril = sub >= lane
    for g in range(GM_GROUPS):
        cols = slice(g * GM_GROUP_DIM, (g + 1) * GM_GROUP_DIM)
        w = jnp.where(tril, w_ref[g], jnp.zeros((), w_ref.dtype))
        mixed = jnp.dot(w, vn[:, cols], preferred_element_type=jnp.float32) + b_ref[g]
        u = _gelu_tanh(u_ref[:, cols].astype(jnp.float32))
        gate = g_ref[:, cols].astype(jnp.float32)
        o_ref[:, cols] = (gate * jax.nn.sigmoid(gate) * (u * mixed)).astype(o_ref.dtype)


def _gmlp(h, gm_gain, w_spatial, b_spatial, m):
    c_ = GM_CHUNK
    w = BRANCH_W

    def seg(name):
        assert SEG[name] % w == 0
        base = SEG[name] // w
        return pl.BlockSpec((c_, w), lambda i: (i, base))

    b_b = jnp.broadcast_to(b_spatial[:, :, None], (GM_GROUPS, c_, GM_GROUP_DIM))
    whole = lambda shape: pl.BlockSpec(shape, lambda i: (0,) * len(shape))
    return pl.pallas_call(
        _gmlp_kernel,
        grid=(m // c_,),
        in_specs=[seg("gu"), seg("gv"), seg("gg"), whole((1, w)), whole((GM_GROUPS, c_, c_)),
                  whole((GM_GROUPS, c_, GM_GROUP_DIM))],
        out_specs=pl.BlockSpec((c_, w), lambda i: (i, 0)),
        out_shape=jax.ShapeDtypeStruct((m, w), jnp.bfloat16),
        compiler_params=_cparams("parallel"),
        name="gmlp",
    )(h, h, h, gm_gain.reshape(1, w), w_spatial.astype(jnp.bfloat16), b_b)


def _merge_kernel(y0_ref, y1_ref, y2_ref, w_ref, g0_ref, g1_ref, g2_ref, o_ref):
    acc = None
    for b, (y_ref, g_ref) in enumerate(((y0_ref, g0_ref), (y1_ref, g1_ref), (y2_ref, g2_ref))):
        proj = jnp.dot(y_ref[...], w_ref[b], preferred_element_type=jnp.float32)
        term = jax.nn.sigmoid(g_ref[...].astype(jnp.float32)) * proj
        acc = term if acc is None else acc + term
    o_ref[...] = acc.astype(o_ref.dtype)


def _merge(ys, w_branch, h):
    m = h.shape[0]
    tm, tn = min(1024, m), 512
    ybs = pl.BlockSpec((tm, BRANCH_W), lambda i, j: (i, 0))

    def gate(b):
        base = (SEG["mg"] + b * D_MODEL) // tn
        return pl.BlockSpec((tm, tn), lambda i, j: (i, base + j))

    return pl.pallas_call(
        _merge_kernel,
        grid=(m // tm, D_MODEL // tn),
        in_specs=[ybs, ybs, ybs, pl.BlockSpec((N_BRANCH, BRANCH_W, tn), lambda i, j: (0, 0, j)),
                  gate(0), gate(1), gate(2)],
        out_specs=pl.BlockSpec((tm, tn), lambda i, j: (i, j)),
        out_shape=jax.ShapeDtypeStruct((m, D_MODEL), jnp.bfloat16),
        compiler_params=_cparams("parallel", "arbitrary"),
        name="merge",
    )(*ys, w_branch, h, h, h)


def _out_proj_kernel(a_ref, w_ref, x_ref, o_ref):
    o_ref[...] = x_ref[...] + jnp.dot(a_ref[...], w_ref[...], preferred_element_type=jnp.float32)


def _out_proj(merged, w_out, x2):
    m, d = x2.shape
    tm, tn = min(1024, m), 512
    return pl.pallas_call(
        _out_proj_kernel,
        grid=(m // tm, d // tn),
        in_specs=[pl.BlockSpec((tm, d), lambda i, j: (i, 0)), pl.BlockSpec((d, tn), lambda i, j: (0, j)),
                  pl.BlockSpec((tm, tn), lambda i, j: (i, j))],
        out_specs=pl.BlockSpec((tm, tn), lambda i, j: (i, j)),
        out_shape=jax.ShapeDtypeStruct((m, d), jnp.float32),
        compiler_params=_cparams("parallel", "arbitrary"),
        name="out_proj",
    )(merged, w_out, x2)


def _relayout_w_in(w):
    runs, o = [], 0
    for name in _MY_ORDER:
        off, width = _REF_OFF[name]
        assert SEG[name] == o
        if runs and runs[-1][0] is not None and runs[-1][1] == off:
            runs[-1][1] = off + width
        else:
            runs.append([off, off + width])
        padded = -(-width // LANES) * LANES
        if padded != width:
            runs.append([None, padded - width])
        o += padded
    if N_IN_PAD != o:
        runs.append([None, N_IN_PAD - o])
    parts = [jnp.zeros((w.shape[0], b), jnp.bfloat16) if a is None else w[:, a:b].astype(jnp.bfloat16)
             for a, b in runs]
    return jnp.concatenate(parts, axis=1)


def kernel(x, positions, norm_gain, w_in, ret_norm_gain, q_norm_gain, k_norm_gain, gm_norm_gain, w_spatial,
           b_spatial, w_branch, w_out):
    batch, seq, d = x.shape
    assert d == D_MODEL and seq % Q_BLOCK == 0
    depth = w_in.shape[0]
    m = batch * seq
    topk = min(TOPK_MAX, seq // 4)
    cos_a, sin_a, cos_b, sin_b = _rope_tables(positions)
    x2 = x.reshape(m, d)
    for l in range(depth):
        xn = _rmsnorm(x2, norm_gain[l])
        h = _in_proj(xn, _relayout_w_in(w_in[l]))
        y_ret = _retention(h, cos_a, sin_a, ret_norm_gain[l], batch, seq)
        qg, k, vt, qi, kidx, w_idx = _dsa_prep(h, cos_b, sin_b, q_norm_gain[l], k_norm_gain[l], batch, seq)
        mask = _dsa_select(qi, w_idx, kidx, topk)
        y_dsa = _dsa_attention(qg, k, vt, mask, h, batch, seq)
        y_gm = _gmlp(h, gm_norm_gain[l], w_spatial[l], b_spatial[l], m)
        merged = _merge((y_ret, y_dsa, y_gm), w_branch[l].astype(jnp.bfloat16), h)
        x2 = _out_proj(merged, w_out[l].astype(jnp.bfloat16), x2)
    return x2.reshape(batch, seq, d)
```

```python
import functools

import numpy as np
import jax
import jax.numpy as jnp
from jax import lax
from jax.experimental import pallas as pl
from jax.experimental.pallas import tpu as pltpu

D_MODEL = 4096
BRANCH_W = D_MODEL // 2
N_BRANCH = 3
RET_HEAD_DIM = 256
RET_HEADS = BRANCH_W // RET_HEAD_DIM
RET_CHUNK = 128
DSA_HEAD_DIM = 128
DSA_HEADS = BRANCH_W // DSA_HEAD_DIM
DSA_KV_HEADS = 4
DSA_GROUP = DSA_HEADS // DSA_KV_HEADS
DSA_KV_W = DSA_KV_HEADS * DSA_HEAD_DIM
IDX_HEADS = 32
IDX_DIM = 128
TOPK_MAX = 256
Q_BLOCK = 128
GM_GROUPS = 16
GM_GROUP_DIM = BRANCH_W // GM_GROUPS
GM_CHUNK = 128
ROPE_THETA = 10000.0
EPS = 1e-6

LANES = 128
VMEM_LIMIT = 56 * 1024 * 1024

_REF_SEGS = (("rq", BRANCH_W), ("rk", BRANCH_W), ("rv", BRANCH_W), ("rg", BRANCH_W),
             ("dq", BRANCH_W), ("dk", DSA_KV_W), ("dv", DSA_KV_W),
             ("iq", IDX_HEADS * IDX_DIM), ("ik", IDX_DIM), ("iw", IDX_HEADS),
             ("dg", BRANCH_W), ("gu", BRANCH_W), ("gv", BRANCH_W), ("gg", BRANCH_W),
             ("mg", N_BRANCH * D_MODEL))
_MY_ORDER = ("dg", "gu", "gv", "gg", "mg", "rq", "rk", "rv", "rg", "dq", "dk", "dv", "iq", "ik", "iw")
N_TILE = 768
IQ_BLOCK = 1024


def _layout():
    ref_off, o = {}, 0
    for name, w in _REF_SEGS:
        ref_off[name] = (o, w)
        o += w
    my_off, o = {}, 0
    for name in _MY_ORDER:
        w = ref_off[name][1]
        my_off[name] = o
        o += -(-w // LANES) * LANES
    total = -(-o // N_TILE) * N_TILE
    return ref_off, my_off, total


_REF_OFF, SEG, N_IN_PAD = _layout()

_NT = (((1,), (1,)), ((), ()))


def _cparams(*sem):
    return pltpu.CompilerParams(dimension_semantics=sem, vmem_limit_bytes=VMEM_LIMIT)


def _rmsnorm_kernel(x_ref, g_ref, o_ref):
    x = x_ref[...]
    ms = jnp.mean(x * x, axis=-1, keepdims=True)
    o_ref[...] = (x * lax.rsqrt(ms + EPS) * g_ref[...]).astype(o_ref.dtype)


def _rmsnorm(x2, gain):
    m, d = x2.shape
    tm = min(256, m)
    return pl.pallas_call(
        _rmsnorm_kernel,
        grid=(m // tm,),
        in_specs=[pl.BlockSpec((tm, d), lambda i: (i, 0)), pl.BlockSpec((1, d), lambda i: (0, 0))],
        out_specs=pl.BlockSpec((tm, d), lambda i: (i, 0)),
        out_shape=jax.ShapeDtypeStruct((m, d), jnp.bfloat16),
        compiler_params=_cparams("parallel"),
        name="rmsnorm",
    )(x2, gain.reshape(1, d))


def _matmul_kernel(a_ref, w_ref, o_ref):
    o_ref[...] = jnp.dot(a_ref[...], w_ref[...], preferred_element_type=jnp.float32).astype(o_ref.dtype)


def _in_proj(xn, w):
    m, k = xn.shape
    n = w.shape[1]
    tm, tn = min(1024, m), N_TILE
    return pl.pallas_call(
        _matmul_kernel,
        grid=(m // tm, n // tn),
        in_specs=[pl.BlockSpec((tm, k), lambda i, j: (i, 0)), pl.BlockSpec((k, tn), lambda i, j: (0, j))],
        out_specs=pl.BlockSpec((tm, tn), lambda i, j: (i, j)),
        out_shape=jax.ShapeDtypeStruct((m, n), jnp.bfloat16),
        compiler_params=_cparams("parallel", "arbitrary"),
        name="in_proj",
    )(xn, w)


def _rope_tables_kernel(pos_ref, inv_a_ref, inv_b_ref, sign_ref, ca_ref, sa_ref, cb_ref, sb_ref):
    p = pos_ref[...].astype(jnp.float32)
    ang_a = p * inv_a_ref[...]
    ca_ref[...] = jnp.cos(ang_a)
    sa_ref[...] = jnp.sin(ang_a)
    ang_b = p * inv_b_ref[...]
    cb_ref[...] = jnp.cos(ang_b)
    sb_ref[...] = jnp.sin(ang_b) * sign_ref[...]


def _rope_tables(positions):
    m = positions.size
    tr = min(1024, m)

    def inv_freq(d):
        half = d // 2
        return 1.0 / (ROPE_THETA ** (jnp.arange(half, dtype=jnp.float32) * 2.0 / d))

    inv_a = inv_freq(RET_HEAD_DIM).reshape(1, LANES)
    inv_b = jnp.tile(inv_freq(DSA_HEAD_DIM), 2).reshape(1, LANES)
    sign = jnp.concatenate([-jnp.ones((LANES // 2,), jnp.float32), jnp.ones((LANES // 2,), jnp.float32)]).reshape(1, LANES)
    row = pl.BlockSpec((1, LANES), lambda i: (0, 0))
    tab = pl.BlockSpec((tr, LANES), lambda i: (i, 0))
    sds = jax.ShapeDtypeStruct((m, LANES), jnp.float32)
    return pl.pallas_call(
        _rope_tables_kernel,
        grid=(m // tr,),
        in_specs=[pl.BlockSpec((tr, 1), lambda i: (i, 0)), row, row, row],
        out_specs=[tab, tab, tab, tab],
        out_shape=[sds, sds, sds, sds],
        compiler_params=_cparams("parallel"),
        name="rope_tables",
    )(positions.reshape(m, 1), inv_a, inv_b, sign)


def _retention_kernel(q_ref, k_ref, v_ref, g_ref, cos_ref, sin_ref, decay_ref, xi_ref, zeta_ref, cd_ref,
                      gain_ref, o_ref, state_ref, *, n_chunk):
    @pl.when(pl.program_id(2) == 0)
    def _():
        state_ref[...] = jnp.zeros_like(state_ref)

    half = RET_HEAD_DIM // 2
    decay = decay_ref[...]
    xi = xi_ref[...]
    zeta = zeta_ref[...]
    cd = cd_ref[0:1, :]
    gain = gain_ref[...]
    k_scale = RET_HEAD_DIM ** -0.5

    def rot(x, cos, sin):
        x1, x2 = x[:, :half], x[:, half:]
        return x1 * cos - x2 * sin, x2 * cos + x1 * sin

    for c in range(n_chunk):
        rows = slice(c * RET_CHUNK, (c + 1) * RET_CHUNK)
        cos, sin = cos_ref[rows, :], sin_ref[rows, :]
        q1, q2 = rot(q_ref[rows, :].astype(jnp.float32), cos, sin)
        k1, k2 = rot(k_ref[rows, :].astype(jnp.float32), cos, sin)
        k1, k2 = k1 * k_scale, k2 * k_scale
        v = v_ref[rows, :]
        qb = jnp.concatenate([q1, q2], axis=1).astype(jnp.bfloat16)
        kb = jnp.concatenate([k1, k2], axis=1).astype(jnp.bfloat16)
        sc = lax.dot_general(qb, kb, _NT, preferred_element_type=jnp.float32) * decay
        inner = jnp.dot(sc.astype(jnp.bfloat16), v, preferred_element_type=jnp.float32)
        state = state_ref[...]
        qx = jnp.concatenate([q1 * xi, q2 * xi], axis=1).astype(jnp.bfloat16)
        cross = jnp.dot(qx, state.astype(jnp.bfloat16), preferred_element_type=jnp.float32)
        out = inner + cross
        kz_t = jnp.concatenate([(k1 * zeta).T, (k2 * zeta).T], axis=0).astype(jnp.bfloat16)
        state_ref[...] = (state * jnp.concatenate([cd, cd], axis=1)
                          + jnp.dot(kz_t, v, preferred_element_type=jnp.float32))
        mu = jnp.mean(out, axis=-1, keepdims=True)
        oc = out - mu
        y = oc * lax.rsqrt(jnp.mean(oc * oc, axis=-1, keepdims=True) + EPS) * gain
        g = g_ref[rows, :].astype(jnp.float32)
        o_ref[rows, :] = (g * jax.nn.sigmoid(g) * y).astype(o_ref.dtype)


def _retention(h, cos_a, sin_a, ret_gain, batch, seq):
    m = batch * seq
    hd = RET_HEAD_DIM
    rb = min(1024, seq)
    nrb = seq // rb
    c = RET_CHUNK
    log_g = jnp.log(1.0 - jnp.power(2.0, -5.0 - jnp.arange(RET_HEADS, dtype=jnp.float32)))
    n = jnp.arange(c, dtype=jnp.float32)
    diff = n[:, None] - n[None, :]
    decay = jnp.where(diff >= 0, jnp.exp(log_g[:, None, None] * jnp.maximum(diff, 0.0)), 0.0)
    xi = jnp.broadcast_to(jnp.exp(log_g[:, None] * (n + 1.0))[:, :, None], (RET_HEADS, c, LANES))
    zeta = jnp.broadcast_to(jnp.exp(log_g[:, None] * (c - 1.0 - n))[:, :, None], (RET_HEADS, c, LANES))
    cd = jnp.broadcast_to(jnp.exp(log_g * c)[:, None, None], (RET_HEADS, 8, LANES))

    def seg(name):
        assert SEG[name] % hd == 0
        base = SEG[name] // hd
        return pl.BlockSpec((rb, hd), lambda b, hh, r: (b * nrb + r, base + hh))

    tab = pl.BlockSpec((rb, LANES), lambda b, hh, r: (b * nrb + r, 0))
    per_head = lambda rows: pl.BlockSpec((None, rows, LANES), lambda b, hh, r: (hh, 0, 0))
    return pl.pallas_call(
        functools.partial(_retention_kernel, n_chunk=rb // c),
        grid=(batch, RET_HEADS, nrb),
        in_specs=[seg("rq"), seg("rk"), seg("rv"), seg("rg"), tab, tab,
                  per_head(c), per_head(c), per_head(c), per_head(8),
                  pl.BlockSpec((1, hd), lambda b, hh, r: (0, hh))],
        out_specs=pl.BlockSpec((rb, hd), lambda b, hh, r: (b * nrb + r, hh)),
        out_shape=jax.ShapeDtypeStruct((m, BRANCH_W), jnp.bfloat16),
        scratch_shapes=[pltpu.VMEM((hd, hd), jnp.float32)],
        compiler_params=_cparams("parallel", "parallel", "arbitrary"),
        name="retention",
    )(h, h, h, h, cos_a, sin_a, decay, xi, zeta, cd, ret_gain.reshape(1, BRANCH_W))


def _dsa_prep_kernel(dq_ref, dk_ref, dv_ref, iq0_ref, iq1_ref, iq2_ref, iq3_ref, ik_ref, iw_ref, cos_ref, sin_ref,
                     qgain_ref, kgain_ref, qg_ref, k_ref, vt_ref, qi_ref, kidx_ref, w_ref):
    iq_refs = (iq0_ref, iq1_ref, iq2_ref, iq3_ref)
    cos, sin = cos_ref[...], sin_ref[...]
    d = DSA_HEAD_DIM

    def rope(x):
        return x * cos + pltpu.roll(x, d // 2, 1) * sin

    def rms(x, g):
        return x * lax.rsqrt(jnp.mean(x * x, axis=-1, keepdims=True) + EPS) * g

    qgain, kgain = qgain_ref[...], kgain_ref[...]
    q_scale = DSA_HEAD_DIM ** -0.5 * np.log2(np.e).astype(np.float32)
    for kvh in range(DSA_KV_HEADS):
        for g in range(DSA_GROUP):
            hh = kvh * DSA_GROUP + g
            x = dq_ref[:, hh * d:(hh + 1) * d].astype(jnp.float32)
            qg_ref[kvh, g * Q_BLOCK:(g + 1) * Q_BLOCK, :] = (rope(rms(x, qgain)) * q_scale).astype(qg_ref.dtype)
        x = dk_ref[:, kvh * d:(kvh + 1) * d].astype(jnp.float32)
        k_ref[kvh] = rope(rms(x, kgain)).astype(k_ref.dtype)
        vt_ref[kvh] = dv_ref[:, kvh * d:(kvh + 1) * d].astype(jnp.float32).T.astype(vt_ref.dtype)
    for hh in range(IDX_HEADS):
        part, col = divmod(hh * IDX_DIM, IQ_BLOCK)
        x = iq_refs[part][:, col:col + IDX_DIM].astype(jnp.float32)
        qi_ref[hh * Q_BLOCK:(hh + 1) * Q_BLOCK, :] = rope(x).astype(qi_ref.dtype)
    kidx_ref[...] = rope(ik_ref[...].astype(jnp.float32)).astype(kidx_ref.dtype)
    w = iw_ref[...].astype(jnp.float32) * (IDX_DIM ** -0.5 * IDX_HEADS ** -0.5)
    w_ref[...] = w.T[:IDX_HEADS, :]


def _dsa_prep(h, cos_b, sin_b, q_gain, k_gain, batch, seq):
    nq = seq // Q_BLOCK
    d = DSA_HEAD_DIM

    def seg(name, width, part=0):
        assert SEG[name] % width == 0
        base = SEG[name] // width + part
        return pl.BlockSpec((Q_BLOCK, width), lambda b, j: (b * nq + j, base))

    tab = pl.BlockSpec((Q_BLOCK, LANES), lambda b, j: (b * nq + j, 0))
    gain = pl.BlockSpec((1, d), lambda b, j: (0, 0))
    bf = jnp.bfloat16
    n_iq = IDX_HEADS * IDX_DIM // IQ_BLOCK
    return pl.pallas_call(
        _dsa_prep_kernel,
        grid=(batch, nq),
        in_specs=[seg("dq", BRANCH_W), seg("dk", DSA_KV_W), seg("dv", DSA_KV_W)]
                 + [seg("iq", IQ_BLOCK, part) for part in range(n_iq)]
                 + [seg("ik", IDX_DIM), seg("iw", LANES), tab, tab, gain, gain],
        out_specs=[
            pl.BlockSpec((None, DSA_KV_HEADS, None, DSA_GROUP * Q_BLOCK, d), lambda b, j: (b, 0, j, 0, 0)),
            pl.BlockSpec((None, DSA_KV_HEADS, Q_BLOCK, d), lambda b, j: (b, 0, j, 0)),
            pl.BlockSpec((None, DSA_KV_HEADS, None, d, Q_BLOCK), lambda b, j: (b, 0, j, 0, 0)),
            pl.BlockSpec((None, None, IDX_HEADS * Q_BLOCK, IDX_DIM), lambda b, j: (b, j, 0, 0)),
            pl.BlockSpec((None, Q_BLOCK, IDX_DIM), lambda b, j: (b, j, 0)),
            pl.BlockSpec((None, None, IDX_HEADS, Q_BLOCK), lambda b, j: (b, j, 0, 0)),
        ],
        out_shape=[
            jax.ShapeDtypeStruct((batch, DSA_KV_HEADS, nq, DSA_GROUP * Q_BLOCK, d), bf),
            jax.ShapeDtypeStruct((batch, DSA_KV_HEADS, seq, d), bf),
            jax.ShapeDtypeStruct((batch, DSA_KV_HEADS, nq, d, Q_BLOCK), bf),
            jax.ShapeDtypeStruct((batch, nq, IDX_HEADS * Q_BLOCK, IDX_DIM), bf),
            jax.ShapeDtypeStruct((batch, seq, IDX_DIM), bf),
            jax.ShapeDtypeStruct((batch, nq, IDX_HEADS, Q_BLOCK), jnp.float32),
        ],
        compiler_params=_cparams("parallel", "parallel"),
        name="dsa_prep",
    )(*([h] * (5 + n_iq)), cos_b, sin_b, q_gain.reshape(1, d), k_gain.reshape(1, d))


def _dsa_select_kernel(qi_ref, w_ref, kidx_ref, mask_ref, keys_ref, hi_ref, lo_ref, *, topk, nq, per):
    j = pl.program_id(1)
    c_ = Q_BLOCK
    tk = per * c_
    sub = lax.broadcasted_iota(jnp.int32, (c_, c_), 0)
    lane = lax.broadcasted_iota(jnp.int32, (c_, c_), 1)
    int_min = jnp.int32(-2 ** 31)
    n_tiles = (j + per) // per

    def causal(chunk):
        return (chunk * c_ + sub) <= (j * c_ + lane)

    def chunk_rows(t, u):
        chunk = t * per + u
        return chunk, pl.ds(pl.multiple_of(chunk * c_, c_), c_)

    def score_tile(t, carry):
        for u in range(per):
            chunk, rows = chunk_rows(t, u)
            kt = kidx_ref[rows, :]
            acc = jnp.zeros((c_, c_), jnp.float32)
            for hg in range(IDX_HEADS // 4):
                s = lax.dot_general(kt, qi_ref[hg * 4 * c_:(hg + 1) * 4 * c_, :], _NT,
                                    preferred_element_type=jnp.float32)
                for hh in range(4):
                    head = hg * 4 + hh
                    acc = acc + jnp.maximum(s[:, hh * c_:(hh + 1) * c_], 0.0) * w_ref[head:head + 1, :]
            bits = lax.bitcast_convert_type(acc, jnp.int32)
            key = bits ^ ((bits >> 31) & jnp.int32(0x7FFFFFFF))
            key = jnp.where(causal(chunk), key, int_min)
            keys_ref[rows, :] = key
            hi_ref[rows, :] = (key >> 16).astype(jnp.int16)
            lo_ref[rows, :] = ((key & 0xFFFF) - 32768).astype(jnp.int16)
        return carry

    lax.fori_loop(0, n_tiles, score_tile, 0)

    def count16(ref, pred):
        def body(t, accs):
            accs = list(accs)
            for u in range(per):
                _, rows = chunk_rows(t, u)
                hit = jnp.where(pred(ref[rows, :]), jnp.int16(1), jnp.int16(0)).reshape(c_ // 16, 16, c_)
                accs[u] = accs[u] + functools.reduce(lambda a, b: a + b, [hit[i] for i in range(c_ // 16)])
            return tuple(accs)
        accs = lax.fori_loop(0, n_tiles, body, (jnp.zeros((16, c_), jnp.int16),) * per)
        total = functools.reduce(lambda a, b: a + b, [a.astype(jnp.int32) for a in accs])
        return jnp.sum(total.astype(jnp.float32), axis=0, keepdims=True)

    def kth_largest16(ref, k):
        zero = jnp.zeros((1, c_), jnp.int32)

        def enough(cand):
            cand16 = cand.astype(jnp.int16)
            return count16(ref, lambda x: x >= cand16) >= k

        v = jnp.where(enough(zero), zero, jnp.int32(-2 ** 15))

        def bit_step(i, v):
            cand = v | jnp.left_shift(jnp.int32(1), 14 - i)
            return jnp.where(enough(cand), cand, v)

        return lax.fori_loop(0, 15, bit_step, v)

    def count(*preds):
        def body(t, accs):
            accs = list(accs)
            for u in range(per):
                _, rows = chunk_rows(t, u)
                kc = keys_ref[rows, :]
                for i, pred in enumerate(preds):
                    hit = jnp.where(pred(kc), 1.0, 0.0)
                    accs[i * per + u] = accs[i * per + u] + jnp.sum(hit.reshape(c_ // 8, 8, c_), axis=0)
            return tuple(accs)
        zero8 = jnp.zeros((8, c_), jnp.float32)
        accs = lax.fori_loop(0, n_tiles, body, (zero8,) * (per * len(preds)))
        return [jnp.sum(sum(accs[i * per:(i + 1) * per]), axis=0, keepdims=True) for i in range(len(preds))]

    kf = jnp.float32(topk)
    thr_hi = kth_largest16(hi_ref, kf)
    thr_hi16 = thr_hi.astype(jnp.int16)
    k_lo = kf - count16(hi_ref, lambda x: x > thr_hi16)

    def restrict_lo(t, carry):
        for u in range(per):
            _, rows = chunk_rows(t, u)
            lo_ref[rows, :] = jnp.where(hi_ref[rows, :] == thr_hi16, lo_ref[rows, :], jnp.int16(-2 ** 15))
        return carry

    lax.fori_loop(0, n_tiles, restrict_lo, 0)
    thr_lo = kth_largest16(lo_ref, k_lo)
    thr = jnp.left_shift(thr_hi, 16) | ((thr_lo + 32768) & 0xFFFF)
    n_ge, n_gt = count(lambda kc: kc >= thr, lambda kc: kc > thr)
    room = kf - n_gt
    some_tie_left_out = jnp.max(jnp.where((thr > int_min) & (n_ge > kf), 1.0, 0.0)) > 0.0

    @pl.when(jnp.logical_not(some_tie_left_out))
    def _():
        def mask_tile(t, carry):
            for u in range(per):
                chunk, rows = chunk_rows(t, u)
                take = (keys_ref[rows, :] >= thr) & causal(chunk)
                mask_ref[rows, :] = jnp.where(take, 1.0, 0.0).astype(mask_ref.dtype)
            return carry
        lax.fori_loop(0, n_tiles, mask_tile, 0)

    @pl.when(some_tie_left_out)
    def _():
        tri = jnp.where(sub > lane, 1.0, 0.0).astype(jnp.bfloat16)

        def mask_tile(t, seen):
            for u in range(per):
                chunk, rows = chunk_rows(t, u)
                kc = keys_ref[rows, :]
                eq = jnp.where(kc == thr, 1.0, 0.0)
                before = jnp.dot(tri, eq.astype(jnp.bfloat16), preferred_element_type=jnp.float32) + seen
                take = jnp.where(kc > thr, 1.0, jnp.where(before < room, eq, 0.0))
                mask_ref[rows, :] = jnp.where(causal(chunk), take, 0.0).astype(mask_ref.dtype)
                seen = seen + jnp.sum(eq, axis=0, keepdims=True)
            return seen
        lax.fori_loop(0, n_tiles, mask_tile, jnp.zeros((1, c_), jnp.float32))

    def zero_tile(t, carry):
        mask_ref[pl.ds(pl.multiple_of(t * tk, tk), tk), :] = jnp.zeros((tk, c_), mask_ref.dtype)
        return carry

    lax.fori_loop(n_tiles, nq // per, zero_tile, 0)


def _dsa_select(qi, w, kidx, topk):
    batch, nq = qi.shape[0], qi.shape[1]
    seq = kidx.shape[1]
    per = min(4, nq)
    assert nq % per == 0
    return pl.pallas_call(
        functools.partial(_dsa_select_kernel, topk=topk, nq=nq, per=per),
        grid=(batch, nq),
        in_specs=[pl.BlockSpec((None, None, IDX_HEADS * Q_BLOCK, IDX_DIM), lambda b, j: (b, j, 0, 0)),
                  pl.BlockSpec((None, None, IDX_HEADS, Q_BLOCK), lambda b, j: (b, j, 0, 0)),
                  pl.BlockSpec((None, seq, IDX_DIM), lambda b, j: (b, 0, 0))],
        out_specs=pl.BlockSpec((None, seq, Q_BLOCK), lambda b, j: (b, 0, j)),
        out_shape=jax.ShapeDtypeStruct((batch, seq, seq), jnp.bfloat16),
        scratch_shapes=[pltpu.VMEM((seq, Q_BLOCK), jnp.int32),
                        pltpu.VMEM((seq, Q_BLOCK), jnp.int16),
                        pltpu.VMEM((seq, Q_BLOCK), jnp.int16)],
        compiler_params=_cparams("parallel", "arbitrary"),
        name="dsa_select",
    )(qi, w, kidx)


def _dsa_attn_kernel(qg_ref, k_ref, vt_ref, mask_ref, dg_ref, o_ref, m_ref, acc_ref, s_ref, p_ref, alpha_ref,
                     *, tk, n_key_tiles):
    j = pl.program_id(1)
    per = tk // Q_BLOCK
    d = DSA_HEAD_DIM
    gw = DSA_GROUP * Q_BLOCK
    heads = range(DSA_KV_HEADS)
    n_live = (j + per) // per
    n_pairs = (n_live + 1) // 2
    neg_inf = jnp.float32(-jnp.inf)
    m_ref[...] = jnp.full_like(m_ref, neg_inf)
    acc_ref[...] = jnp.zeros_like(acc_ref)
    p_ref[:, 1] = jnp.zeros((DSA_KV_HEADS,) + p_ref.shape[2:], p_ref.dtype)
    alpha_ref[:, 1] = jnp.ones((DSA_KV_HEADS,) + alpha_ref.shape[2:], alpha_ref.dtype)
    ones_rows = jnp.ones((acc_ref.shape[1] - d, tk), jnp.bfloat16)

    def logits(kv, t):
        r0 = pl.multiple_of(t * tk, tk)
        return lax.dot_general(k_ref[kv, pl.ds(r0, tk), :], qg_ref[kv], _NT, preferred_element_type=jnp.float32)

    def softmax_stage(kv, sel, slot):
        s = s_ref[kv, slot]
        s = jnp.concatenate([jnp.where(sel, s[:, g * Q_BLOCK:(g + 1) * Q_BLOCK], neg_inf)
                             for g in range(DSA_GROUP)], axis=1)
        m_old = m_ref[kv]
        m_new = jnp.maximum(m_old, jnp.max(s, axis=0, keepdims=True))
        m_safe = jnp.where(m_new == neg_inf, 0.0, m_new)
        p_ref[kv, slot] = jnp.exp2(s - m_safe).astype(p_ref.dtype)
        alpha_ref[kv, slot] = jnp.exp2(m_old - m_safe)
        m_ref[kv] = m_new

    def pv_stage(kv, t, slot):
        vt = jnp.concatenate([vt_ref[kv, t * per + i] for i in range(per)], axis=1)
        lhs = jnp.concatenate([vt, ones_rows], axis=0)
        acc_ref[kv] = acc_ref[kv] * alpha_ref[kv, slot] + jnp.dot(lhs, p_ref[kv, slot],
                                                                  preferred_element_type=jnp.float32)

    for kv in heads:
        s_ref[kv, 0] = logits(kv, 0)

    def pair(i, carry):
        for u in range(2):
            t = 2 * i + u
            r0 = pl.multiple_of(t * tk, tk)
            sel = mask_ref[pl.ds(r0, tk), :].astype(jnp.float32) > 0.5
            for kv in heads:
                s_ref[kv, 1 - u] = logits(kv, jnp.minimum(t + 1, n_key_tiles - 1))
                softmax_stage(kv, sel, u)
                pv_stage(kv, jnp.maximum(t - 1, 0), 1 - u)
        return carry

    lax.fori_loop(0, n_pairs, pair, 0)
    for kv in heads:
        pv_stage(kv, 2 * n_pairs - 1, 1)
        out_t = acc_ref[kv, :d, :] / acc_ref[kv, d:d + 1, :]
        for g in range(DSA_GROUP):
            cols = slice(kv * gw + g * Q_BLOCK, kv * gw + (g + 1) * Q_BLOCK)
            gate = dg_ref[:, cols].astype(jnp.float32)
            o_ref[:, cols] = (gate * jax.nn.sigmoid(gate)
                              * out_t[:, g * Q_BLOCK:(g + 1) * Q_BLOCK].T).astype(o_ref.dtype)


def _dsa_attention(qg, k, vt, mask, h, batch, seq):
    nq = seq // Q_BLOCK
    d = DSA_HEAD_DIM
    gw = DSA_GROUP * d
    tk = min(256, seq // 2)
    assert seq % (2 * tk) == 0
    assert SEG["dg"] % BRANCH_W == 0
    kvh = DSA_KV_HEADS
    once = pl.Buffered(1)
    return pl.pallas_call(
        functools.partial(_dsa_attn_kernel, tk=tk, n_key_tiles=seq // tk),
        grid=(batch, nq),
        in_specs=[pl.BlockSpec((None, kvh, None, gw, d), lambda b, j: (b, 0, j, 0, 0)),
                  pl.BlockSpec((None, kvh, seq, d), lambda b, j: (b, 0, 0, 0), pipeline_mode=once),
                  pl.BlockSpec((None, kvh, nq, d, Q_BLOCK), lambda b, j: (b, 0, 0, 0, 0), pipeline_mode=once),
                  pl.BlockSpec((None, seq, Q_BLOCK), lambda b, j: (b, 0, j)),
                  pl.BlockSpec((Q_BLOCK, BRANCH_W), lambda b, j: (b * nq + j, SEG["dg"] // BRANCH_W))],
        out_specs=pl.BlockSpec((Q_BLOCK, BRANCH_W), lambda b, j: (b * nq + j, 0)),
        out_shape=jax.ShapeDtypeStruct((batch * seq, BRANCH_W), jnp.bfloat16),
        scratch_shapes=[pltpu.VMEM((kvh, 1, gw), jnp.float32),
                        pltpu.VMEM((kvh, d + 16, gw), jnp.float32),
                        pltpu.VMEM((kvh, 2, tk, gw), jnp.float32),
                        pltpu.VMEM((kvh, 2, tk, gw), jnp.bfloat16),
                        pltpu.VMEM((kvh, 2, 1, gw), jnp.float32)],
        compiler_params=_cparams("parallel", "arbitrary"),
        name="dsa_attn",
    )(qg, k, vt, mask, h)


def _gelu_tanh(x):
    return 0.5 * x * (1.0 + jnp.tanh(np.sqrt(2.0 / np.pi).astype(np.float32) * (x + 0.044715 * (x * x * x))))


def _gmlp_kernel(u_ref, v_ref, g_ref, gain_ref, w_ref, b_ref, o_ref):
    c_ = GM_CHUNK
    v = _gelu_tanh(v_ref[...].astype(jnp.float32))
    mu = jnp.mean(v, axis=-1, keepdims=True)
    vc = v - mu
    vn = (vc * lax.rsqrt(jnp.mean(vc * vc, axis=-1, keepdims=True) + EPS) * gain_ref[...]).astype(jnp.bfloat16)
    sub = lax.broadcasted_iota(jnp.int32, (c_, c_), 0)
    lane = lax.broadcasted_iota(jnp.int32, (c_, c_), 1)
    tril = sub >= lane
    for g in range(GM_GROUPS):
        cols = slice(g * GM_GROUP_DIM, (g + 1) * GM_GROUP_DIM)
        w = jnp.where(tril, w_ref[g], jnp.zeros((), w_ref.dtype))
        mixed = jnp.dot(w, vn[:, cols], preferred_element_type=jnp.float32) + b_ref[g]
        u = _gelu_tanh(u_ref[:, cols].astype(jnp.float32))
        gate = g_ref[:, cols].astype(jnp.float32)
        o_ref[:, cols] = (gate * jax.nn.sigmoid(gate) * (u * mixed)).astype(o_ref.dtype)


def _gmlp(h, gm_gain, w_spatial, b_spatial, m):
    c_ = GM_CHUNK
    w = BRANCH_W

    def seg(name):
        assert SEG[name] % w == 0
        base = SEG[name] // w
        return pl.BlockSpec((c_, w), lambda i: (i, base))

    b_b = jnp.broadcast_to(b_spatial[:, :, None], (GM_GROUPS, c_, GM_GROUP_DIM))
    whole = lambda shape: pl.BlockSpec(shape, lambda i: (0,) * len(shape))
    return pl.pallas_call(
        _gmlp_kernel,
        grid=(m // c_,),
        in_specs=[seg("gu"), seg("gv"), seg("gg"), whole((1, w)), whole((GM_GROUPS, c_, c_)),
                  whole((GM_GROUPS, c_, GM_GROUP_DIM))],
        out_specs=pl.BlockSpec((c_, w), lambda i: (i, 0)),
        out_shape=jax.ShapeDtypeStruct((m, w), jnp.bfloat16),
        compiler_params=_cparams("parallel"),
        name="gmlp",
    )(h, h, h, gm_gain.reshape(1, w), w_spatial.astype(jnp.bfloat16), b_b)


def _merge_kernel(y0_ref, y1_ref, y2_ref, w_ref, g0_ref, g1_ref, g2_ref, o_ref):
    acc = None
    for b, (y_ref, g_ref) in enumerate(((y0_ref, g0_ref), (y1_ref, g1_ref), (y2_ref, g2_ref))):
        proj = jnp.dot(y_ref[...], w_ref[b], preferred_element_type=jnp.float32)
        term = jax.nn.sigmoid(g_ref[...].astype(jnp.float32)) * proj
        acc = term if acc is None else acc + term
    o_ref[...] = acc.astype(o_ref.dtype)


def _merge(ys, w_branch, h):
    m = h.shape[0]
    tm, tn = min(1024, m), 512
    ybs = pl.BlockSpec((tm, BRANCH_W), lambda i, j: (i, 0))

    def gate(b):
        base = (SEG["mg"] + b * D_MODEL) // tn
        return pl.BlockSpec((tm, tn), lambda i, j: (i, base + j))

    return pl.pallas_call(
        _merge_kernel,
        grid=(m // tm, D_MODEL // tn),
        in_specs=[ybs, ybs, ybs, pl.BlockSpec((N_BRANCH, BRANCH_W, tn), lambda i, j: (0, 0, j)),
                  gate(0), gate(1), gate(2)],
        out_specs=pl.BlockSpec((tm, tn), lambda i, j: (i, j)),
        out_shape=jax.ShapeDtypeStruct((m, D_MODEL), jnp.bfloat16),
        compiler_params=_cparams("parallel", "arbitrary"),
        name="merge",
    )(*ys, w_branch, h, h, h)


def _out_proj_kernel(a_ref, w_ref, x_ref, o_ref):
    o_ref[...] = x_ref[...] + jnp.dot(a_ref[...], w_ref[...], preferred_element_type=jnp.float32)


def _out_proj(merged, w_out, x2):
    m, d = x2.shape
    tm, tn = min(1024, m), 512
    return pl.pallas_call(
        _out_proj_kernel,
        grid=(m // tm, d // tn),
        in_specs=[pl.BlockSpec((tm, d), lambda i, j: (i, 0)), pl.BlockSpec((d, tn), lambda i, j: (0, j)),
                  pl.BlockSpec((tm, tn), lambda i, j: (i, j))],
        out_specs=pl.BlockSpec((tm, tn), lambda i, j: (i, j)),
        out_shape=jax.ShapeDtypeStruct((m, d), jnp.float32),
        compiler_params=_cparams("parallel", "arbitrary"),
        name="out_proj",
    )(merged, w_out, x2)


def _w_in_runs():
    runs, o = [], 0
    for name in _MY_ORDER:
        off, width = _REF_OFF[name]
        assert SEG[name] == o
        if runs and runs[-1][1] is not None and runs[-1][1] + runs[-1][2] == off and runs[-1][0] + runs[-1][2] == o:
            runs[-1][2] += width
        else:
            runs.append([o, off, width])
        padded = -(-width // LANES) * LANES
        if padded != width:
            runs.append([o + width, None, padded - width])
        o += padded
    if N_IN_PAD != o:
        runs.append([o, None, N_IN_PAD - o])
    return runs


def _relayout_kernel(w_ref, o_ref):
    for dst, src, width in _w_in_runs():
        if src is None:
            o_ref[:, dst:dst + width] = jnp.zeros((o_ref.shape[0], width), o_ref.dtype)
        else:
            o_ref[:, dst:dst + width] = w_ref[:, src:src + width].astype(o_ref.dtype)


def _relayout_w_in(w_in, layer):
    _, d, n = w_in.shape
    tr = 64
    return pl.pallas_call(
        _relayout_kernel,
        grid=(d // tr,),
        in_specs=[pl.BlockSpec((None, tr, n), lambda i: (layer, i, 0))],
        out_specs=pl.BlockSpec((tr, N_IN_PAD), lambda i: (i, 0)),
        out_shape=jax.ShapeDtypeStruct((d, N_IN_PAD), jnp.bfloat16),
        compiler_params=_cparams("parallel"),
        name="relayout_w_in",
    )(w_in)


def kernel(x, positions, norm_gain, w_in, ret_norm_gain, q_norm_gain, k_norm_gain, gm_norm_gain, w_spatial,
           b_spatial, w_branch, w_out):
    batch, seq, d = x.shape
    assert d == D_MODEL and seq % Q_BLOCK == 0
    depth = w_in.shape[0]
    m = batch * seq
    topk = min(TOPK_MAX, seq // 4)
    cos_a, sin_a, cos_b, sin_b = _rope_tables(positions)
    x2 = x.reshape(m, d)
    for l in range(depth):
        xn = _rmsnorm(x2, norm_gain[l])
        h = _in_proj(xn, _relayout_w_in(w_in, l))
        y_ret = _retention(h, cos_a, sin_a, ret_norm_gain[l], batch, seq)
        qg, k, vt, qi, kidx, w_idx = _dsa_prep(h, cos_b, sin_b, q_norm_gain[l], k_norm_gain[l], batch, seq)
        mask = _dsa_select(qi, w_idx, kidx, topk)
        y_dsa = _dsa_attention(qg, k, vt, mask, h, batch, seq)
        y_gm = _gmlp(h, gm_norm_gain[l], w_spatial[l], b_spatial[l], m)
        merged = _merge((y_ret, y_dsa, y_gm), w_branch[l].astype(jnp.bfloat16), h)
        x2 = _out_proj(merged, w_out[l].astype(jnp.bfloat16), x2)
    return x2.reshape(batch, seq, d)
```

```python
import functools

import numpy as np
import jax
import jax.numpy as jnp
from jax import lax
from jax.experimental import pallas as pl
from jax.experimental.pallas import tpu as pltpu

D_MODEL = 4096
BRANCH_W = D_MODEL // 2
N_BRANCH = 3
RET_HEAD_DIM = 256
RET_HEADS = BRANCH_W // RET_HEAD_DIM
RET_CHUNK = 128
DSA_HEAD_DIM = 128
DSA_HEADS = BRANCH_W // DSA_HEAD_DIM
DSA_KV_HEADS = 4
DSA_GROUP = DSA_HEADS // DSA_KV_HEADS
DSA_KV_W = DSA_KV_HEADS * DSA_HEAD_DIM
IDX_HEADS = 32
IDX_DIM = 128
TOPK_MAX = 256
Q_BLOCK = 128
GM_GROUPS = 16
GM_GROUP_DIM = BRANCH_W // GM_GROUPS
GM_CHUNK = 128
ROPE_THETA = 10000.0
EPS = 1e-6

LANES = 128
VMEM_LIMIT = 56 * 1024 * 1024

_REF_SEGS = (("rq", BRANCH_W), ("rk", BRANCH_W), ("rv", BRANCH_W), ("rg", BRANCH_W),
             ("dq", BRANCH_W), ("dk", DSA_KV_W), ("dv", DSA_KV_W),
             ("iq", IDX_HEADS * IDX_DIM), ("ik", IDX_DIM), ("iw", IDX_HEADS),
             ("dg", BRANCH_W), ("gu", BRANCH_W), ("gv", BRANCH_W), ("gg", BRANCH_W),
             ("mg", N_BRANCH * D_MODEL))
_MY_ORDER = ("dg", "gu", "gv", "gg", "mg", "rq", "rk", "rv", "rg", "dq", "dk", "dv", "iq", "ik", "iw")
N_TILE = 768
IQ_BLOCK = 1024


def _layout():
    ref_off, o = {}, 0
    for name, w in _REF_SEGS:
        ref_off[name] = (o, w)
        o += w
    my_off, o = {}, 0
    for name in _MY_ORDER:
        w = ref_off[name][1]
        my_off[name] = o
        o += -(-w // LANES) * LANES
    total = -(-o // N_TILE) * N_TILE
    return ref_off, my_off, total


_REF_OFF, SEG, N_IN_PAD = _layout()

_NT = (((1,), (1,)), ((), ()))


def _cparams(*sem):
    return pltpu.CompilerParams(dimension_semantics=sem, vmem_limit_bytes=VMEM_LIMIT)


def _rmsnorm_kernel(x_ref, g_ref, o_ref):
    x = x_ref[...]
    ms = jnp.mean(x * x, axis=-1, keepdims=True)
    o_ref[...] = (x * lax.rsqrt(ms + EPS) * g_ref[...]).astype(o_ref.dtype)


def _rmsnorm(x2, gain):
    m, d = x2.shape
    tm = min(256, m)
    return pl.pallas_call(
        _rmsnorm_kernel,
        grid=(m // tm,),
        in_specs=[pl.BlockSpec((tm, d), lambda i: (i, 0)), pl.BlockSpec((1, d), lambda i: (0, 0))],
        out_specs=pl.BlockSpec((tm, d), lambda i: (i, 0)),
        out_shape=jax.ShapeDtypeStruct((m, d), jnp.bfloat16),
        compiler_params=_cparams("parallel"),
        name="rmsnorm",
    )(x2, gain.reshape(1, d))


def _matmul_kernel(a_ref, w_ref, o_ref):
    o_ref[...] = jnp.dot(a_ref[...], w_ref[...], preferred_element_type=jnp.float32).astype(o_ref.dtype)


def _in_proj(xn, w):
    m, k = xn.shape
    n = w.shape[1]
    tm, tn = min(1024, m), N_TILE
    return pl.pallas_call(
        _matmul_kernel,
        grid=(m // tm, n // tn),
        in_specs=[pl.BlockSpec((tm, k), lambda i, j: (i, 0)), pl.BlockSpec((k, tn), lambda i, j: (0, j))],
        out_specs=pl.BlockSpec((tm, tn), lambda i, j: (i, j)),
        out_shape=jax.ShapeDtypeStruct((m, n), jnp.bfloat16),
        compiler_params=_cparams("parallel", "arbitrary"),
        name="in_proj",
    )(xn, w)


def _rope_tables_kernel(pos_ref, inv_a_ref, inv_b_ref, sign_ref, ca_ref, sa_ref, cb_ref, sb_ref):
    p = pos_ref[...].astype(jnp.float32)
    ang_a = p * inv_a_ref[...]
    ca_ref[...] = jnp.cos(ang_a)
    sa_ref[...] = jnp.sin(ang_a)
    ang_b = p * inv_b_ref[...]
    cb_ref[...] = jnp.cos(ang_b)
    sb_ref[...] = jnp.sin(ang_b) * sign_ref[...]


def _rope_tables(positions):
    m = positions.size
    tr = min(1024, m)

    def inv_freq(d):
        half = d // 2
        return 1.0 / (ROPE_THETA ** (jnp.arange(half, dtype=jnp.float32) * 2.0 / d))

    inv_a = inv_freq(RET_HEAD_DIM).reshape(1, LANES)
    inv_b = jnp.tile(inv_freq(DSA_HEAD_DIM), 2).reshape(1, LANES)
    sign = jnp.concatenate([-jnp.ones((LANES // 2,), jnp.float32), jnp.ones((LANES // 2,), jnp.float32)]).reshape(1, LANES)
    row = pl.BlockSpec((1, LANES), lambda i: (0, 0))
    tab = pl.BlockSpec((tr, LANES), lambda i: (i, 0))
    sds = jax.ShapeDtypeStruct((m, LANES), jnp.float32)
    return pl.pallas_call(
        _rope_tables_kernel,
        grid=(m // tr,),
        in_specs=[pl.BlockSpec((tr, 1), lambda i: (i, 0)), row, row, row],
        out_specs=[tab, tab, tab, tab],
        out_shape=[sds, sds, sds, sds],
        compiler_params=_cparams("parallel"),
        name="rope_tables",
    )(positions.reshape(m, 1), inv_a, inv_b, sign)


def _retention_kernel(q_ref, k_ref, v_ref, g_ref, cos_ref, sin_ref, decay_ref, xi_ref, zeta_ref, cd_ref,
                      gain_ref, o_ref, state_ref, *, n_chunk):
    @pl.when(pl.program_id(2) == 0)
    def _():
        state_ref[...] = jnp.zeros_like(state_ref)

    half = RET_HEAD_DIM // 2
    decay = decay_ref[...]
    xi = xi_ref[...]
    zeta = zeta_ref[...]
    cd = cd_ref[0:1, :]
    gain = gain_ref[...]
    k_scale = RET_HEAD_DIM ** -0.5

    def rot(x, cos, sin):
        x1, x2 = x[:, :half], x[:, half:]
        return x1 * cos - x2 * sin, x2 * cos + x1 * sin

    for c in range(n_chunk):
        rows = slice(c * RET_CHUNK, (c + 1) * RET_CHUNK)
        cos, sin = cos_ref[rows, :], sin_ref[rows, :]
        q1, q2 = rot(q_ref[rows, :].astype(jnp.float32), cos, sin)
        k1, k2 = rot(k_ref[rows, :].astype(jnp.float32), cos, sin)
        k1, k2 = k1 * k_scale, k2 * k_scale
        v = v_ref[rows, :]
        qb = jnp.concatenate([q1, q2], axis=1).astype(jnp.bfloat16)
        kb = jnp.concatenate([k1, k2], axis=1).astype(jnp.bfloat16)
        sc = lax.dot_general(qb, kb, _NT, preferred_element_type=jnp.float32) * decay
        inner = jnp.dot(sc.astype(jnp.bfloat16), v, preferred_element_type=jnp.float32)
        state = state_ref[...]
        qx = jnp.concatenate([q1 * xi, q2 * xi], axis=1).astype(jnp.bfloat16)
        cross = jnp.dot(qx, state.astype(jnp.bfloat16), preferred_element_type=jnp.float32)
        out = inner + cross
        kz_t = jnp.concatenate([(k1 * zeta).T, (k2 * zeta).T], axis=0).astype(jnp.bfloat16)
        state_ref[...] = (state * jnp.concatenate([cd, cd], axis=1)
                          + jnp.dot(kz_t, v, preferred_element_type=jnp.float32))
        mu = jnp.mean(out, axis=-1, keepdims=True)
        oc = out - mu
        y = oc * lax.rsqrt(jnp.mean(oc * oc, axis=-1, keepdims=True) + EPS) * gain
        g = g_ref[rows, :].astype(jnp.float32)
        o_ref[rows, :] = (g * jax.nn.sigmoid(g) * y).astype(o_ref.dtype)


def _retention(h, cos_a, sin_a, ret_gain, batch, seq):
    m = batch * seq
    hd = RET_HEAD_DIM
    rb = min(1024, seq)
    nrb = seq // rb
    c = RET_CHUNK
    log_g = jnp.log(1.0 - jnp.power(2.0, -5.0 - jnp.arange(RET_HEADS, dtype=jnp.float32)))
    n = jnp.arange(c, dtype=jnp.float32)
    diff = n[:, None] - n[None, :]
    decay = jnp.where(diff >= 0, jnp.exp(log_g[:, None, None] * jnp.maximum(diff, 0.0)), 0.0)
    xi = jnp.broadcast_to(jnp.exp(log_g[:, None] * (n + 1.0))[:, :, None], (RET_HEADS, c, LANES))
    zeta = jnp.broadcast_to(jnp.exp(log_g[:, None] * (c - 1.0 - n))[:, :, None], (RET_HEADS, c, LANES))
    cd = jnp.broadcast_to(jnp.exp(log_g * c)[:, None, None], (RET_HEADS, 8, LANES))

    def seg(name):
        assert SEG[name] % hd == 0
        base = SEG[name] // hd
        return pl.BlockSpec((rb, hd), lambda b, hh, r: (b * nrb + r, base + hh))

    tab = pl.BlockSpec((rb, LANES), lambda b, hh, r: (b * nrb + r, 0))
    per_head = lambda rows: pl.BlockSpec((None, rows, LANES), lambda b, hh, r: (hh, 0, 0))
    return pl.pallas_call(
        functools.partial(_retention_kernel, n_chunk=rb // c),
        grid=(batch, RET_HEADS, nrb),
        in_specs=[seg("rq"), seg("rk"), seg("rv"), seg("rg"), tab, tab,
                  per_head(c), per_head(c), per_head(c), per_head(8),
                  pl.BlockSpec((1, hd), lambda b, hh, r: (0, hh))],
        out_specs=pl.BlockSpec((rb, hd), lambda b, hh, r: (b * nrb + r, hh)),
        out_shape=jax.ShapeDtypeStruct((m, BRANCH_W), jnp.bfloat16),
        scratch_shapes=[pltpu.VMEM((hd, hd), jnp.float32)],
        compiler_params=_cparams("parallel", "parallel", "arbitrary"),
        name="retention",
    )(h, h, h, h, cos_a, sin_a, decay, xi, zeta, cd, ret_gain.reshape(1, BRANCH_W))


def _dsa_prep_kernel(dq_ref, dk_ref, dv_ref, iq0_ref, iq1_ref, iq2_ref, iq3_ref, ik_ref, iw_ref, cos_ref, sin_ref,
                     qgain_ref, kgain_ref, qg_ref, k_ref, vt_ref, qi_ref, kidx_ref, w_ref):
    iq_refs = (iq0_ref, iq1_ref, iq2_ref, iq3_ref)
    cos, sin = cos_ref[...], sin_ref[...]
    d = DSA_HEAD_DIM

    def rope(x):
        return x * cos + pltpu.roll(x, d // 2, 1) * sin

    def rms(x, g):
        return x * lax.rsqrt(jnp.mean(x * x, axis=-1, keepdims=True) + EPS) * g

    qgain, kgain = qgain_ref[...], kgain_ref[...]
    q_scale = DSA_HEAD_DIM ** -0.5 * np.log2(np.e).astype(np.float32)
    for kvh in range(DSA_KV_HEADS):
        for g in range(DSA_GROUP):
            hh = kvh * DSA_GROUP + g
            x = dq_ref[:, hh * d:(hh + 1) * d].astype(jnp.float32)
            qg_ref[kvh, g * Q_BLOCK:(g + 1) * Q_BLOCK, :] = (rope(rms(x, qgain)) * q_scale).astype(qg_ref.dtype)
        x = dk_ref[:, kvh * d:(kvh + 1) * d].astype(jnp.float32)
        k_ref[kvh] = rope(rms(x, kgain)).astype(k_ref.dtype)
        vt_ref[kvh] = dv_ref[:, kvh * d:(kvh + 1) * d].astype(jnp.float32).T.astype(vt_ref.dtype)
    for hh in range(IDX_HEADS):
        part, col = divmod(hh * IDX_DIM, IQ_BLOCK)
        x = iq_refs[part][:, col:col + IDX_DIM].astype(jnp.float32)
        qi_ref[hh * Q_BLOCK:(hh + 1) * Q_BLOCK, :] = rope(x).astype(qi_ref.dtype)
    kidx_ref[...] = rope(ik_ref[...].astype(jnp.float32)).astype(kidx_ref.dtype)
    w = iw_ref[...].astype(jnp.float32) * (IDX_DIM ** -0.5 * IDX_HEADS ** -0.5)
    for hh in range(IDX_HEADS):
        w_ref[hh * Q_BLOCK:(hh + 1) * Q_BLOCK, :] = jnp.broadcast_to(w[:, hh:hh + 1], (Q_BLOCK, LANES))


def _dsa_prep(h, cos_b, sin_b, q_gain, k_gain, batch, seq):
    nq = seq // Q_BLOCK
    d = DSA_HEAD_DIM

    def seg(name, width, part=0):
        assert SEG[name] % width == 0
        base = SEG[name] // width + part
        return pl.BlockSpec((Q_BLOCK, width), lambda b, j: (b * nq + j, base))

    tab = pl.BlockSpec((Q_BLOCK, LANES), lambda b, j: (b * nq + j, 0))
    gain = pl.BlockSpec((1, d), lambda b, j: (0, 0))
    bf = jnp.bfloat16
    n_iq = IDX_HEADS * IDX_DIM // IQ_BLOCK
    return pl.pallas_call(
        _dsa_prep_kernel,
        grid=(batch, nq),
        in_specs=[seg("dq", BRANCH_W), seg("dk", DSA_KV_W), seg("dv", DSA_KV_W)]
                 + [seg("iq", IQ_BLOCK, part) for part in range(n_iq)]
                 + [seg("ik", IDX_DIM), seg("iw", LANES), tab, tab, gain, gain],
        out_specs=[
            pl.BlockSpec((None, DSA_KV_HEADS, None, DSA_GROUP * Q_BLOCK, d), lambda b, j: (b, 0, j, 0, 0)),
            pl.BlockSpec((None, DSA_KV_HEADS, Q_BLOCK, d), lambda b, j: (b, 0, j, 0)),
            pl.BlockSpec((None, DSA_KV_HEADS, None, d, Q_BLOCK), lambda b, j: (b, 0, j, 0, 0)),
            pl.BlockSpec((None, None, IDX_HEADS * Q_BLOCK, IDX_DIM), lambda b, j: (b, j, 0, 0)),
            pl.BlockSpec((None, Q_BLOCK, IDX_DIM), lambda b, j: (b, j, 0)),
            pl.BlockSpec((None, None, IDX_HEADS * Q_BLOCK, LANES), lambda b, j: (b, j, 0, 0)),
        ],
        out_shape=[
            jax.ShapeDtypeStruct((batch, DSA_KV_HEADS, nq, DSA_GROUP * Q_BLOCK, d), bf),
            jax.ShapeDtypeStruct((batch, DSA_KV_HEADS, seq, d), bf),
            jax.ShapeDtypeStruct((batch, DSA_KV_HEADS, nq, d, Q_BLOCK), bf),
            jax.ShapeDtypeStruct((batch, nq, IDX_HEADS * Q_BLOCK, IDX_DIM), bf),
            jax.ShapeDtypeStruct((batch, seq, IDX_DIM), bf),
            jax.ShapeDtypeStruct((batch, nq, IDX_HEADS * Q_BLOCK, LANES), jnp.float32),
        ],
        compiler_params=_cparams("parallel", "parallel"),
        name="dsa_prep",
    )(*([h] * (5 + n_iq)), cos_b, sin_b, q_gain.reshape(1, d), k_gain.reshape(1, d))


def _dsa_select_kernel(qi_ref, w_ref, kidx_ref, mask_ref, keys_ref, hi_ref, lo_ref, *, topk, nq, per):
    j = pl.program_id(1)
    c_ = Q_BLOCK
    tk = per * c_
    sub = lax.broadcasted_iota(jnp.int32, (c_, c_), 0)
    lane = lax.broadcasted_iota(jnp.int32, (c_, c_), 1)
    int_min = jnp.int32(-2 ** 31)
    n_tiles = (j + per) // per

    def causal(chunk):
        return (chunk * c_ + sub) <= (j * c_ + lane)

    def chunk_rows(t, u):
        chunk = t * per + u
        return chunk, pl.ds(pl.multiple_of(chunk * c_, c_), c_)

    def store_keys(chunk, key):
        rows = pl.ds(pl.multiple_of(chunk * c_, c_), c_)
        keys_ref[rows, :] = key
        hi_ref[rows, :] = (key >> 16).astype(jnp.int16)
        lo_ref[rows, :] = ((key & 0xFFFF) - 32768).astype(jnp.int16)

    def score_tile(t, carry):
        for u2 in range(per // 2):
            first = t * per + 2 * u2
            kt = kidx_ref[pl.ds(pl.multiple_of(first * c_, 2 * c_), 2 * c_), :]
            s = lax.dot_general(qi_ref[...], kt, _NT, preferred_element_type=jnp.float32)
            acc = [jnp.zeros((c_, c_), jnp.float32), jnp.zeros((c_, c_), jnp.float32)]
            for head in range(IDX_HEADS):
                rows = slice(head * c_, (head + 1) * c_)
                w = w_ref[rows, :]
                for half in range(2):
                    acc[half] = acc[half] + jnp.maximum(s[rows, half * c_:(half + 1) * c_], 0.0) * w
            for half in range(2):
                bits = lax.bitcast_convert_type(acc[half].T, jnp.int32)
                key = bits ^ ((bits >> 31) & jnp.int32(0x7FFFFFFF))
                store_keys(first + half, jnp.where(causal(first + half), key, int_min))
        return carry

    lax.fori_loop(0, n_tiles, score_tile, 0)

    def count16(ref, pred):
        def body(t, accs):
            accs = list(accs)
            for u in range(per):
                _, rows = chunk_rows(t, u)
                hit = jnp.where(pred(ref[rows, :]), jnp.int16(1), jnp.int16(0)).reshape(c_ // 16, 16, c_)
                accs[u] = accs[u] + functools.reduce(lambda a, b: a + b, [hit[i] for i in range(c_ // 16)])
            return tuple(accs)
        accs = lax.fori_loop(0, n_tiles, body, (jnp.zeros((16, c_), jnp.int16),) * per)
        total = functools.reduce(lambda a, b: a + b, [a.astype(jnp.int32) for a in accs])
        return jnp.sum(total.astype(jnp.float32), axis=0, keepdims=True)

    def kth_largest16(ref, k):
        zero = jnp.zeros((1, c_), jnp.int32)

        def enough(cand):
            cand16 = cand.astype(jnp.int16)
            return count16(ref, lambda x: x >= cand16) >= k

        v = jnp.where(enough(zero), zero, jnp.int32(-2 ** 15))

        def bit_step(i, v):
            cand = v | jnp.left_shift(jnp.int32(1), 14 - i)
            return jnp.where(enough(cand), cand, v)

        return lax.fori_loop(0, 15, bit_step, v)

    def count(*preds):
        def body(t, accs):
            accs = list(accs)
            for u in range(per):
                _, rows = chunk_rows(t, u)
                kc = keys_ref[rows, :]
                for i, pred in enumerate(preds):
                    hit = jnp.where(pred(kc), 1.0, 0.0)
                    accs[i * per + u] = accs[i * per + u] + jnp.sum(hit.reshape(c_ // 8, 8, c_), axis=0)
            return tuple(accs)
        zero8 = jnp.zeros((8, c_), jnp.float32)
        accs = lax.fori_loop(0, n_tiles, body, (zero8,) * (per * len(preds)))
        return [jnp.sum(sum(accs[i * per:(i + 1) * per]), axis=0, keepdims=True) for i in range(len(preds))]

    kf = jnp.float32(topk)
    thr_hi = kth_largest16(hi_ref, kf)
    thr_hi16 = thr_hi.astype(jnp.int16)
    k_lo = kf - count16(hi_ref, lambda x: x > thr_hi16)

    def restrict_lo(t, carry):
        for u in range(per):
            _, rows = chunk_rows(t, u)
            lo_ref[rows, :] = jnp.where(hi_ref[rows, :] == thr_hi16, lo_ref[rows, :], jnp.int16(-2 ** 15))
        return carry

    lax.fori_loop(0, n_tiles, restrict_lo, 0)
    thr_lo = kth_largest16(lo_ref, k_lo)
    thr = jnp.left_shift(thr_hi, 16) | ((thr_lo + 32768) & 0xFFFF)
    n_ge, n_gt = count(lambda kc: kc >= thr, lambda kc: kc > thr)
    room = kf - n_gt
    some_tie_left_out = jnp.max(jnp.where((thr > int_min) & (n_ge > kf), 1.0, 0.0)) > 0.0

    @pl.when(jnp.logical_not(some_tie_left_out))
    def _():
        def mask_tile(t, carry):
            for u in range(per):
                chunk, rows = chunk_rows(t, u)
                take = (keys_ref[rows, :] >= thr) & causal(chunk)
                mask_ref[rows, :] = jnp.where(take, 1.0, 0.0).astype(mask_ref.dtype)
            return carry
        lax.fori_loop(0, n_tiles, mask_tile, 0)

    @pl.when(some_tie_left_out)
    def _():
        tri = jnp.where(sub > lane, 1.0, 0.0).astype(jnp.bfloat16)

        def mask_tile(t, seen):
            for u in range(per):
                chunk, rows = chunk_rows(t, u)
                kc = keys_ref[rows, :]
                eq = jnp.where(kc == thr, 1.0, 0.0)
                before = jnp.dot(tri, eq.astype(jnp.bfloat16), preferred_element_type=jnp.float32) + seen
                take = jnp.where(kc > thr, 1.0, jnp.where(before < room, eq, 0.0))
                mask_ref[rows, :] = jnp.where(causal(chunk), take, 0.0).astype(mask_ref.dtype)
                seen = seen + jnp.sum(eq, axis=0, keepdims=True)
            return seen
        lax.fori_loop(0, n_tiles, mask_tile, jnp.zeros((1, c_), jnp.float32))

    def zero_tile(t, carry):
        mask_ref[pl.ds(pl.multiple_of(t * tk, tk), tk), :] = jnp.zeros((tk, c_), mask_ref.dtype)
        return carry

    lax.fori_loop(n_tiles, nq // per, zero_tile, 0)


def _dsa_select(qi, w, kidx, topk):
    batch, nq = qi.shape[0], qi.shape[1]
    seq = kidx.shape[1]
    per = min(4, nq)
    assert nq % per == 0 and per % 2 == 0
    return pl.pallas_call(
        functools.partial(_dsa_select_kernel, topk=topk, nq=nq, per=per),
        grid=(batch, nq),
        in_specs=[pl.BlockSpec((None, None, IDX_HEADS * Q_BLOCK, IDX_DIM), lambda b, j: (b, j, 0, 0)),
                  pl.BlockSpec((None, None, IDX_HEADS * Q_BLOCK, LANES), lambda b, j: (b, j, 0, 0)),
                  pl.BlockSpec((None, seq, IDX_DIM), lambda b, j: (b, 0, 0))],
        out_specs=pl.BlockSpec((None, seq, Q_BLOCK), lambda b, j: (b, 0, j)),
        out_shape=jax.ShapeDtypeStruct((batch, seq, seq), jnp.bfloat16),
        scratch_shapes=[pltpu.VMEM((seq, Q_BLOCK), jnp.int32),
                        pltpu.VMEM((seq, Q_BLOCK), jnp.int16),
                        pltpu.VMEM((seq, Q_BLOCK), jnp.int16)],
        compiler_params=_cparams("parallel", "arbitrary"),
        name="dsa_select",
    )(qi, w, kidx)


def _dsa_attn_kernel(qg_ref, k_ref, vt_ref, mask_ref, dg_ref, o_ref, m_ref, acc_ref, s_ref, p_ref, alpha_ref,
                     *, tk, n_key_tiles):
    j = pl.program_id(1)
    per = tk // Q_BLOCK
    d = DSA_HEAD_DIM
    gw = DSA_GROUP * Q_BLOCK
    heads = range(DSA_KV_HEADS)
    n_live = (j + per) // per
    n_pairs = (n_live + 1) // 2
    neg_inf = jnp.float32(-jnp.inf)
    m_ref[...] = jnp.full_like(m_ref, neg_inf)
    acc_ref[...] = jnp.zeros_like(acc_ref)
    p_ref[:, 1] = jnp.zeros((DSA_KV_HEADS,) + p_ref.shape[2:], p_ref.dtype)
    alpha_ref[:, 1] = jnp.ones((DSA_KV_HEADS,) + alpha_ref.shape[2:], alpha_ref.dtype)
    ones_rows = jnp.ones((acc_ref.shape[1] - d, tk), jnp.bfloat16)

    def logits(kv, t):
        r0 = pl.multiple_of(t * tk, tk)
        return lax.dot_general(k_ref[kv, pl.ds(r0, tk), :], qg_ref[kv], _NT, preferred_element_type=jnp.float32)

    def softmax_stage(kv, sel, slot):
        s = s_ref[kv, slot]
        s = jnp.concatenate([jnp.where(sel, s[:, g * Q_BLOCK:(g + 1) * Q_BLOCK], neg_inf)
                             for g in range(DSA_GROUP)], axis=1)
        m_old = m_ref[kv]
        m_new = jnp.maximum(m_old, jnp.max(s, axis=0, keepdims=True))
        m_safe = jnp.where(m_new == neg_inf, 0.0, m_new)
        p_ref[kv, slot] = jnp.exp2(s - m_safe).astype(p_ref.dtype)
        alpha_ref[kv, slot] = jnp.exp2(m_old - m_safe)
        m_ref[kv] = m_new

    def pv_stage(kv, t, slot):
        vt = jnp.concatenate([vt_ref[kv, t * per + i] for i in range(per)], axis=1)
        lhs = jnp.concatenate([vt, ones_rows], axis=0)
        acc_ref[kv] = acc_ref[kv] * alpha_ref[kv, slot] + jnp.dot(lhs, p_ref[kv, slot],
                                                                  preferred_element_type=jnp.float32)

    for kv in heads:
        s_ref[kv, 0] = logits(kv, 0)

    def pair(i, carry):
        for u in range(2):
            t = 2 * i + u
            r0 = pl.multiple_of(t * tk, tk)
            sel = mask_ref[pl.ds(r0, tk), :].astype(jnp.float32) > 0.5
            for kv in heads:
                s_ref[kv, 1 - u] = logits(kv, jnp.minimum(t + 1, n_key_tiles - 1))
                softmax_stage(kv, sel, u)
                pv_stage(kv, jnp.maximum(t - 1, 0), 1 - u)
        return carry

    lax.fori_loop(0, n_pairs, pair, 0)
    for kv in heads:
        pv_stage(kv, 2 * n_pairs - 1, 1)
        out_t = acc_ref[kv, :d, :] / acc_ref[kv, d:d + 1, :]
        for g in range(DSA_GROUP):
            cols = slice(kv * gw + g * Q_BLOCK, kv * gw + (g + 1) * Q_BLOCK)
            gate = dg_ref[:, cols].astype(jnp.float32)
            o_ref[:, cols] = (gate * jax.nn.sigmoid(gate)
                              * out_t[:, g * Q_BLOCK:(g + 1) * Q_BLOCK].T).astype(o_ref.dtype)


def _dsa_attention(qg, k, vt, mask, h, batch, seq):
    nq = seq // Q_BLOCK
    d = DSA_HEAD_DIM
    gw = DSA_GROUP * d
    tk = min(256, seq // 2)
    assert seq % (2 * tk) == 0
    assert SEG["dg"] % BRANCH_W == 0
    kvh = DSA_KV_HEADS
    once = pl.Buffered(1)
    return pl.pallas_call(
        functools.partial(_dsa_attn_kernel, tk=tk, n_key_tiles=seq // tk),
        grid=(batch, nq),
        in_specs=[pl.BlockSpec((None, kvh, None, gw, d), lambda b, j: (b, 0, j, 0, 0)),
                  pl.BlockSpec((None, kvh, seq, d), lambda b, j: (b, 0, 0, 0), pipeline_mode=once),
                  pl.BlockSpec((None, kvh, nq, d, Q_BLOCK), lambda b, j: (b, 0, 0, 0, 0), pipeline_mode=once),
                  pl.BlockSpec((None, seq, Q_BLOCK), lambda b, j: (b, 0, j)),
                  pl.BlockSpec((Q_BLOCK, BRANCH_W), lambda b, j: (b * nq + j, SEG["dg"] // BRANCH_W))],
        out_specs=pl.BlockSpec((Q_BLOCK, BRANCH_W), lambda b, j: (b * nq + j, 0)),
        out_shape=jax.ShapeDtypeStruct((batch * seq, BRANCH_W), jnp.bfloat16),
        scratch_shapes=[pltpu.VMEM((kvh, 1, gw), jnp.float32),
                        pltpu.VMEM((kvh, d + 16, gw), jnp.float32),
                        pltpu.VMEM((kvh, 2, tk, gw), jnp.float32),
                        pltpu.VMEM((kvh, 2, tk, gw), jnp.bfloat16),
                        pltpu.VMEM((kvh, 2, 1, gw), jnp.float32)],
        compiler_params=_cparams("parallel", "arbitrary"),
        name="dsa_attn",
    )(qg, k, vt, mask, h)


def _gelu_tanh(x):
    return 0.5 * x * (1.0 + jnp.tanh(np.sqrt(2.0 / np.pi).astype(np.float32) * (x + 0.044715 * (x * x * x))))


def _gmlp_kernel(u_ref, v_ref, g_ref, gain_ref, w_ref, b_ref, o_ref):
    c_ = GM_CHUNK
    v = _gelu_tanh(v_ref[...].astype(jnp.float32))
    mu = jnp.mean(v, axis=-1, keepdims=True)
    vc = v - mu
    vn = (vc * lax.rsqrt(jnp.mean(vc * vc, axis=-1, keepdims=True) + EPS) * gain_ref[...]).astype(jnp.bfloat16)
    sub = lax.broadcasted_iota(jnp.int32, (c_, c_), 0)
    lane = lax.broadcasted_iota(jnp.int32, (c_, c_), 1)
    tril = sub >= lane
    for g in range(GM_GROUPS):
        cols = slice(g * GM_GROUP_DIM, (g + 1) * GM_GROUP_DIM)
        w = jnp.where(tril, w_ref[g], jnp.zeros((), w_ref.dtype))
        mixed = jnp.dot(w, vn[:, cols], preferred_element_type=jnp.float32) + b_ref[g]
        u = _gelu_tanh(u_ref[:, cols].astype(jnp.float32))
        gate = g_ref[:, cols].astype(jnp.float32)
        o_ref[:, cols] = (gate * jax.nn.sigmoid(gate) * (u * mixed)).astype(o_ref.dtype)


def _gmlp(h, gm_gain, w_spatial, b_spatial, m):
    c_ = GM_CHUNK
    w = BRANCH_W

    def seg(name):
        assert SEG[name] % w == 0
        base = SEG[name] // w
        return pl.BlockSpec((c_, w), lambda i: (i, base))

    b_b = jnp.broadcast_to(b_spatial[:, :, None], (GM_GROUPS, c_, GM_GROUP_DIM))
    whole = lambda shape: pl.BlockSpec(shape, lambda i: (0,) * len(shape))
    return pl.pallas_call(
        _gmlp_kernel,
        grid=(m // c_,),
        in_specs=[seg("gu"), seg("gv"), seg("gg"), whole((1, w)), whole((GM_GROUPS, c_, c_)),
                  whole((GM_GROUPS, c_, GM_GROUP_DIM))],
        out_specs=pl.BlockSpec((c_, w), lambda i: (i, 0)),
        out_shape=jax.ShapeDtypeStruct((m, w), jnp.bfloat16),
        compiler_params=_cparams("parallel"),
        name="gmlp",
    )(h, h, h, gm_gain.reshape(1, w), w_spatial.astype(jnp.bfloat16), b_b)


def _merge_kernel(y0_ref, y1_ref, y2_ref, w_ref, g0_ref, g1_ref, g2_ref, o_ref):
    acc = None
    for b, (y_ref, g_ref) in enumerate(((y0_ref, g0_ref), (y1_ref, g1_ref), (y2_ref, g2_ref))):
        proj = jnp.dot(y_ref[...], w_ref[b], preferred_element_type=jnp.float32)
        term = jax.nn.sigmoid(g_ref[...].astype(jnp.float32)) * proj
        acc = term if acc is None else acc + term
    o_ref[...] = acc.astype(o_ref.dtype)


def _merge(ys, w_branch, h):
    m = h.shape[0]
    tm, tn = min(1024, m), 512
    ybs = pl.BlockSpec((tm, BRANCH_W), lambda i, j: (i, 0))

    def gate(b):
        base = (SEG["mg"] + b * D_MODEL) // tn
        return pl.BlockSpec((tm, tn), lambda i, j: (i, base + j))

    return pl.pallas_call(
        _merge_kernel,
        grid=(m // tm, D_MODEL // tn),
        in_specs=[ybs, ybs, ybs, pl.BlockSpec((N_BRANCH, BRANCH_W, tn), lambda i, j: (0, 0, j)),
                  gate(0), gate(1), gate(2)],
        out_specs=pl.BlockSpec((tm, tn), lambda i, j: (i, j)),
        out_shape=jax.ShapeDtypeStruct((m, D_MODEL), jnp.bfloat16),
        compiler_params=_cparams("parallel", "arbitrary"),
        name="merge",
    )(*ys, w_branch, h, h, h)


def _out_proj_kernel(a_ref, w_ref, x_ref, o_ref):
    o_ref[...] = x_ref[...] + jnp.dot(a_ref[...], w_ref[...], preferred_element_type=jnp.float32)


def _out_proj(merged, w_out, x2):
    m, d = x2.shape
    tm, tn = min(1024, m), 512
    return pl.pallas_call(
        _out_proj_kernel,
        grid=(m // tm, d // tn),
        in_specs=[pl.BlockSpec((tm, d), lambda i, j: (i, 0)), pl.BlockSpec((d, tn), lambda i, j: (0, j)),
                  pl.BlockSpec((tm, tn), lambda i, j: (i, j))],
        out_specs=pl.BlockSpec((tm, tn), lambda i, j: (i, j)),
        out_shape=jax.ShapeDtypeStruct((m, d), jnp.float32),
        compiler_params=_cparams("parallel", "arbitrary"),
        name="out_proj",
    )(merged, w_out, x2)


def _w_in_runs():
    runs, o = [], 0
    for name in _MY_ORDER:
        off, width = _REF_OFF[name]
        assert SEG[name] == o
        if runs and runs[-1][1] is not None and runs[-1][1] + runs[-1][2] == off and runs[-1][0] + runs[-1][2] == o:
            runs[-1][2] += width
        else:
            runs.append([o, off, width])
        padded = -(-width // LANES) * LANES
        if padded != width:
            runs.append([o + width, None, padded - width])
        o += padded
    if N_IN_PAD != o:
        runs.append([o, None, N_IN_PAD - o])
    return runs


def _relayout_plan():
    src_of = np.full((N_IN_PAD,), -1, np.int64)
    for dst, src, width in _w_in_runs():
        if src is not None:
            src_of[dst:dst + width] = np.arange(src, src + width)
    first, valid = [], []
    for c in range(N_IN_PAD // LANES):
        cols = src_of[c * LANES:(c + 1) * LANES]
        n = int((cols >= 0).sum())
        assert n > 0 and (cols[:n] == cols[0] + np.arange(n)).all() and (cols[n:] < 0).all()
        first.append(int(cols[0]))
        valid.append(n)
    return np.asarray(first, np.int32), np.asarray(valid, np.int32)


def _relayout_kernel(first_ref, valid_ref, wt_ref, o_ref):
    x = wt_ref[0].T
    lane = lax.broadcasted_iota(jnp.int32, x.shape, 1)
    o_ref[...] = jnp.where(lane < valid_ref[pl.program_id(0)], x, 0.0).astype(o_ref.dtype)


def _relayout_w_in(w_in, layer):
    _, d, n = w_in.shape
    first, valid = _relayout_plan()
    assert int((first + LANES).max()) <= n
    wt = jnp.swapaxes(w_in, 1, 2)
    return pl.pallas_call(
        _relayout_kernel,
        grid_spec=pltpu.PrefetchScalarGridSpec(
            num_scalar_prefetch=2,
            grid=(N_IN_PAD // LANES,),
            in_specs=[pl.BlockSpec((pl.Element(1), pl.Element(LANES), pl.Element(d)),
                                   lambda c, first, valid: (layer, pl.multiple_of(first[c], 8), 0))],
            out_specs=pl.BlockSpec((d, LANES), lambda c, first, valid: (0, c)),
        ),
        out_shape=jax.ShapeDtypeStruct((d, N_IN_PAD), jnp.bfloat16),
        compiler_params=_cparams("parallel"),
        name="relayout_w_in",
    )(jnp.asarray(first), jnp.asarray(valid), wt)


def kernel(x, positions, norm_gain, w_in, ret_norm_gain, q_norm_gain, k_norm_gain, gm_norm_gain, w_spatial,
           b_spatial, w_branch, w_out):
    batch, seq, d = x.shape
    assert d == D_MODEL and seq % Q_BLOCK == 0
    depth = w_in.shape[0]
    m = batch * seq
    topk = min(TOPK_MAX, seq // 4)
    cos_a, sin_a, cos_b, sin_b = _rope_tables(positions)
    x2 = x.reshape(m, d)
    for l in range(depth):
        xn = _rmsnorm(x2, norm_gain[l])
        h = _in_proj(xn, _relayout_w_in(w_in, l))
        y_ret = _retention(h, cos_a, sin_a, ret_norm_gain[l], batch, seq)
        qg, k, vt, qi, kidx, w_idx = _dsa_prep(h, cos_b, sin_b, q_norm_gain[l], k_norm_gain[l], batch, seq)
        mask = _dsa_select(qi, w_idx, kidx, topk)
        y_dsa = _dsa_attention(qg, k, vt, mask, h, batch, seq)
        y_gm = _gmlp(h, gm_norm_gain[l], w_spatial[l], b_spatial[l], m)
        merged = _merge((y_ret, y_dsa, y_gm), w_branch[l].astype(jnp.bfloat16), h)
        x2 = _out_proj(merged, w_out[l].astype(jnp.bfloat16), x2)
    return x2.reshape(batch, seq, d)
```

```python
import functools

import numpy as np
import jax
import jax.numpy as jnp
from jax import lax
from jax.experimental import pallas as pl
from jax.experimental.pallas import tpu as pltpu

D_MODEL = 4096
BRANCH_W = D_MODEL // 2
N_BRANCH = 3
RET_HEAD_DIM = 256
RET_HEADS = BRANCH_W // RET_HEAD_DIM
RET_CHUNK = 128
DSA_HEAD_DIM = 128
DSA_HEADS = BRANCH_W // DSA_HEAD_DIM
DSA_KV_HEADS = 4
DSA_GROUP = DSA_HEADS // DSA_KV_HEADS
DSA_KV_W = DSA_KV_HEADS * DSA_HEAD_DIM
IDX_HEADS = 32
IDX_DIM = 128
TOPK_MAX = 256
Q_BLOCK = 128
GM_GROUPS = 16
GM_GROUP_DIM = BRANCH_W // GM_GROUPS
GM_CHUNK = 128
ROPE_THETA = 10000.0
EPS = 1e-6

LANES = 128
VMEM_LIMIT = 56 * 1024 * 1024

_REF_SEGS = (("rq", BRANCH_W), ("rk", BRANCH_W), ("rv", BRANCH_W), ("rg", BRANCH_W),
             ("dq", BRANCH_W), ("dk", DSA_KV_W), ("dv", DSA_KV_W),
             ("iq", IDX_HEADS * IDX_DIM), ("ik", IDX_DIM), ("iw", IDX_HEADS),
             ("dg", BRANCH_W), ("gu", BRANCH_W), ("gv", BRANCH_W), ("gg", BRANCH_W),
             ("mg", N_BRANCH * D_MODEL))
_MY_ORDER = ("dg", "gu", "gv", "gg", "mg", "rq", "rk", "rv", "rg", "dq", "dk", "dv", "iq", "ik", "iw")
N_TILE = 768
IQ_BLOCK = 1024


def _layout():
    ref_off, o = {}, 0
    for name, w in _REF_SEGS:
        ref_off[name] = (o, w)
        o += w
    my_off, o = {}, 0
    for name in _MY_ORDER:
        w = ref_off[name][1]
        my_off[name] = o
        o += -(-w // LANES) * LANES
    total = -(-o // N_TILE) * N_TILE
    return ref_off, my_off, total


_REF_OFF, SEG, N_IN_PAD = _layout()

_NT = (((1,), (1,)), ((), ()))


def _cparams(*sem):
    return pltpu.CompilerParams(dimension_semantics=sem, vmem_limit_bytes=VMEM_LIMIT)


def _rmsnorm_kernel(x_ref, g_ref, o_ref):
    x = x_ref[...]
    ms = jnp.mean(x * x, axis=-1, keepdims=True)
    o_ref[...] = (x * lax.rsqrt(ms + EPS) * g_ref[...]).astype(o_ref.dtype)


def _rmsnorm(x2, gain):
    m, d = x2.shape
    tm = min(256, m)
    return pl.pallas_call(
        _rmsnorm_kernel,
        grid=(m // tm,),
        in_specs=[pl.BlockSpec((tm, d), lambda i: (i, 0)), pl.BlockSpec((1, d), lambda i: (0, 0))],
        out_specs=pl.BlockSpec((tm, d), lambda i: (i, 0)),
        out_shape=jax.ShapeDtypeStruct((m, d), jnp.bfloat16),
        compiler_params=_cparams("parallel"),
        name="rmsnorm",
    )(x2, gain.reshape(1, d))


def _matmul_kernel(a_ref, w_ref, o_ref):
    o_ref[...] = jnp.dot(a_ref[...], w_ref[...], preferred_element_type=jnp.float32).astype(o_ref.dtype)


def _in_proj(xn, w):
    m, k = xn.shape
    n = w.shape[1]
    tm, tn = min(1024, m), N_TILE
    return pl.pallas_call(
        _matmul_kernel,
        grid=(m // tm, n // tn),
        in_specs=[pl.BlockSpec((tm, k), lambda i, j: (i, 0)), pl.BlockSpec((k, tn), lambda i, j: (0, j))],
        out_specs=pl.BlockSpec((tm, tn), lambda i, j: (i, j)),
        out_shape=jax.ShapeDtypeStruct((m, n), jnp.bfloat16),
        compiler_params=_cparams("parallel", "arbitrary"),
        name="in_proj",
    )(xn, w)


def _rope_tables_kernel(pos_ref, inv_a_ref, inv_b_ref, sign_ref, ca_ref, sa_ref, cb_ref, sb_ref):
    p = pos_ref[...].astype(jnp.float32)
    ang_a = p * inv_a_ref[...]
    ca_ref[...] = jnp.cos(ang_a)
    sa_ref[...] = jnp.sin(ang_a)
    ang_b = p * inv_b_ref[...]
    cb_ref[...] = jnp.cos(ang_b)
    sb_ref[...] = jnp.sin(ang_b) * sign_ref[...]


def _rope_tables(positions):
    m = positions.size
    tr = min(1024, m)

    def inv_freq(d):
        half = d // 2
        return 1.0 / (ROPE_THETA ** (jnp.arange(half, dtype=jnp.float32) * 2.0 / d))

    inv_a = inv_freq(RET_HEAD_DIM).reshape(1, LANES)
    inv_b = jnp.tile(inv_freq(DSA_HEAD_DIM), 2).reshape(1, LANES)
    sign = jnp.concatenate([-jnp.ones((LANES // 2,), jnp.float32), jnp.ones((LANES // 2,), jnp.float32)]).reshape(1, LANES)
    row = pl.BlockSpec((1, LANES), lambda i: (0, 0))
    tab = pl.BlockSpec((tr, LANES), lambda i: (i, 0))
    sds = jax.ShapeDtypeStruct((m, LANES), jnp.float32)
    return pl.pallas_call(
        _rope_tables_kernel,
        grid=(m // tr,),
        in_specs=[pl.BlockSpec((tr, 1), lambda i: (i, 0)), row, row, row],
        out_specs=[tab, tab, tab, tab],
        out_shape=[sds, sds, sds, sds],
        compiler_params=_cparams("parallel"),
        name="rope_tables",
    )(positions.reshape(m, 1), inv_a, inv_b, sign)


def _retention_kernel(q_ref, k_ref, v_ref, g_ref, cos_ref, sin_ref, decay_ref, xi_ref, zeta_ref, cd_ref,
                      gain_ref, o_ref, state_ref, *, n_chunk):
    @pl.when(pl.program_id(2) == 0)
    def _():
        state_ref[...] = jnp.zeros_like(state_ref)

    half = RET_HEAD_DIM // 2
    decay = decay_ref[...]
    xi = xi_ref[...]
    zeta = zeta_ref[...]
    cd = cd_ref[0:1, :]
    gain = gain_ref[...]
    k_scale = RET_HEAD_DIM ** -0.5

    def rot(x, cos, sin):
        x1, x2 = x[:, :half], x[:, half:]
        return x1 * cos - x2 * sin, x2 * cos + x1 * sin

    for c in range(n_chunk):
        rows = slice(c * RET_CHUNK, (c + 1) * RET_CHUNK)
        cos, sin = cos_ref[rows, :], sin_ref[rows, :]
        q1, q2 = rot(q_ref[rows, :].astype(jnp.float32), cos, sin)
        k1, k2 = rot(k_ref[rows, :].astype(jnp.float32), cos, sin)
        k1, k2 = k1 * k_scale, k2 * k_scale
        v = v_ref[rows, :]
        qb = jnp.concatenate([q1, q2], axis=1).astype(jnp.bfloat16)
        kb = jnp.concatenate([k1, k2], axis=1).astype(jnp.bfloat16)
        sc = lax.dot_general(qb, kb, _NT, preferred_element_type=jnp.float32) * decay
        inner = jnp.dot(sc.astype(jnp.bfloat16), v, preferred_element_type=jnp.float32)
        state = state_ref[...]
        qx = jnp.concatenate([q1 * xi, q2 * xi], axis=1).astype(jnp.bfloat16)
        cross = jnp.dot(qx, state.astype(jnp.bfloat16), preferred_element_type=jnp.float32)
        out = inner + cross
        kz_t = jnp.concatenate([(k1 * zeta).T, (k2 * zeta).T], axis=0).astype(jnp.bfloat16)
        state_ref[...] = (state * jnp.concatenate([cd, cd], axis=1)
                          + jnp.dot(kz_t, v, preferred_element_type=jnp.float32))
        mu = jnp.mean(out, axis=-1, keepdims=True)
        oc = out - mu
        y = oc * lax.rsqrt(jnp.mean(oc * oc, axis=-1, keepdims=True) + EPS) * gain
        g = g_ref[rows, :].astype(jnp.float32)
        o_ref[rows, :] = (g * jax.nn.sigmoid(g) * y).astype(o_ref.dtype)


def _retention(h, cos_a, sin_a, ret_gain, batch, seq):
    m = batch * seq
    hd = RET_HEAD_DIM
    rb = min(1024, seq)
    nrb = seq // rb
    c = RET_CHUNK
    log_g = jnp.log(1.0 - jnp.power(2.0, -5.0 - jnp.arange(RET_HEADS, dtype=jnp.float32)))
    n = jnp.arange(c, dtype=jnp.float32)
    diff = n[:, None] - n[None, :]
    decay = jnp.where(diff >= 0, jnp.exp(log_g[:, None, None] * jnp.maximum(diff, 0.0)), 0.0)
    xi = jnp.broadcast_to(jnp.exp(log_g[:, None] * (n + 1.0))[:, :, None], (RET_HEADS, c, LANES))
    zeta = jnp.broadcast_to(jnp.exp(log_g[:, None] * (c - 1.0 - n))[:, :, None], (RET_HEADS, c, LANES))
    cd = jnp.broadcast_to(jnp.exp(log_g * c)[:, None, None], (RET_HEADS, 8, LANES))

    def seg(name):
        assert SEG[name] % hd == 0
        base = SEG[name] // hd
        return pl.BlockSpec((rb, hd), lambda b, hh, r: (b * nrb + r, base + hh))

    tab = pl.BlockSpec((rb, LANES), lambda b, hh, r: (b * nrb + r, 0))
    per_head = lambda rows: pl.BlockSpec((None, rows, LANES), lambda b, hh, r: (hh, 0, 0))
    return pl.pallas_call(
        functools.partial(_retention_kernel, n_chunk=rb // c),
        grid=(batch, RET_HEADS, nrb),
        in_specs=[seg("rq"), seg("rk"), seg("rv"), seg("rg"), tab, tab,
                  per_head(c), per_head(c), per_head(c), per_head(8),
                  pl.BlockSpec((1, hd), lambda b, hh, r: (0, hh))],
        out_specs=pl.BlockSpec((rb, hd), lambda b, hh, r: (b * nrb + r, hh)),
        out_shape=jax.ShapeDtypeStruct((m, BRANCH_W), jnp.bfloat16),
        scratch_shapes=[pltpu.VMEM((hd, hd), jnp.float32)],
        compiler_params=_cparams("parallel", "parallel", "arbitrary"),
        name="retention",
    )(h, h, h, h, cos_a, sin_a, decay, xi, zeta, cd, ret_gain.reshape(1, BRANCH_W))


def _dsa_prep_kernel(dq_ref, dk_ref, dv_ref, iq0_ref, iq1_ref, iq2_ref, iq3_ref, ik_ref, iw_ref, cos_ref, sin_ref,
                     qgain_ref, kgain_ref, qg_ref, k_ref, vt_ref, qi_ref, kidx_ref, w_ref):
    iq_refs = (iq0_ref, iq1_ref, iq2_ref, iq3_ref)
    cos, sin = cos_ref[...], sin_ref[...]
    d = DSA_HEAD_DIM

    def rope(x):
        return x * cos + pltpu.roll(x, d // 2, 1) * sin

    def rms(x, g):
        return x * lax.rsqrt(jnp.mean(x * x, axis=-1, keepdims=True) + EPS) * g

    qgain, kgain = qgain_ref[...], kgain_ref[...]
    q_scale = DSA_HEAD_DIM ** -0.5 * np.log2(np.e).astype(np.float32)
    for kvh in range(DSA_KV_HEADS):
        for g in range(DSA_GROUP):
            hh = kvh * DSA_GROUP + g
            x = dq_ref[:, hh * d:(hh + 1) * d].astype(jnp.float32)
            qg_ref[kvh, g * Q_BLOCK:(g + 1) * Q_BLOCK, :] = (rope(rms(x, qgain)) * q_scale).astype(qg_ref.dtype)
        x = dk_ref[:, kvh * d:(kvh + 1) * d].astype(jnp.float32)
        k_ref[kvh] = rope(rms(x, kgain)).astype(k_ref.dtype)
        vt_ref[kvh] = dv_ref[:, kvh * d:(kvh + 1) * d].astype(jnp.float32).T.astype(vt_ref.dtype)
    for hh in range(IDX_HEADS):
        part, col = divmod(hh * IDX_DIM, IQ_BLOCK)
        x = iq_refs[part][:, col:col + IDX_DIM].astype(jnp.float32)
        qi_ref[hh * Q_BLOCK:(hh + 1) * Q_BLOCK, :] = rope(x).astype(qi_ref.dtype)
    kidx_ref[...] = rope(ik_ref[...].astype(jnp.float32)).astype(kidx_ref.dtype)
    w = iw_ref[...].astype(jnp.float32) * (IDX_DIM ** -0.5 * IDX_HEADS ** -0.5)
    for hh in range(IDX_HEADS):
        w_ref[hh * Q_BLOCK:(hh + 1) * Q_BLOCK, :] = jnp.broadcast_to(w[:, hh:hh + 1], (Q_BLOCK, LANES))


def _dsa_prep(h, cos_b, sin_b, q_gain, k_gain, batch, seq):
    nq = seq // Q_BLOCK
    d = DSA_HEAD_DIM

    def seg(name, width, part=0):
        assert SEG[name] % width == 0
        base = SEG[name] // width + part
        return pl.BlockSpec((Q_BLOCK, width), lambda b, j: (b * nq + j, base))

    tab = pl.BlockSpec((Q_BLOCK, LANES), lambda b, j: (b * nq + j, 0))
    gain = pl.BlockSpec((1, d), lambda b, j: (0, 0))
    bf = jnp.bfloat16
    n_iq = IDX_HEADS * IDX_DIM // IQ_BLOCK
    return pl.pallas_call(
        _dsa_prep_kernel,
        grid=(batch, nq),
        in_specs=[seg("dq", BRANCH_W), seg("dk", DSA_KV_W), seg("dv", DSA_KV_W)]
                 + [seg("iq", IQ_BLOCK, part) for part in range(n_iq)]
                 + [seg("ik", IDX_DIM), seg("iw", LANES), tab, tab, gain, gain],
        out_specs=[
            pl.BlockSpec((None, DSA_KV_HEADS, None, DSA_GROUP * Q_BLOCK, d), lambda b, j: (b, 0, j, 0, 0)),
            pl.BlockSpec((None, DSA_KV_HEADS, Q_BLOCK, d), lambda b, j: (b, 0, j, 0)),
            pl.BlockSpec((None, DSA_KV_HEADS, None, d, Q_BLOCK), lambda b, j: (b, 0, j, 0, 0)),
            pl.BlockSpec((None, None, IDX_HEADS * Q_BLOCK, IDX_DIM), lambda b, j: (b, j, 0, 0)),
            pl.BlockSpec((None, Q_BLOCK, IDX_DIM), lambda b, j: (b, j, 0)),
            pl.BlockSpec((None, None, IDX_HEADS * Q_BLOCK, LANES), lambda b, j: (b, j, 0, 0)),
        ],
        out_shape=[
            jax.ShapeDtypeStruct((batch, DSA_KV_HEADS, nq, DSA_GROUP * Q_BLOCK, d), bf),
            jax.ShapeDtypeStruct((batch, DSA_KV_HEADS, seq, d), bf),
            jax.ShapeDtypeStruct((batch, DSA_KV_HEADS, nq, d, Q_BLOCK), bf),
            jax.ShapeDtypeStruct((batch, nq, IDX_HEADS * Q_BLOCK, IDX_DIM), bf),
            jax.ShapeDtypeStruct((batch, seq, IDX_DIM), bf),
            jax.ShapeDtypeStruct((batch, nq, IDX_HEADS * Q_BLOCK, LANES), jnp.float32),
        ],
        compiler_params=_cparams("parallel", "parallel"),
        name="dsa_prep",
    )(*([h] * (5 + n_iq)), cos_b, sin_b, q_gain.reshape(1, d), k_gain.reshape(1, d))


def _dsa_select_kernel(qi_ref, w_ref, kidx_ref, mask_ref, hi_ref, lo_ref, *, topk, nq, per, nqb):
    jp = pl.program_id(1)
    c_ = Q_BLOCK
    tk = per * c_
    blocks = range(nqb)
    sub = lax.broadcasted_iota(jnp.int32, (c_, c_), 0)
    lane = lax.broadcasted_iota(jnp.int32, (c_, c_), 1)
    int_min = jnp.int32(-2 ** 31)
    i16_min = jnp.int32(-2 ** 15)
    n_tiles = (jp * nqb + nqb - 1 + per) // per

    def causal(qb, chunk):
        return (chunk * c_ + sub) <= ((jp * nqb + qb) * c_ + lane)

    def chunk_rows(t, u):
        return pl.ds(pl.multiple_of((t * per + u) * c_, c_), c_)

    def store_keys(qb, chunk, key):
        rows = pl.ds(pl.multiple_of(chunk * c_, c_), c_)
        hi_ref[qb, rows, :] = (key >> 16).astype(jnp.int16)
        lo_ref[qb, rows, :] = ((key & 0xFFFF) - 32768).astype(jnp.int16)

    def score_tile(t, carry):
        for u2 in range(per // 2):
            first = t * per + 2 * u2
            kt = kidx_ref[pl.ds(pl.multiple_of(first * c_, 2 * c_), 2 * c_), :]
            for qb in blocks:
                s = lax.dot_general(qi_ref[qb], kt, _NT, preferred_element_type=jnp.float32)
                acc = [jnp.zeros((c_, c_), jnp.float32), jnp.zeros((c_, c_), jnp.float32)]
                for head in range(IDX_HEADS):
                    rows = slice(head * c_, (head + 1) * c_)
                    w = w_ref[qb, rows, :]
                    for half in range(2):
                        acc[half] = acc[half] + jnp.maximum(s[rows, half * c_:(half + 1) * c_], 0.0) * w
                for half in range(2):
                    bits = lax.bitcast_convert_type(acc[half].T, jnp.int32)
                    key = bits ^ ((bits >> 31) & jnp.int32(0x7FFFFFFF))
                    store_keys(qb, first + half, jnp.where(causal(qb, first + half), key, int_min))
        return carry

    lax.fori_loop(0, n_tiles, score_tile, 0)

    def count16(hits):
        def body(t, accs):
            accs = list(accs)
            for u in range(per):
                rows = chunk_rows(t, u)
                for qb in blocks:
                    hit = jnp.where(hits(qb, rows), jnp.int16(1), jnp.int16(0)).reshape(c_ // 16, 16, c_)
                    accs[qb * per + u] = accs[qb * per + u] + functools.reduce(
                        lambda a, b: a + b, [hit[i] for i in range(c_ // 16)])
            return tuple(accs)
        accs = lax.fori_loop(0, n_tiles, body, (jnp.zeros((16, c_), jnp.int16),) * (per * nqb))
        out = []
        for qb in blocks:
            total = functools.reduce(lambda a, b: a + b, [a.astype(jnp.int32) for a in accs[qb * per:(qb + 1) * per]])
            out.append(jnp.sum(total.astype(jnp.float32), axis=0, keepdims=True))
        return out

    def kth_largest16(ref, ks):
        zero = jnp.zeros((1, c_), jnp.int32)

        def enough(cands):
            c16 = [c.astype(jnp.int16) for c in cands]
            counts = count16(lambda qb, rows: ref[qb, rows, :] >= c16[qb])
            return [n >= k for n, k in zip(counts, ks)]

        vs = tuple(jnp.where(ok, zero, i16_min) for ok in enough([zero] * nqb))

        def bit_step(i, vs):
            cands = [v | jnp.left_shift(jnp.int32(1), 14 - i) for v in vs]
            return tuple(jnp.where(ok, c, v) for ok, c, v in zip(enough(cands), cands, vs))

        return lax.fori_loop(0, 15, bit_step, vs)

    kf = jnp.float32(topk)
    thr_hi = kth_largest16(hi_ref, [kf] * nqb)
    th = [v.astype(jnp.int16) for v in thr_hi]
    k_lo = [kf - n for n in count16(lambda qb, rows: hi_ref[qb, rows, :] > th[qb])]

    def restrict_lo(t, carry):
        for u in range(per):
            rows = chunk_rows(t, u)
            for qb in blocks:
                lo_ref[qb, rows, :] = jnp.where(hi_ref[qb, rows, :] == th[qb], lo_ref[qb, rows, :],
                                                jnp.int16(-2 ** 15))
        return carry

    lax.fori_loop(0, n_tiles, restrict_lo, 0)
    thr_lo = kth_largest16(lo_ref, k_lo)
    thr_lo = [jnp.where((h == i16_min) & (l == i16_min), i16_min + 1, l) for h, l in zip(thr_hi, thr_lo)]
    tl = [v.astype(jnp.int16) for v in thr_lo]

    def cmp_key(qb, rows, lo_test):
        hi = hi_ref[qb, rows, :]
        return (hi > th[qb]) | ((hi == th[qb]) & lo_test(lo_ref[qb, rows, :], tl[qb]))

    n_ge = count16(lambda qb, rows: cmp_key(qb, rows, lambda lo, t: lo >= t))
    n_gt = count16(lambda qb, rows: cmp_key(qb, rows, lambda lo, t: lo > t))
    room = [kf - n for n in n_gt]
    left_out = functools.reduce(jnp.maximum, [jnp.where(n > kf, 1.0, 0.0) for n in n_ge])
    some_tie_left_out = jnp.max(left_out) > 0.0

    @pl.when(jnp.logical_not(some_tie_left_out))
    def _():
        def mask_tile(t, carry):
            for u in range(per):
                rows = chunk_rows(t, u)
                for qb in blocks:
                    take = cmp_key(qb, rows, lambda lo, t: lo >= t)
                    mask_ref[qb, rows, :] = jnp.where(take, jnp.ones((), mask_ref.dtype), jnp.zeros((), mask_ref.dtype))
            return carry
        lax.fori_loop(0, n_tiles, mask_tile, 0)

    @pl.when(some_tie_left_out)
    def _():
        tri = jnp.where(sub > lane, 1.0, 0.0).astype(jnp.bfloat16)
        thr = [jnp.left_shift(h, 16) | ((l + 32768) & 0xFFFF) for h, l in zip(thr_hi, thr_lo)]

        def mask_tile(t, seen):
            seen = list(seen)
            for u in range(per):
                rows = chunk_rows(t, u)
                for qb in blocks:
                    kc = (jnp.left_shift(hi_ref[qb, rows, :].astype(jnp.int32), 16)
                          | ((lo_ref[qb, rows, :].astype(jnp.int32) + 32768) & 0xFFFF))
                    eq = jnp.where(kc == thr[qb], 1.0, 0.0)
                    before = jnp.dot(tri, eq.astype(jnp.bfloat16), preferred_element_type=jnp.float32) + seen[qb]
                    take = jnp.where(kc > thr[qb], 1.0, jnp.where(before < room[qb], eq, 0.0))
                    mask_ref[qb, rows, :] = take.astype(mask_ref.dtype)
                    seen[qb] = seen[qb] + jnp.sum(eq, axis=0, keepdims=True)
            return tuple(seen)
        lax.fori_loop(0, n_tiles, mask_tile, (jnp.zeros((1, c_), jnp.float32),) * nqb)

    def zero_tile(t, carry):
        for qb in blocks:
            mask_ref[qb, pl.ds(pl.multiple_of(t * tk, tk), tk), :] = jnp.zeros((tk, c_), mask_ref.dtype)
        return carry

    lax.fori_loop(n_tiles, nq // per, zero_tile, 0)


def _dsa_select(qi, w, kidx, topk):
    batch, nq = qi.shape[0], qi.shape[1]
    seq = kidx.shape[1]
    per = min(4, nq)
    nqb = 2
    assert nq % per == 0 and per % 2 == 0 and nq % nqb == 0 and seq < 2 ** 15
    return pl.pallas_call(
        functools.partial(_dsa_select_kernel, topk=topk, nq=nq, per=per, nqb=nqb),
        grid=(batch, nq // nqb),
        in_specs=[pl.BlockSpec((None, nqb, IDX_HEADS * Q_BLOCK, IDX_DIM), lambda b, j: (b, j, 0, 0)),
                  pl.BlockSpec((None, nqb, IDX_HEADS * Q_BLOCK, LANES), lambda b, j: (b, j, 0, 0)),
                  pl.BlockSpec((None, seq, IDX_DIM), lambda b, j: (b, 0, 0), pipeline_mode=pl.Buffered(1))],
        out_specs=pl.BlockSpec((None, nqb, seq, Q_BLOCK), lambda b, j: (b, j, 0, 0)),
        out_shape=jax.ShapeDtypeStruct((batch, nq, seq, Q_BLOCK), jnp.bfloat16),
        scratch_shapes=[pltpu.VMEM((nqb, seq, Q_BLOCK), jnp.int16),
                        pltpu.VMEM((nqb, seq, Q_BLOCK), jnp.int16)],
        compiler_params=_cparams("parallel", "arbitrary"),
        name="dsa_select",
    )(qi, w, kidx)


def _dsa_attn_kernel(qg_ref, k_ref, vt_ref, mask_ref, dg_ref, o_ref, m_ref, acc_ref, s_ref, p_ref, alpha_ref,
                     *, tk, n_key_tiles):
    j = pl.program_id(1)
    per = tk // Q_BLOCK
    d = DSA_HEAD_DIM
    gw = DSA_GROUP * Q_BLOCK
    heads = range(DSA_KV_HEADS)
    n_live = (j + per) // per
    n_pairs = (n_live + 1) // 2
    neg_inf = jnp.float32(-jnp.inf)
    m_ref[...] = jnp.full_like(m_ref, neg_inf)
    acc_ref[...] = jnp.zeros_like(acc_ref)
    p_ref[:, 1] = jnp.zeros((DSA_KV_HEADS,) + p_ref.shape[2:], p_ref.dtype)
    alpha_ref[:, 1] = jnp.ones((DSA_KV_HEADS,) + alpha_ref.shape[2:], alpha_ref.dtype)
    ones_rows = jnp.ones((acc_ref.shape[1] - d, tk), jnp.bfloat16)

    def logits(kv, t):
        r0 = pl.multiple_of(t * tk, tk)
        return lax.dot_general(k_ref[kv, pl.ds(r0, tk), :], qg_ref[kv], _NT, preferred_element_type=jnp.float32)

    def softmax_stage(kv, sel, slot):
        s = s_ref[kv, slot]
        s = jnp.concatenate([jnp.where(sel, s[:, g * Q_BLOCK:(g + 1) * Q_BLOCK], neg_inf)
                             for g in range(DSA_GROUP)], axis=1)
        m_old = m_ref[kv]
        m_new = jnp.maximum(m_old, jnp.max(s, axis=0, keepdims=True))
        m_safe = jnp.where(m_new == neg_inf, 0.0, m_new)
        p_ref[kv, slot] = jnp.exp2(s - m_safe).astype(p_ref.dtype)
        alpha_ref[kv, slot] = jnp.exp2(m_old - m_safe)
        m_ref[kv] = m_new

    def pv_stage(kv, t, slot):
        vt = jnp.concatenate([vt_ref[kv, t * per + i] for i in range(per)], axis=1)
        lhs = jnp.concatenate([vt, ones_rows], axis=0)
        acc_ref[kv] = acc_ref[kv] * alpha_ref[kv, slot] + jnp.dot(lhs, p_ref[kv, slot],
                                                                  preferred_element_type=jnp.float32)

    for kv in heads:
        s_ref[kv, 0] = logits(kv, 0)

    def pair(i, carry):
        for u in range(2):
            t = 2 * i + u
            r0 = pl.multiple_of(t * tk, tk)
            sel = mask_ref[pl.ds(r0, tk), :].astype(jnp.float32) > 0.5
            for kv in heads:
                s_ref[kv, 1 - u] = logits(kv, jnp.minimum(t + 1, n_key_tiles - 1))
                softmax_stage(kv, sel, u)
                pv_stage(kv, jnp.maximum(t - 1, 0), 1 - u)
        return carry

    lax.fori_loop(0, n_pairs, pair, 0)
    for kv in heads:
        pv_stage(kv, 2 * n_pairs - 1, 1)
        out_t = acc_ref[kv, :d, :] / acc_ref[kv, d:d + 1, :]
        for g in range(DSA_GROUP):
            cols = slice(kv * gw + g * Q_BLOCK, kv * gw + (g + 1) * Q_BLOCK)
            gate = dg_ref[:, cols].astype(jnp.float32)
            o_ref[:, cols] = (gate * jax.nn.sigmoid(gate)
                              * out_t[:, g * Q_BLOCK:(g + 1) * Q_BLOCK].T).astype(o_ref.dtype)


def _dsa_attention(qg, k, vt, mask, h, batch, seq):
    nq = seq // Q_BLOCK
    d = DSA_HEAD_DIM
    gw = DSA_GROUP * d
    tk = min(256, seq // 2)
    assert seq % (2 * tk) == 0
    assert SEG["dg"] % BRANCH_W == 0
    kvh = DSA_KV_HEADS
    once = pl.Buffered(1)
    return pl.pallas_call(
        functools.partial(_dsa_attn_kernel, tk=tk, n_key_tiles=seq // tk),
        grid=(batch, nq),
        in_specs=[pl.BlockSpec((None, kvh, None, gw, d), lambda b, j: (b, 0, j, 0, 0)),
                  pl.BlockSpec((None, kvh, seq, d), lambda b, j: (b, 0, 0, 0), pipeline_mode=once),
                  pl.BlockSpec((None, kvh, nq, d, Q_BLOCK), lambda b, j: (b, 0, 0, 0, 0), pipeline_mode=once),
                  pl.BlockSpec((None, None, seq, Q_BLOCK), lambda b, j: (b, j, 0, 0)),
                  pl.BlockSpec((Q_BLOCK, BRANCH_W), lambda b, j: (b * nq + j, SEG["dg"] // BRANCH_W))],
        out_specs=pl.BlockSpec((Q_BLOCK, BRANCH_W), lambda b, j: (b * nq + j, 0)),
        out_shape=jax.ShapeDtypeStruct((batch * seq, BRANCH_W), jnp.bfloat16),
        scratch_shapes=[pltpu.VMEM((kvh, 1, gw), jnp.float32),
                        pltpu.VMEM((kvh, d + 16, gw), jnp.float32),
                        pltpu.VMEM((kvh, 2, tk, gw), jnp.float32),
                        pltpu.VMEM((kvh, 2, tk, gw), jnp.bfloat16),
                        pltpu.VMEM((kvh, 2, 1, gw), jnp.float32)],
        compiler_params=_cparams("parallel", "arbitrary"),
        name="dsa_attn",
    )(qg, k, vt, mask, h)


def _gelu_tanh(x):
    return 0.5 * x * (1.0 + jnp.tanh(np.sqrt(2.0 / np.pi).astype(np.float32) * (x + 0.044715 * (x * x * x))))


def _gmlp_kernel(u_ref, v_ref, g_ref, gain_ref, w_ref, b_ref, o_ref):
    c_ = GM_CHUNK
    v = _gelu_tanh(v_ref[...].astype(jnp.float32))
    mu = jnp.mean(v, axis=-1, keepdims=True)
    vc = v - mu
    vn = (vc * lax.rsqrt(jnp.mean(vc * vc, axis=-1, keepdims=True) + EPS) * gain_ref[...]).astype(jnp.bfloat16)
    sub = lax.broadcasted_iota(jnp.int32, (c_, c_), 0)
    lane = lax.broadcasted_iota(jnp.int32, (c_, c_), 1)
    tril = sub >= lane
    for g in range(GM_GROUPS):
        cols = slice(g * GM_GROUP_DIM, (g + 1) * GM_GROUP_DIM)
        w = jnp.where(tril, w_ref[g], jnp.zeros((), w_ref.dtype))
        mixed = jnp.dot(w, vn[:, cols], preferred_element_type=jnp.float32) + b_ref[g]
        u = _gelu_tanh(u_ref[:, cols].astype(jnp.float32))
        gate = g_ref[:, cols].astype(jnp.float32)
        o_ref[:, cols] = (gate * jax.nn.sigmoid(gate) * (u * mixed)).astype(o_ref.dtype)


def _gmlp(h, gm_gain, w_spatial, b_spatial, m):
    c_ = GM_CHUNK
    w = BRANCH_W

    def seg(name):
        assert SEG[name] % w == 0
        base = SEG[name] // w
        return pl.BlockSpec((c_, w), lambda i: (i, base))

    b_b = jnp.broadcast_to(b_spatial[:, :, None], (GM_GROUPS, c_, GM_GROUP_DIM))
    whole = lambda shape: pl.BlockSpec(shape, lambda i: (0,) * len(shape))
    return pl.pallas_call(
        _gmlp_kernel,
        grid=(m // c_,),
        in_specs=[seg("gu"), seg("gv"), seg("gg"), whole((1, w)), whole((GM_GROUPS, c_, c_)),
                  whole((GM_GROUPS, c_, GM_GROUP_DIM))],
        out_specs=pl.BlockSpec((c_, w), lambda i: (i, 0)),
        out_shape=jax.ShapeDtypeStruct((m, w), jnp.bfloat16),
        compiler_params=_cparams("parallel"),
        name="gmlp",
    )(h, h, h, gm_gain.reshape(1, w), w_spatial.astype(jnp.bfloat16), b_b)


def _merge_kernel(y0_ref, y1_ref, y2_ref, w_ref, g0_ref, g1_ref, g2_ref, o_ref):
    acc = None
    for b, (y_ref, g_ref) in enumerate(((y0_ref, g0_ref), (y1_ref, g1_ref), (y2_ref, g2_ref))):
        proj = jnp.dot(y_ref[...], w_ref[b], preferred_element_type=jnp.float32)
        term = jax.nn.sigmoid(g_ref[...].astype(jnp.float32)) * proj
        acc = term if acc is None else acc + term
    o_ref[...] = acc.astype(o_ref.dtype)


def _merge(ys, w_branch, h):
    m = h.shape[0]
    tm, tn = min(1024, m), 512
    ybs = pl.BlockSpec((tm, BRANCH_W), lambda i, j: (i, 0))

    def gate(b):
        base = (SEG["mg"] + b * D_MODEL) // tn
        return pl.BlockSpec((tm, tn), lambda i, j: (i, base + j))

    return pl.pallas_call(
        _merge_kernel,
        grid=(m // tm, D_MODEL // tn),
        in_specs=[ybs, ybs, ybs, pl.BlockSpec((N_BRANCH, BRANCH_W, tn), lambda i, j: (0, 0, j)),
                  gate(0), gate(1), gate(2)],
        out_specs=pl.BlockSpec((tm, tn), lambda i, j: (i, j)),
        out_shape=jax.ShapeDtypeStruct((m, D_MODEL), jnp.bfloat16),
        compiler_params=_cparams("parallel", "arbitrary"),
        name="merge",
    )(*ys, w_branch, h, h, h)


def _out_proj_kernel(a_ref, w_ref, x_ref, o_ref):
    o_ref[...] = x_ref[...] + jnp.dot(a_ref[...], w_ref[...], preferred_element_type=jnp.float32)


def _out_proj(merged, w_out, x2):
    m, d = x2.shape
    tm, tn = min(1024, m), 512
    return pl.pallas_call(
        _out_proj_kernel,
        grid=(m // tm, d // tn),
        in_specs=[pl.BlockSpec((tm, d), lambda i, j: (i, 0)), pl.BlockSpec((d, tn), lambda i, j: (0, j)),
                  pl.BlockSpec((tm, tn), lambda i, j: (i, j))],
        out_specs=pl.BlockSpec((tm, tn), lambda i, j: (i, j)),
        out_shape=jax.ShapeDtypeStruct((m, d), jnp.float32),
        compiler_params=_cparams("parallel", "arbitrary"),
        name="out_proj",
    )(merged, w_out, x2)


def _w_in_runs():
    runs, o = [], 0
    for name in _MY_ORDER:
        off, width = _REF_OFF[name]
        assert SEG[name] == o
        if runs and runs[-1][1] is not None and runs[-1][1] + runs[-1][2] == off and runs[-1][0] + runs[-1][2] == o:
            runs[-1][2] += width
        else:
            runs.append([o, off, width])
        padded = -(-width // LANES) * LANES
        if padded != width:
            runs.append([o + width, None, padded - width])
        o += padded
    if N_IN_PAD != o:
        runs.append([o, None, N_IN_PAD - o])
    return runs


def _relayout_plan():
    src_of = np.full((N_IN_PAD,), -1, np.int64)
    for dst, src, width in _w_in_runs():
        if src is not None:
            src_of[dst:dst + width] = np.arange(src, src + width)
    first, valid = [], []
    for c in range(N_IN_PAD // LANES):
        cols = src_of[c * LANES:(c + 1) * LANES]
        n = int((cols >= 0).sum())
        assert n > 0 and (cols[:n] == cols[0] + np.arange(n)).all() and (cols[n:] < 0).all()
        first.append(int(cols[0]))
        valid.append(n)
    return np.asarray(first, np.int32), np.asarray(valid, np.int32)


def _relayout_kernel(first_ref, valid_ref, wt_ref, o_ref):
    x = wt_ref[0].T
    lane = lax.broadcasted_iota(jnp.int32, x.shape, 1)
    o_ref[...] = jnp.where(lane < valid_ref[pl.program_id(0)], x, 0.0).astype(o_ref.dtype)


def _relayout_w_in(w_in, layer):
    _, d, n = w_in.shape
    first, valid = _relayout_plan()
    assert int((first + LANES).max()) <= n
    wt = jnp.swapaxes(w_in, 1, 2)
    return pl.pallas_call(
        _relayout_kernel,
        grid_spec=pltpu.PrefetchScalarGridSpec(
            num_scalar_prefetch=2,
            grid=(N_IN_PAD // LANES,),
            in_specs=[pl.BlockSpec((pl.Element(1), pl.Element(LANES), pl.Element(d)),
                                   lambda c, first, valid: (layer, pl.multiple_of(first[c], 8), 0))],
            out_specs=pl.BlockSpec((d, LANES), lambda c, first, valid: (0, c)),
        ),
        out_shape=jax.ShapeDtypeStruct((d, N_IN_PAD), jnp.bfloat16),
        compiler_params=_cparams("parallel"),
        name="relayout_w_in",
    )(jnp.asarray(first), jnp.asarray(valid), wt)


def kernel(x, positions, norm_gain, w_in, ret_norm_gain, q_norm_gain, k_norm_gain, gm_norm_gain, w_spatial,
           b_spatial, w_branch, w_out):
    batch, seq, d = x.shape
    assert d == D_MODEL and seq % Q_BLOCK == 0
    depth = w_in.shape[0]
    m = batch * seq
    topk = min(TOPK_MAX, seq // 4)
    cos_a, sin_a, cos_b, sin_b = _rope_tables(positions)
    x2 = x.reshape(m, d)
    for l in range(depth):
        xn = _rmsnorm(x2, norm_gain[l])
        h = _in_proj(xn, _relayout_w_in(w_in, l))
        y_ret = _retention(h, cos_a, sin_a, ret_norm_gain[l], batch, seq)
        qg, k, vt, qi, kidx, w_idx = _dsa_prep(h, cos_b, sin_b, q_norm_gain[l], k_norm_gain[l], batch, seq)
        mask = _dsa_select(qi, w_idx, kidx, topk)
        y_dsa = _dsa_attention(qg, k, vt, mask, h, batch, seq)
        y_gm = _gmlp(h, gm_norm_gain[l], w_spatial[l], b_spatial[l], m)
        merged = _merge((y_ret, y_dsa, y_gm), w_branch[l].astype(jnp.bfloat16), h)
        x2 = _out_proj(merged, w_out[l].astype(jnp.bfloat16), x2)
    return x2.reshape(batch, seq, d)
```

```python
import functools

import numpy as np
import jax
import jax.numpy as jnp
from jax import lax
from jax.experimental import pallas as pl
from jax.experimental.pallas import tpu as pltpu

D_MODEL = 4096
BRANCH_W = D_MODEL // 2
N_BRANCH = 3
RET_HEAD_DIM = 256
RET_HEADS = BRANCH_W // RET_HEAD_DIM
RET_CHUNK = 128
DSA_HEAD_DIM = 128
DSA_HEADS = BRANCH_W // DSA_HEAD_DIM
DSA_KV_HEADS = 4
DSA_GROUP = DSA_HEADS // DSA_KV_HEADS
DSA_KV_W = DSA_KV_HEADS * DSA_HEAD_DIM
IDX_HEADS = 32
IDX_DIM = 128
TOPK_MAX = 256
Q_BLOCK = 128
GM_GROUPS = 16
GM_GROUP_DIM = BRANCH_W // GM_GROUPS
GM_CHUNK = 128
ROPE_THETA = 10000.0
EPS = 1e-6

LANES = 128
VMEM_LIMIT = 56 * 1024 * 1024

_REF_SEGS = (("rq", BRANCH_W), ("rk", BRANCH_W), ("rv", BRANCH_W), ("rg", BRANCH_W),
             ("dq", BRANCH_W), ("dk", DSA_KV_W), ("dv", DSA_KV_W),
             ("iq", IDX_HEADS * IDX_DIM), ("ik", IDX_DIM), ("iw", IDX_HEADS),
             ("dg", BRANCH_W), ("gu", BRANCH_W), ("gv", BRANCH_W), ("gg", BRANCH_W),
             ("mg", N_BRANCH * D_MODEL))
_MY_ORDER = ("dg", "gu", "gv", "gg", "mg", "rq", "rk", "rv", "rg", "dq", "dk", "dv", "iq", "ik", "iw")
N_TILE = 768
IQ_BLOCK = 1024


def _layout():
    ref_off, o = {}, 0
    for name, w in _REF_SEGS:
        ref_off[name] = (o, w)
        o += w
    my_off, o = {}, 0
    for name in _MY_ORDER:
        w = ref_off[name][1]
        my_off[name] = o
        o += -(-w // LANES) * LANES
    total = -(-o // N_TILE) * N_TILE
    return ref_off, my_off, total


_REF_OFF, SEG, N_IN_PAD = _layout()

_NT = (((1,), (1,)), ((), ()))


def _cparams(*sem):
    return pltpu.CompilerParams(dimension_semantics=sem, vmem_limit_bytes=VMEM_LIMIT)


def _rmsnorm_kernel(x_ref, g_ref, o_ref):
    x = x_ref[...]
    ms = jnp.mean(x * x, axis=-1, keepdims=True)
    o_ref[...] = (x * lax.rsqrt(ms + EPS) * g_ref[...]).astype(o_ref.dtype)


def _rmsnorm(x2, gain):
    m, d = x2.shape
    tm = min(256, m)
    return pl.pallas_call(
        _rmsnorm_kernel,
        grid=(m // tm,),
        in_specs=[pl.BlockSpec((tm, d), lambda i: (i, 0)), pl.BlockSpec((1, d), lambda i: (0, 0))],
        out_specs=pl.BlockSpec((tm, d), lambda i: (i, 0)),
        out_shape=jax.ShapeDtypeStruct((m, d), jnp.bfloat16),
        compiler_params=_cparams("parallel"),
        name="rmsnorm",
    )(x2, gain.reshape(1, d))


def _matmul_kernel(a_ref, w_ref, o_ref):
    o_ref[...] = jnp.dot(a_ref[...], w_ref[...], preferred_element_type=jnp.float32).astype(o_ref.dtype)


def _in_proj(xn, w):
    m, k = xn.shape
    n = w.shape[1]
    tm, tn = min(1024, m), N_TILE
    return pl.pallas_call(
        _matmul_kernel,
        grid=(m // tm, n // tn),
        in_specs=[pl.BlockSpec((tm, k), lambda i, j: (i, 0)), pl.BlockSpec((k, tn), lambda i, j: (0, j))],
        out_specs=pl.BlockSpec((tm, tn), lambda i, j: (i, j)),
        out_shape=jax.ShapeDtypeStruct((m, n), jnp.bfloat16),
        compiler_params=_cparams("parallel", "arbitrary"),
        name="in_proj",
    )(xn, w)


def _rope_tables_kernel(pos_ref, inv_a_ref, inv_b_ref, sign_ref, ca_ref, sa_ref, cb_ref, sb_ref):
    p = pos_ref[...].astype(jnp.float32)
    ang_a = p * inv_a_ref[...]
    ca_ref[...] = jnp.cos(ang_a)
    sa_ref[...] = jnp.sin(ang_a)
    ang_b = p * inv_b_ref[...]
    cb_ref[...] = jnp.cos(ang_b)
    sb_ref[...] = jnp.sin(ang_b) * sign_ref[...]


def _rope_tables(positions):
    m = positions.size
    tr = min(1024, m)

    def inv_freq(d):
        half = d // 2
        return 1.0 / (ROPE_THETA ** (jnp.arange(half, dtype=jnp.float32) * 2.0 / d))

    inv_a = inv_freq(RET_HEAD_DIM).reshape(1, LANES)
    inv_b = jnp.tile(inv_freq(DSA_HEAD_DIM), 2).reshape(1, LANES)
    sign = jnp.concatenate([-jnp.ones((LANES // 2,), jnp.float32), jnp.ones((LANES // 2,), jnp.float32)]).reshape(1, LANES)
    row = pl.BlockSpec((1, LANES), lambda i: (0, 0))
    tab = pl.BlockSpec((tr, LANES), lambda i: (i, 0))
    sds = jax.ShapeDtypeStruct((m, LANES), jnp.float32)
    return pl.pallas_call(
        _rope_tables_kernel,
        grid=(m // tr,),
        in_specs=[pl.BlockSpec((tr, 1), lambda i: (i, 0)), row, row, row],
        out_specs=[tab, tab, tab, tab],
        out_shape=[sds, sds, sds, sds],
        compiler_params=_cparams("parallel"),
        name="rope_tables",
    )(positions.reshape(m, 1), inv_a, inv_b, sign)


def _retention_kernel(q_ref, k_ref, v_ref, g_ref, cos_ref, sin_ref, decay_ref, xi_ref, zeta_ref, cd_ref,
                      gain_ref, o_ref, state_ref, *, n_chunk):
    @pl.when(pl.program_id(2) == 0)
    def _():
        state_ref[...] = jnp.zeros_like(state_ref)

    half = RET_HEAD_DIM // 2
    decay = decay_ref[...]
    xi = xi_ref[...]
    zeta = zeta_ref[...]
    cd = cd_ref[0:1, :]
    gain = gain_ref[...]
    k_scale = RET_HEAD_DIM ** -0.5

    def rot(x, cos, sin):
        x1, x2 = x[:, :half], x[:, half:]
        return x1 * cos - x2 * sin, x2 * cos + x1 * sin

    for c in range(n_chunk):
        rows = slice(c * RET_CHUNK, (c + 1) * RET_CHUNK)
        cos, sin = cos_ref[rows, :], sin_ref[rows, :]
        q1, q2 = rot(q_ref[rows, :].astype(jnp.float32), cos, sin)
        k1, k2 = rot(k_ref[rows, :].astype(jnp.float32), cos, sin)
        k1, k2 = k1 * k_scale, k2 * k_scale
        v = v_ref[rows, :]
        qb = jnp.concatenate([q1, q2], axis=1).astype(jnp.bfloat16)
        kb = jnp.concatenate([k1, k2], axis=1).astype(jnp.bfloat16)
        sc = lax.dot_general(qb, kb, _NT, preferred_element_type=jnp.float32) * decay
        inner = jnp.dot(sc.astype(jnp.bfloat16), v, preferred_element_type=jnp.float32)
        state = state_ref[...]
        qx = jnp.concatenate([q1 * xi, q2 * xi], axis=1).astype(jnp.bfloat16)
        cross = jnp.dot(qx, state.astype(jnp.bfloat16), preferred_element_type=jnp.float32)
        out = inner + cross
        kz_t = jnp.concatenate([(k1 * zeta).T, (k2 * zeta).T], axis=0).astype(jnp.bfloat16)
        state_ref[...] = (state * jnp.concatenate([cd, cd], axis=1)
                          + jnp.dot(kz_t, v, preferred_element_type=jnp.float32))
        mu = jnp.mean(out, axis=-1, keepdims=True)
        oc = out - mu
        y = oc * lax.rsqrt(jnp.mean(oc * oc, axis=-1, keepdims=True) + EPS) * gain
        g = g_ref[rows, :].astype(jnp.float32)
        o_ref[rows, :] = (g * jax.nn.sigmoid(g) * y).astype(o_ref.dtype)


def _retention(h, cos_a, sin_a, ret_gain, batch, seq):
    m = batch * seq
    hd = RET_HEAD_DIM
    rb = min(1024, seq)
    nrb = seq // rb
    c = RET_CHUNK
    log_g = jnp.log(1.0 - jnp.power(2.0, -5.0 - jnp.arange(RET_HEADS, dtype=jnp.float32)))
    n = jnp.arange(c, dtype=jnp.float32)
    diff = n[:, None] - n[None, :]
    decay = jnp.where(diff >= 0, jnp.exp(log_g[:, None, None] * jnp.maximum(diff, 0.0)), 0.0)
    xi = jnp.broadcast_to(jnp.exp(log_g[:, None] * (n + 1.0))[:, :, None], (RET_HEADS, c, LANES))
    zeta = jnp.broadcast_to(jnp.exp(log_g[:, None] * (c - 1.0 - n))[:, :, None], (RET_HEADS, c, LANES))
    cd = jnp.broadcast_to(jnp.exp(log_g * c)[:, None, None], (RET_HEADS, 8, LANES))

    def seg(name):
        assert SEG[name] % hd == 0
        base = SEG[name] // hd
        return pl.BlockSpec((rb, hd), lambda b, hh, r: (b * nrb + r, base + hh))

    tab = pl.BlockSpec((rb, LANES), lambda b, hh, r: (b * nrb + r, 0))
    per_head = lambda rows: pl.BlockSpec((None, rows, LANES), lambda b, hh, r: (hh, 0, 0))
    return pl.pallas_call(
        functools.partial(_retention_kernel, n_chunk=rb // c),
        grid=(batch, RET_HEADS, nrb),
        in_specs=[seg("rq"), seg("rk"), seg("rv"), seg("rg"), tab, tab,
                  per_head(c), per_head(c), per_head(c), per_head(8),
                  pl.BlockSpec((1, hd), lambda b, hh, r: (0, hh))],
        out_specs=pl.BlockSpec((rb, hd), lambda b, hh, r: (b * nrb + r, hh)),
        out_shape=jax.ShapeDtypeStruct((m, BRANCH_W), jnp.bfloat16),
        scratch_shapes=[pltpu.VMEM((hd, hd), jnp.float32)],
        compiler_params=_cparams("parallel", "parallel", "arbitrary"),
        name="retention",
    )(h, h, h, h, cos_a, sin_a, decay, xi, zeta, cd, ret_gain.reshape(1, BRANCH_W))


def _dsa_prep_kernel(dq_ref, dk_ref, dv_ref, iq0_ref, iq1_ref, iq2_ref, iq3_ref, ik_ref, iw_ref, cos_ref, sin_ref,
                     qgain_ref, kgain_ref, qg_ref, k_ref, vt_ref, qi_ref, kidx_ref, w_ref, kn_ref):
    iq_refs = (iq0_ref, iq1_ref, iq2_ref, iq3_ref)
    cos, sin = cos_ref[...], sin_ref[...]
    d = DSA_HEAD_DIM

    def rope(x):
        return x * cos + pltpu.roll(x, d // 2, 1) * sin

    def rms(x, g):
        return x * lax.rsqrt(jnp.mean(x * x, axis=-1, keepdims=True) + EPS) * g

    qgain, kgain = qgain_ref[...], kgain_ref[...]
    q_scale = DSA_HEAD_DIM ** -0.5 * np.log2(np.e).astype(np.float32)
    for kvh in range(DSA_KV_HEADS):
        for g in range(DSA_GROUP):
            hh = kvh * DSA_GROUP + g
            x = dq_ref[:, hh * d:(hh + 1) * d].astype(jnp.float32)
            qg_ref[kvh, g * Q_BLOCK:(g + 1) * Q_BLOCK, :] = (rope(rms(x, qgain)) * q_scale).astype(qg_ref.dtype)
        x = dk_ref[:, kvh * d:(kvh + 1) * d].astype(jnp.float32)
        kb = rope(rms(x, kgain)).astype(k_ref.dtype)
        k_ref[kvh] = kb
        kf = kb.astype(jnp.float32)
        kn2 = jnp.max(jnp.sum(kf * kf, axis=-1, keepdims=True), axis=0, keepdims=True)
        kn_ref[kvh:kvh + 1, :] = jnp.broadcast_to(kn2, (1, LANES))
        vt_ref[kvh] = dv_ref[:, kvh * d:(kvh + 1) * d].astype(jnp.float32).T.astype(vt_ref.dtype)
    for hh in range(IDX_HEADS):
        part, col = divmod(hh * IDX_DIM, IQ_BLOCK)
        x = iq_refs[part][:, col:col + IDX_DIM].astype(jnp.float32)
        qi_ref[hh * Q_BLOCK:(hh + 1) * Q_BLOCK, :] = rope(x).astype(qi_ref.dtype)
    kidx_ref[...] = rope(ik_ref[...].astype(jnp.float32)).astype(kidx_ref.dtype)
    w = iw_ref[...].astype(jnp.float32) * (IDX_DIM ** -0.5 * IDX_HEADS ** -0.5)
    for hh in range(IDX_HEADS):
        w_ref[hh * Q_BLOCK:(hh + 1) * Q_BLOCK, :] = jnp.broadcast_to(w[:, hh:hh + 1], (Q_BLOCK, LANES))


def _dsa_prep(h, cos_b, sin_b, q_gain, k_gain, batch, seq):
    nq = seq // Q_BLOCK
    d = DSA_HEAD_DIM

    def seg(name, width, part=0):
        assert SEG[name] % width == 0
        base = SEG[name] // width + part
        return pl.BlockSpec((Q_BLOCK, width), lambda b, j: (b * nq + j, base))

    tab = pl.BlockSpec((Q_BLOCK, LANES), lambda b, j: (b * nq + j, 0))
    gain = pl.BlockSpec((1, d), lambda b, j: (0, 0))
    bf = jnp.bfloat16
    n_iq = IDX_HEADS * IDX_DIM // IQ_BLOCK
    return pl.pallas_call(
        _dsa_prep_kernel,
        grid=(batch, nq),
        in_specs=[seg("dq", BRANCH_W), seg("dk", DSA_KV_W), seg("dv", DSA_KV_W)]
                 + [seg("iq", IQ_BLOCK, part) for part in range(n_iq)]
                 + [seg("ik", IDX_DIM), seg("iw", LANES), tab, tab, gain, gain],
        out_specs=[
            pl.BlockSpec((None, DSA_KV_HEADS, None, DSA_GROUP * Q_BLOCK, d), lambda b, j: (b, 0, j, 0, 0)),
            pl.BlockSpec((None, DSA_KV_HEADS, Q_BLOCK, d), lambda b, j: (b, 0, j, 0)),
            pl.BlockSpec((None, DSA_KV_HEADS, None, d, Q_BLOCK), lambda b, j: (b, 0, j, 0, 0)),
            pl.BlockSpec((None, None, IDX_HEADS * Q_BLOCK, IDX_DIM), lambda b, j: (b, j, 0, 0)),
            pl.BlockSpec((None, Q_BLOCK, IDX_DIM), lambda b, j: (b, j, 0)),
            pl.BlockSpec((None, None, IDX_HEADS * Q_BLOCK, LANES), lambda b, j: (b, j, 0, 0)),
            pl.BlockSpec((None, None, DSA_KV_HEADS, LANES), lambda b, j: (b, j, 0, 0)),
        ],
        out_shape=[
            jax.ShapeDtypeStruct((batch, DSA_KV_HEADS, nq, DSA_GROUP * Q_BLOCK, d), bf),
            jax.ShapeDtypeStruct((batch, DSA_KV_HEADS, seq, d), bf),
            jax.ShapeDtypeStruct((batch, DSA_KV_HEADS, nq, d, Q_BLOCK), bf),
            jax.ShapeDtypeStruct((batch, nq, IDX_HEADS * Q_BLOCK, IDX_DIM), bf),
            jax.ShapeDtypeStruct((batch, seq, IDX_DIM), bf),
            jax.ShapeDtypeStruct((batch, nq, IDX_HEADS * Q_BLOCK, LANES), jnp.float32),
            jax.ShapeDtypeStruct((batch, nq, DSA_KV_HEADS, LANES), jnp.float32),
        ],
        compiler_params=_cparams("parallel", "parallel"),
        name="dsa_prep",
    )(*([h] * (5 + n_iq)), cos_b, sin_b, q_gain.reshape(1, d), k_gain.reshape(1, d))


def _dsa_select_kernel(qi_ref, w_ref, kidx_ref, mask_ref, hi_ref, lo_ref, *, topk, nq, per, nqb):
    jp = pl.program_id(1)
    c_ = Q_BLOCK
    tk = per * c_
    blocks = range(nqb)
    sub = lax.broadcasted_iota(jnp.int32, (c_, c_), 0)
    lane = lax.broadcasted_iota(jnp.int32, (c_, c_), 1)
    int_min = jnp.int32(-2 ** 31)
    i16_min = jnp.int32(-2 ** 15)
    n_tiles = (jp * nqb + nqb - 1 + per) // per

    def causal(qb, chunk):
        return (chunk * c_ + sub) <= ((jp * nqb + qb) * c_ + lane)

    def chunk_rows(t, u):
        return pl.ds(pl.multiple_of((t * per + u) * c_, c_), c_)

    def store_keys(qb, chunk, key):
        rows = pl.ds(pl.multiple_of(chunk * c_, c_), c_)
        hi_ref[qb, rows, :] = (key >> 16).astype(jnp.int16)
        lo_ref[qb, rows, :] = ((key & 0xFFFF) - 32768).astype(jnp.int16)

    def score_tile(t, carry):
        for u2 in range(per // 2):
            first = t * per + 2 * u2
            kt = kidx_ref[pl.ds(pl.multiple_of(first * c_, 2 * c_), 2 * c_), :]
            for qb in blocks:
                s = lax.dot_general(qi_ref[qb], kt, _NT, preferred_element_type=jnp.float32)
                acc = [jnp.zeros((c_, c_), jnp.float32), jnp.zeros((c_, c_), jnp.float32)]
                for head in range(IDX_HEADS):
                    rows = slice(head * c_, (head + 1) * c_)
                    w = w_ref[qb, rows, :]
                    for half in range(2):
                        acc[half] = acc[half] + jnp.maximum(s[rows, half * c_:(half + 1) * c_], 0.0) * w
                for half in range(2):
                    bits = lax.bitcast_convert_type(acc[half].T, jnp.int32)
                    key = bits ^ ((bits >> 31) & jnp.int32(0x7FFFFFFF))
                    store_keys(qb, first + half, jnp.where(causal(qb, first + half), key, int_min))
        return carry

    lax.fori_loop(0, n_tiles, score_tile, 0)

    def count16(hits):
        def body(t, accs):
            accs = list(accs)
            for u in range(per):
                rows = chunk_rows(t, u)
                for qb in blocks:
                    hit = jnp.where(hits(qb, rows), jnp.int16(1), jnp.int16(0)).reshape(c_ // 16, 16, c_)
                    accs[qb * per + u] = accs[qb * per + u] + functools.reduce(
                        lambda a, b: a + b, [hit[i] for i in range(c_ // 16)])
            return tuple(accs)
        accs = lax.fori_loop(0, n_tiles, body, (jnp.zeros((16, c_), jnp.int16),) * (per * nqb))
        out = []
        for qb in blocks:
            total = functools.reduce(lambda a, b: a + b, [a.astype(jnp.int32) for a in accs[qb * per:(qb + 1) * per]])
            out.append(jnp.sum(total.astype(jnp.float32), axis=0, keepdims=True))
        return out

    def kth_largest16(ref, ks):
        zero = jnp.zeros((1, c_), jnp.int32)

        def enough(cands):
            c16 = [c.astype(jnp.int16) for c in cands]
            counts = count16(lambda qb, rows: ref[qb, rows, :] >= c16[qb])
            return [n >= k for n, k in zip(counts, ks)]

        vs = tuple(jnp.where(ok, zero, i16_min) for ok in enough([zero] * nqb))

        def bit_step(i, vs):
            cands = [v | jnp.left_shift(jnp.int32(1), 14 - i) for v in vs]
            return tuple(jnp.where(ok, c, v) for ok, c, v in zip(enough(cands), cands, vs))

        return lax.fori_loop(0, 15, bit_step, vs)

    kf = jnp.float32(topk)
    thr_hi = kth_largest16(hi_ref, [kf] * nqb)
    th = [v.astype(jnp.int16) for v in thr_hi]
    k_lo = [kf - n for n in count16(lambda qb, rows: hi_ref[qb, rows, :] > th[qb])]

    def restrict_lo(t, carry):
        for u in range(per):
            rows = chunk_rows(t, u)
            for qb in blocks:
                lo_ref[qb, rows, :] = jnp.where(hi_ref[qb, rows, :] == th[qb], lo_ref[qb, rows, :],
                                                jnp.int16(-2 ** 15))
        return carry

    lax.fori_loop(0, n_tiles, restrict_lo, 0)
    thr_lo = kth_largest16(lo_ref, k_lo)
    thr_lo = [jnp.where((h == i16_min) & (l == i16_min), i16_min + 1, l) for h, l in zip(thr_hi, thr_lo)]
    tl = [v.astype(jnp.int16) for v in thr_lo]

    def cmp_key(qb, rows, lo_test):
        hi = hi_ref[qb, rows, :]
        return (hi > th[qb]) | ((hi == th[qb]) & lo_test(lo_ref[qb, rows, :], tl[qb]))

    n_ge = count16(lambda qb, rows: cmp_key(qb, rows, lambda lo, t: lo >= t))
    n_gt = count16(lambda qb, rows: cmp_key(qb, rows, lambda lo, t: lo > t))
    room = [kf - n for n in n_gt]
    left_out = functools.reduce(jnp.maximum, [jnp.where(n > kf, 1.0, 0.0) for n in n_ge])
    some_tie_left_out = jnp.max(left_out) > 0.0

    @pl.when(jnp.logical_not(some_tie_left_out))
    def _():
        def mask_tile(t, carry):
            for u in range(per):
                rows = chunk_rows(t, u)
                for qb in blocks:
                    take = cmp_key(qb, rows, lambda lo, t: lo >= t)
                    mask_ref[qb, rows, :] = jnp.where(take, jnp.ones((), mask_ref.dtype), jnp.zeros((), mask_ref.dtype))
            return carry
        lax.fori_loop(0, n_tiles, mask_tile, 0)

    @pl.when(some_tie_left_out)
    def _():
        tri = jnp.where(sub > lane, 1.0, 0.0).astype(jnp.bfloat16)
        thr = [jnp.left_shift(h, 16) | ((l + 32768) & 0xFFFF) for h, l in zip(thr_hi, thr_lo)]

        def mask_tile(t, seen):
            seen = list(seen)
            for u in range(per):
                rows = chunk_rows(t, u)
                for qb in blocks:
                    kc = (jnp.left_shift(hi_ref[qb, rows, :].astype(jnp.int32), 16)
                          | ((lo_ref[qb, rows, :].astype(jnp.int32) + 32768) & 0xFFFF))
                    eq = jnp.where(kc == thr[qb], 1.0, 0.0)
                    before = jnp.dot(tri, eq.astype(jnp.bfloat16), preferred_element_type=jnp.float32) + seen[qb]
                    take = jnp.where(kc > thr[qb], 1.0, jnp.where(before < room[qb], eq, 0.0))
                    mask_ref[qb, rows, :] = take.astype(mask_ref.dtype)
                    seen[qb] = seen[qb] + jnp.sum(eq, axis=0, keepdims=True)
            return tuple(seen)
        lax.fori_loop(0, n_tiles, mask_tile, (jnp.zeros((1, c_), jnp.float32),) * nqb)

    def zero_tile(t, carry):
        for qb in blocks:
            mask_ref[qb, pl.ds(pl.multiple_of(t * tk, tk), tk), :] = jnp.zeros((tk, c_), mask_ref.dtype)
        return carry

    lax.fori_loop(n_tiles, nq // per, zero_tile, 0)


def _dsa_select(qi, w, kidx, topk):
    batch, nq = qi.shape[0], qi.shape[1]
    seq = kidx.shape[1]
    per = min(4, nq)
    nqb = 2
    assert nq % per == 0 and per % 2 == 0 and nq % nqb == 0 and seq < 2 ** 15
    return pl.pallas_call(
        functools.partial(_dsa_select_kernel, topk=topk, nq=nq, per=per, nqb=nqb),
        grid=(batch, nq // nqb),
        in_specs=[pl.BlockSpec((None, nqb, IDX_HEADS * Q_BLOCK, IDX_DIM), lambda b, j: (b, j, 0, 0)),
                  pl.BlockSpec((None, nqb, IDX_HEADS * Q_BLOCK, LANES), lambda b, j: (b, j, 0, 0)),
                  pl.BlockSpec((None, seq, IDX_DIM), lambda b, j: (b, 0, 0), pipeline_mode=pl.Buffered(1))],
        out_specs=pl.BlockSpec((None, nqb, seq, Q_BLOCK), lambda b, j: (b, j, 0, 0)),
        out_shape=jax.ShapeDtypeStruct((batch, nq, seq, Q_BLOCK), jnp.bfloat16),
        scratch_shapes=[pltpu.VMEM((nqb, seq, Q_BLOCK), jnp.int16),
                        pltpu.VMEM((nqb, seq, Q_BLOCK), jnp.int16)],
        compiler_params=_cparams("parallel", "arbitrary"),
        name="dsa_select",
    )(qi, w, kidx)


def _dsa_attn_kernel(qg_ref, k_ref, vt_ref, kn_ref, mask_ref, dg_ref, o_ref, m_ref, acc_ref, s_ref, p_ref, alpha_ref,
                     *, tk, n_key_tiles):
    j = pl.program_id(1)
    per = tk // Q_BLOCK
    d = DSA_HEAD_DIM
    gw = DSA_GROUP * Q_BLOCK
    heads = range(DSA_KV_HEADS)
    n_live = (j + per) // per
    n_pairs = (n_live + 1) // 2
    neg_inf = jnp.float32(-jnp.inf)
    acc_ref[...] = jnp.zeros_like(acc_ref)
    p_ref[:, 1] = jnp.zeros((DSA_KV_HEADS,) + p_ref.shape[2:], p_ref.dtype)
    ones_rows = jnp.ones((acc_ref.shape[1] - d, tk), jnp.bfloat16)

    k_max2 = jnp.max(kn_ref[...], axis=0)
    ones8 = jnp.ones((8, d), jnp.bfloat16)
    bound = []
    for kv in heads:
        qf = qg_ref[kv].astype(jnp.float32)
        q_norm2 = lax.dot_general(ones8, (qf * qf).astype(jnp.bfloat16), _NT,
                                  preferred_element_type=jnp.float32)[0:1, :]
        bound.append(jnp.sqrt(q_norm2 * jnp.concatenate([k_max2[kv:kv + 1, :]] * DSA_GROUP, axis=1)) * 1.03 + 1e-3)
    bounded_is_safe = jnp.max(functools.reduce(jnp.maximum, bound)) <= 50.0

    def logits(kv, t):
        r0 = pl.multiple_of(t * tk, tk)
        return lax.dot_general(k_ref[kv, pl.ds(r0, tk), :], qg_ref[kv], _NT, preferred_element_type=jnp.float32)

    def softmax_running_max(kv, sel, slot):
        s = s_ref[kv, slot]
        s = jnp.concatenate([jnp.where(sel, s[:, g * Q_BLOCK:(g + 1) * Q_BLOCK], neg_inf)
                             for g in range(DSA_GROUP)], axis=1)
        m_old = m_ref[kv]
        m_new = jnp.maximum(m_old, jnp.max(s, axis=0, keepdims=True))
        m_safe = jnp.where(m_new == neg_inf, 0.0, m_new)
        p_ref[kv, slot] = jnp.exp2(s - m_safe).astype(p_ref.dtype)
        alpha_ref[kv, slot] = jnp.exp2(m_old - m_safe)
        m_ref[kv] = m_new

    def softmax_bounded(kv, sel, slot):
        s = s_ref[kv, slot]
        parts = []
        for g in range(DSA_GROUP):
            cols = slice(g * Q_BLOCK, (g + 1) * Q_BLOCK)
            shift = jnp.where(sel, -bound[kv][:, cols], neg_inf)
            parts.append(jnp.exp2(s[:, cols] + shift))
        p_ref[kv, slot] = jnp.concatenate(parts, axis=1).astype(p_ref.dtype)

    def pv_stage(kv, t, slot, rescale):
        vt = jnp.concatenate([vt_ref[kv, t * per + i] for i in range(per)], axis=1)
        lhs = jnp.concatenate([vt, ones_rows], axis=0)
        pv = jnp.dot(lhs, p_ref[kv, slot], preferred_element_type=jnp.float32)
        acc_ref[kv] = (acc_ref[kv] * alpha_ref[kv, slot] if rescale else acc_ref[kv]) + pv

    def run(softmax_stage, rescale):
        for kv in heads:
            s_ref[kv, 0] = logits(kv, 0)

        def pair(i, carry):
            for u in range(2):
                t = 2 * i + u
                r0 = pl.multiple_of(t * tk, tk)
                sel = mask_ref[pl.ds(r0, tk), :].astype(jnp.float32) > 0.5
                for kv in heads:
                    s_ref[kv, 1 - u] = logits(kv, jnp.minimum(t + 1, n_key_tiles - 1))
                    softmax_stage(kv, sel, u)
                    pv_stage(kv, jnp.maximum(t - 1, 0), 1 - u, rescale)
            return carry

        lax.fori_loop(0, n_pairs, pair, 0)
        for kv in heads:
            pv_stage(kv, 2 * n_pairs - 1, 1, rescale)

    @pl.when(bounded_is_safe)
    def _():
        run(softmax_bounded, rescale=False)

    @pl.when(jnp.logical_not(bounded_is_safe))
    def _():
        m_ref[...] = jnp.full_like(m_ref, neg_inf)
        alpha_ref[:, 1] = jnp.ones((DSA_KV_HEADS,) + alpha_ref.shape[2:], alpha_ref.dtype)
        run(softmax_running_max, rescale=True)

    for kv in heads:
        out_t = acc_ref[kv, :d, :] / acc_ref[kv, d:d + 1, :]
        for g in range(DSA_GROUP):
            cols = slice(kv * gw + g * Q_BLOCK, kv * gw + (g + 1) * Q_BLOCK)
            gate = dg_ref[:, cols].astype(jnp.float32)
            o_ref[:, cols] = (gate * jax.nn.sigmoid(gate)
                              * out_t[:, g * Q_BLOCK:(g + 1) * Q_BLOCK].T).astype(o_ref.dtype)


def _dsa_attention(qg, k, vt, kn, mask, h, batch, seq):
    nq = seq // Q_BLOCK
    d = DSA_HEAD_DIM
    gw = DSA_GROUP * d
    tk = min(256, seq // 2)
    assert seq % (2 * tk) == 0
    assert SEG["dg"] % BRANCH_W == 0
    kvh = DSA_KV_HEADS
    once = pl.Buffered(1)
    return pl.pallas_call(
        functools.partial(_dsa_attn_kernel, tk=tk, n_key_tiles=seq // tk),
        grid=(batch, nq),
        in_specs=[pl.BlockSpec((None, kvh, None, gw, d), lambda b, j: (b, 0, j, 0, 0)),
                  pl.BlockSpec((None, kvh, seq, d), lambda b, j: (b, 0, 0, 0), pipeline_mode=once),
                  pl.BlockSpec((None, kvh, nq, d, Q_BLOCK), lambda b, j: (b, 0, 0, 0, 0), pipeline_mode=once),
                  pl.BlockSpec((None, nq, kvh, LANES), lambda b, j: (b, 0, 0, 0), pipeline_mode=once),
                  pl.BlockSpec((None, None, seq, Q_BLOCK), lambda b, j: (b, j, 0, 0)),
                  pl.BlockSpec((Q_BLOCK, BRANCH_W), lambda b, j: (b * nq + j, SEG["dg"] // BRANCH_W))],
        out_specs=pl.BlockSpec((Q_BLOCK, BRANCH_W), lambda b, j: (b * nq + j, 0)),
        out_shape=jax.ShapeDtypeStruct((batch * seq, BRANCH_W), jnp.bfloat16),
        scratch_shapes=[pltpu.VMEM((kvh, 1, gw), jnp.float32),
                        pltpu.VMEM((kvh, d + 16, gw), jnp.float32),
                        pltpu.VMEM((kvh, 2, tk, gw), jnp.float32),
                        pltpu.VMEM((kvh, 2, tk, gw), jnp.bfloat16),
                        pltpu.VMEM((kvh, 2, 1, gw), jnp.float32)],
        compiler_params=_cparams("parallel", "arbitrary"),
        name="dsa_attn",
    )(qg, k, vt, kn, mask, h)


def _gelu_tanh(x):
    return 0.5 * x * (1.0 + jnp.tanh(np.sqrt(2.0 / np.pi).astype(np.float32) * (x + 0.044715 * (x * x * x))))


def _gmlp_kernel(u_ref, v_ref, g_ref, gain_ref, w_ref, b_ref, o_ref):
    c_ = GM_CHUNK
    v = _gelu_tanh(v_ref[...].astype(jnp.float32))
    mu = jnp.mean(v, axis=-1, keepdims=True)
    vc = v - mu
    vn = (vc * lax.rsqrt(jnp.mean(vc * vc, axis=-1, keepdims=True) + EPS) * gain_ref[...]).astype(jnp.bfloat16)
    sub = lax.broadcasted_iota(jnp.int32, (c_, c_), 0)
    lane = lax.broadcasted_iota(jnp.int32, (c_, c_), 1)
    tril = sub >= lane
    for g in range(GM_GROUPS):
        cols = slice(g * GM_GROUP_DIM, (g + 1) * GM_GROUP_DIM)
        w = jnp.where(tril, w_ref[g], jnp.zeros((), w_ref.dtype))
        mixed = jnp.dot(w, vn[:, cols], preferred_element_type=jnp.float32) + b_ref[g]
        u = _gelu_tanh(u_ref[:, cols].astype(jnp.float32))
        gate = g_ref[:, cols].astype(jnp.float32)
        o_ref[:, cols] = (gate * jax.nn.sigmoid(gate) * (u * mixed)).astype(o_ref.dtype)


def _gmlp(h, gm_gain, w_spatial, b_spatial, m):
    c_ = GM_CHUNK
    w = BRANCH_W

    def seg(name):
        assert SEG[name] % w == 0
        base = SEG[name] // w
        return pl.BlockSpec((c_, w), lambda i: (i, base))

    b_b = jnp.broadcast_to(b_spatial[:, :, None], (GM_GROUPS, c_, GM_GROUP_DIM))
    whole = lambda shape: pl.BlockSpec(shape, lambda i: (0,) * len(shape))
    return pl.pallas_call(
        _gmlp_kernel,
        grid=(m // c_,),
        in_specs=[seg("gu"), seg("gv"), seg("gg"), whole((1, w)), whole((GM_GROUPS, c_, c_)),
                  whole((GM_GROUPS, c_, GM_GROUP_DIM))],
        out_specs=pl.BlockSpec((c_, w), lambda i: (i, 0)),
        out_shape=jax.ShapeDtypeStruct((m, w), jnp.bfloat16),
        compiler_params=_cparams("parallel"),
        name="gmlp",
    )(h, h, h, gm_gain.reshape(1, w), w_spatial.astype(jnp.bfloat16), b_b)


def _merge_kernel(y0_ref, y1_ref, y2_ref, w_ref, g0_ref, g1_ref, g2_ref, o_ref):
    acc = None
    for b, (y_ref, g_ref) in enumerate(((y0_ref, g0_ref), (y1_ref, g1_ref), (y2_ref, g2_ref))):
        proj = jnp.dot(y_ref[...], w_ref[b], preferred_element_type=jnp.float32)
        term = jax.nn.sigmoid(g_ref[...].astype(jnp.float32)) * proj
        acc = term if acc is None else acc + term
    o_ref[...] = acc.astype(o_ref.dtype)


def _merge(ys, w_branch, h):
    m = h.shape[0]
    tm, tn = min(1024, m), 512
    ybs = pl.BlockSpec((tm, BRANCH_W), lambda i, j: (i, 0))

    def gate(b):
        base = (SEG["mg"] + b * D_MODEL) // tn
        return pl.BlockSpec((tm, tn), lambda i, j: (i, base + j))

    return pl.pallas_call(
        _merge_kernel,
        grid=(m // tm, D_MODEL // tn),
        in_specs=[ybs, ybs, ybs, pl.BlockSpec((N_BRANCH, BRANCH_W, tn), lambda i, j: (0, 0, j)),
                  gate(0), gate(1), gate(2)],
        out_specs=pl.BlockSpec((tm, tn), lambda i, j: (i, j)),
        out_shape=jax.ShapeDtypeStruct((m, D_MODEL), jnp.bfloat16),
        compiler_params=_cparams("parallel", "arbitrary"),
        name="merge",
    )(*ys, w_branch, h, h, h)


def _out_proj_kernel(a_ref, w_ref, x_ref, o_ref):
    o_ref[...] = x_ref[...] + jnp.dot(a_ref[...], w_ref[...], preferred_element_type=jnp.float32)


def _out_proj(merged, w_out, x2):
    m, d = x2.shape
    tm, tn = min(1024, m), 512
    return pl.pallas_call(
        _out_proj_kernel,
        grid=(m // tm, d // tn),
        in_specs=[pl.BlockSpec((tm, d), lambda i, j: (i, 0)), pl.BlockSpec((d, tn), lambda i, j: (0, j)),
                  pl.BlockSpec((tm, tn), lambda i, j: (i, j))],
        out_specs=pl.BlockSpec((tm, tn), lambda i, j: (i, j)),
        out_shape=jax.ShapeDtypeStruct((m, d), jnp.float32),
        compiler_params=_cparams("parallel", "arbitrary"),
        name="out_proj",
    )(merged, w_out, x2)


def _w_in_runs():
    runs, o = [], 0
    for name in _MY_ORDER:
        off, width = _REF_OFF[name]
        assert SEG[name] == o
        if runs and runs[-1][1] is not None and runs[-1][1] + runs[-1][2] == off and runs[-1][0] + runs[-1][2] == o:
            runs[-1][2] += width
        else:
            runs.append([o, off, width])
        padded = -(-width // LANES) * LANES
        if padded != width:
            runs.append([o + width, None, padded - width])
        o += padded
    if N_IN_PAD != o:
        runs.append([o, None, N_IN_PAD - o])
    return runs


def _relayout_plan():
    src_of = np.full((N_IN_PAD,), -1, np.int64)
    for dst, src, width in _w_in_runs():
        if src is not None:
            src_of[dst:dst + width] = np.arange(src, src + width)
    first, valid = [], []
    for c in range(N_IN_PAD // LANES):
        cols = src_of[c * LANES:(c + 1) * LANES]
        n = int((cols >= 0).sum())
        assert n > 0 and (cols[:n] == cols[0] + np.arange(n)).all() and (cols[n:] < 0).all()
        first.append(int(cols[0]))
        valid.append(n)
    return np.asarray(first, np.int32), np.asarray(valid, np.int32)


def _relayout_kernel(first_ref, valid_ref, wt_ref, o_ref):
    x = wt_ref[0].T
    lane = lax.broadcasted_iota(jnp.int32, x.shape, 1)
    o_ref[...] = jnp.where(lane < valid_ref[pl.program_id(0)], x, 0.0).astype(o_ref.dtype)


def _relayout_w_in(w_in, layer):
    _, d, n = w_in.shape
    first, valid = _relayout_plan()
    assert int((first + LANES).max()) <= n
    wt = jnp.swapaxes(w_in, 1, 2)
    return pl.pallas_call(
        _relayout_kernel,
        grid_spec=pltpu.PrefetchScalarGridSpec(
            num_scalar_prefetch=2,
            grid=(N_IN_PAD // LANES,),
            in_specs=[pl.BlockSpec((pl.Element(1), pl.Element(LANES), pl.Element(d)),
                                   lambda c, first, valid: (layer, pl.multiple_of(first[c], 8), 0))],
            out_specs=pl.BlockSpec((d, LANES), lambda c, first, valid: (0, c)),
        ),
        out_shape=jax.ShapeDtypeStruct((d, N_IN_PAD), jnp.bfloat16),
        compiler_params=_cparams("parallel"),
        name="relayout_w_in",
    )(jnp.asarray(first), jnp.asarray(valid), wt)


def kernel(x, positions, norm_gain, w_in, ret_norm_gain, q_norm_gain, k_norm_gain, gm_norm_gain, w_spatial,
           b_spatial, w_branch, w_out):
    batch, seq, d = x.shape
    assert d == D_MODEL and seq % Q_BLOCK == 0
    depth = w_in.shape[0]
    m = batch * seq
    topk = min(TOPK_MAX, seq // 4)
    cos_a, sin_a, cos_b, sin_b = _rope_tables(positions)
    x2 = x.reshape(m, d)
    for l in range(depth):
        xn = _rmsnorm(x2, norm_gain[l])
        h = _in_proj(xn, _relayout_w_in(w_in, l))
        y_ret = _retention(h, cos_a, sin_a, ret_norm_gain[l], batch, seq)
        qg, k, vt, qi, kidx, w_idx, kn = _dsa_prep(h, cos_b, sin_b, q_norm_gain[l], k_norm_gain[l], batch, seq)
        mask = _dsa_select(qi, w_idx, kidx, topk)
        y_dsa = _dsa_attention(qg, k, vt, kn, mask, h, batch, seq)
        y_gm = _gmlp(h, gm_norm_gain[l], w_spatial[l], b_spatial[l], m)
        merged = _merge((y_ret, y_dsa, y_gm), w_branch[l].astype(jnp.bfloat16), h)
        x2 = _out_proj(merged, w_out[l].astype(jnp.bfloat16), x2)
    return x2.reshape(batch, seq, d)
```

```python
import functools

import numpy as np
import jax
import jax.numpy as jnp
from jax import lax
from jax.experimental import pallas as pl
from jax.experimental.pallas import tpu as pltpu

D_MODEL = 4096
BRANCH_W = D_MODEL // 2
N_BRANCH = 3
RET_HEAD_DIM = 256
RET_HEADS = BRANCH_W // RET_HEAD_DIM
RET_CHUNK = 128
DSA_HEAD_DIM = 128
DSA_HEADS = BRANCH_W // DSA_HEAD_DIM
DSA_KV_HEADS = 4
DSA_GROUP = DSA_HEADS // DSA_KV_HEADS
DSA_KV_W = DSA_KV_HEADS * DSA_HEAD_DIM
IDX_HEADS = 32
IDX_DIM = 128
TOPK_MAX = 256
Q_BLOCK = 128
GM_GROUPS = 16
GM_GROUP_DIM = BRANCH_W // GM_GROUPS
GM_CHUNK = 128
ROPE_THETA = 10000.0
EPS = 1e-6

LANES = 128
VMEM_LIMIT = 56 * 1024 * 1024

_REF_SEGS = (("rq", BRANCH_W), ("rk", BRANCH_W), ("rv", BRANCH_W), ("rg", BRANCH_W),
             ("dq", BRANCH_W), ("dk", DSA_KV_W), ("dv", DSA_KV_W),
             ("iq", IDX_HEADS * IDX_DIM), ("ik", IDX_DIM), ("iw", IDX_HEADS),
             ("dg", BRANCH_W), ("gu", BRANCH_W), ("gv", BRANCH_W), ("gg", BRANCH_W),
             ("mg", N_BRANCH * D_MODEL))
_MY_ORDER = ("dg", "gu", "gv", "gg", "mg", "rq", "rk", "rv", "rg", "dq", "dk", "dv", "iq", "ik", "iw")
N_TILE = 768
IQ_BLOCK = 1024


def _layout():
    ref_off, o = {}, 0
    for name, w in _REF_SEGS:
        ref_off[name] = (o, w)
        o += w
    my_off, o = {}, 0
    for name in _MY_ORDER:
        w = ref_off[name][1]
        my_off[name] = o
        o += -(-w // LANES) * LANES
    total = -(-o // N_TILE) * N_TILE
    return ref_off, my_off, total


_REF_OFF, SEG, N_IN_PAD = _layout()

_NT = (((1,), (1,)), ((), ()))


def _cparams(*sem):
    return pltpu.CompilerParams(dimension_semantics=sem, vmem_limit_bytes=VMEM_LIMIT)


def _rmsnorm_kernel(x_ref, g_ref, o_ref):
    x = x_ref[...]
    ms = jnp.mean(x * x, axis=-1, keepdims=True)
    o_ref[...] = (x * lax.rsqrt(ms + EPS) * g_ref[...]).astype(o_ref.dtype)


def _rmsnorm(x2, gain):
    m, d = x2.shape
    tm = min(256, m)
    return pl.pallas_call(
        _rmsnorm_kernel,
        grid=(m // tm,),
        in_specs=[pl.BlockSpec((tm, d), lambda i: (i, 0)), pl.BlockSpec((1, d), lambda i: (0, 0))],
        out_specs=pl.BlockSpec((tm, d), lambda i: (i, 0)),
        out_shape=jax.ShapeDtypeStruct((m, d), jnp.bfloat16),
        compiler_params=_cparams("parallel"),
        name="rmsnorm",
    )(x2, gain.reshape(1, d))


def _matmul_kernel(a_ref, w_ref, o_ref):
    o_ref[...] = jnp.dot(a_ref[...], w_ref[...], preferred_element_type=jnp.float32).astype(o_ref.dtype)


def _in_proj(xn, w):
    m, k = xn.shape
    n = w.shape[1]
    tm, tn = min(1024, m), N_TILE
    return pl.pallas_call(
        _matmul_kernel,
        grid=(m // tm, n // tn),
        in_specs=[pl.BlockSpec((tm, k), lambda i, j: (i, 0)), pl.BlockSpec((k, tn), lambda i, j: (0, j))],
        out_specs=pl.BlockSpec((tm, tn), lambda i, j: (i, j)),
        out_shape=jax.ShapeDtypeStruct((m, n), jnp.bfloat16),
        compiler_params=_cparams("parallel", "arbitrary"),
        name="in_proj",
    )(xn, w)


def _rope_tables_kernel(pos_ref, inv_a_ref, inv_b_ref, sign_ref, ca_ref, sa_ref, cb_ref, sb_ref):
    p = pos_ref[...].astype(jnp.float32)
    ang_a = p * inv_a_ref[...]
    ca_ref[...] = jnp.cos(ang_a)
    sa_ref[...] = jnp.sin(ang_a)
    ang_b = p * inv_b_ref[...]
    cb_ref[...] = jnp.cos(ang_b)
    sb_ref[...] = jnp.sin(ang_b) * sign_ref[...]


def _rope_tables(positions):
    m = positions.size
    tr = min(1024, m)

    def inv_freq(d):
        half = d // 2
        return 1.0 / (ROPE_THETA ** (jnp.arange(half, dtype=jnp.float32) * 2.0 / d))

    inv_a = inv_freq(RET_HEAD_DIM).reshape(1, LANES)
    inv_b = jnp.tile(inv_freq(DSA_HEAD_DIM), 2).reshape(1, LANES)
    sign = jnp.concatenate([-jnp.ones((LANES // 2,), jnp.float32), jnp.ones((LANES // 2,), jnp.float32)]).reshape(1, LANES)
    row = pl.BlockSpec((1, LANES), lambda i: (0, 0))
    tab = pl.BlockSpec((tr, LANES), lambda i: (i, 0))
    sds = jax.ShapeDtypeStruct((m, LANES), jnp.float32)
    return pl.pallas_call(
        _rope_tables_kernel,
        grid=(m // tr,),
        in_specs=[pl.BlockSpec((tr, 1), lambda i: (i, 0)), row, row, row],
        out_specs=[tab, tab, tab, tab],
        out_shape=[sds, sds, sds, sds],
        compiler_params=_cparams("parallel"),
        name="rope_tables",
    )(positions.reshape(m, 1), inv_a, inv_b, sign)


def _retention_kernel(q_ref, k_ref, v_ref, g_ref, cos_ref, sin_ref, decay_ref, xi_ref, zeta_ref, cd_ref,
                      gain_ref, o_ref, state_ref, *, n_chunk):
    @pl.when(pl.program_id(2) == 0)
    def _():
        state_ref[...] = jnp.zeros_like(state_ref)

    half = RET_HEAD_DIM // 2
    decay = decay_ref[...]
    xi = xi_ref[...]
    zeta = zeta_ref[...]
    cd = cd_ref[0:1, :]
    gain = gain_ref[...]
    k_scale = RET_HEAD_DIM ** -0.5

    def rot(x, cos, sin):
        x1, x2 = x[:, :half], x[:, half:]
        return x1 * cos - x2 * sin, x2 * cos + x1 * sin

    for c in range(n_chunk):
        rows = slice(c * RET_CHUNK, (c + 1) * RET_CHUNK)
        cos, sin = cos_ref[rows, :], sin_ref[rows, :]
        q1, q2 = rot(q_ref[rows, :].astype(jnp.float32), cos, sin)
        k1, k2 = rot(k_ref[rows, :].astype(jnp.float32), cos, sin)
        k1, k2 = k1 * k_scale, k2 * k_scale
        v = v_ref[rows, :]
        qb = jnp.concatenate([q1, q2], axis=1).astype(jnp.bfloat16)
        kb = jnp.concatenate([k1, k2], axis=1).astype(jnp.bfloat16)
        sc = lax.dot_general(qb, kb, _NT, preferred_element_type=jnp.float32) * decay
        inner = jnp.dot(sc.astype(jnp.bfloat16), v, preferred_element_type=jnp.float32)
        state = state_ref[...]
        qx = jnp.concatenate([q1 * xi, q2 * xi], axis=1).astype(jnp.bfloat16)
        cross = jnp.dot(qx, state.astype(jnp.bfloat16), preferred_element_type=jnp.float32)
        out = inner + cross
        kz_t = jnp.concatenate([(k1 * zeta).T, (k2 * zeta).T], axis=0).astype(jnp.bfloat16)
        state_ref[...] = (state * jnp.concatenate([cd, cd], axis=1)
                          + jnp.dot(kz_t, v, preferred_element_type=jnp.float32))
        mu = jnp.mean(out, axis=-1, keepdims=True)
        oc = out - mu
        y = oc * lax.rsqrt(jnp.mean(oc * oc, axis=-1, keepdims=True) + EPS) * gain
        g = g_ref[rows, :].astype(jnp.float32)
        o_ref[rows, :] = (g * jax.nn.sigmoid(g) * y).astype(o_ref.dtype)


def _retention(h, cos_a, sin_a, ret_gain, batch, seq):
    m = batch * seq
    hd = RET_HEAD_DIM
    rb = min(1024, seq)
    nrb = seq // rb
    c = RET_CHUNK
    log_g = jnp.log(1.0 - jnp.power(2.0, -5.0 - jnp.arange(RET_HEADS, dtype=jnp.float32)))
    n = jnp.arange(c, dtype=jnp.float32)
    diff = n[:, None] - n[None, :]
    decay = jnp.where(diff >= 0, jnp.exp(log_g[:, None, None] * jnp.maximum(diff, 0.0)), 0.0)
    xi = jnp.broadcast_to(jnp.exp(log_g[:, None] * (n + 1.0))[:, :, None], (RET_HEADS, c, LANES))
    zeta = jnp.broadcast_to(jnp.exp(log_g[:, None] * (c - 1.0 - n))[:, :, None], (RET_HEADS, c, LANES))
    cd = jnp.broadcast_to(jnp.exp(log_g * c)[:, None, None], (RET_HEADS, 8, LANES))

    def seg(name):
        assert SEG[name] % hd == 0
        base = SEG[name] // hd
        return pl.BlockSpec((rb, hd), lambda b, hh, r: (b * nrb + r, base + hh))

    tab = pl.BlockSpec((rb, LANES), lambda b, hh, r: (b * nrb + r, 0))
    per_head = lambda rows: pl.BlockSpec((None, rows, LANES), lambda b, hh, r: (hh, 0, 0))
    return pl.pallas_call(
        functools.partial(_retention_kernel, n_chunk=rb // c),
        grid=(batch, RET_HEADS, nrb),
        in_specs=[seg("rq"), seg("rk"), seg("rv"), seg("rg"), tab, tab,
                  per_head(c), per_head(c), per_head(c), per_head(8),
                  pl.BlockSpec((1, hd), lambda b, hh, r: (0, hh))],
        out_specs=pl.BlockSpec((rb, hd), lambda b, hh, r: (b * nrb + r, hh)),
        out_shape=jax.ShapeDtypeStruct((m, BRANCH_W), jnp.bfloat16),
        scratch_shapes=[pltpu.VMEM((hd, hd), jnp.float32)],
        compiler_params=_cparams("parallel", "parallel", "arbitrary"),
        name="retention",
    )(h, h, h, h, cos_a, sin_a, decay, xi, zeta, cd, ret_gain.reshape(1, BRANCH_W))


def _rope_half(x, cos, sin):
    return x * cos + pltpu.roll(x, DSA_HEAD_DIM // 2, 1) * sin


def _rms_head(x, g):
    return x * lax.rsqrt(jnp.mean(x * x, axis=-1, keepdims=True) + EPS) * g


def _dsa_prep_kernel(dk_ref, dv_ref, ik_ref, cos_ref, sin_ref, kgain_ref, k_ref, vt_ref, kidx_ref, kn_ref):
    cos, sin = cos_ref[...], sin_ref[...]
    d = DSA_HEAD_DIM
    kgain = kgain_ref[...]
    for kvh in range(DSA_KV_HEADS):
        x = dk_ref[:, kvh * d:(kvh + 1) * d].astype(jnp.float32)
        kb = _rope_half(_rms_head(x, kgain), cos, sin).astype(k_ref.dtype)
        k_ref[kvh] = kb
        kf = kb.astype(jnp.float32)
        kn2 = jnp.max(jnp.sum(kf * kf, axis=-1, keepdims=True), axis=0, keepdims=True)
        kn_ref[kvh:kvh + 1, :] = jnp.broadcast_to(kn2, (1, LANES))
        vt_ref[kvh] = dv_ref[:, kvh * d:(kvh + 1) * d].astype(jnp.float32).T.astype(vt_ref.dtype)
    kidx_ref[...] = _rope_half(ik_ref[...].astype(jnp.float32), cos, sin).astype(kidx_ref.dtype)


def _h_block(name, width, rows, row_index, part=0):
    assert SEG[name] % width == 0
    base = SEG[name] // width + part
    return pl.BlockSpec((rows, width), lambda b, j: (row_index(b, j), base))


def _dsa_prep(h, cos_b, sin_b, k_gain, batch, seq):
    nq = seq // Q_BLOCK
    d = DSA_HEAD_DIM
    row = lambda b, j: b * nq + j
    tab = pl.BlockSpec((Q_BLOCK, LANES), lambda b, j: (row(b, j), 0))
    bf = jnp.bfloat16
    return pl.pallas_call(
        _dsa_prep_kernel,
        grid=(batch, nq),
        in_specs=[_h_block("dk", DSA_KV_W, Q_BLOCK, row), _h_block("dv", DSA_KV_W, Q_BLOCK, row),
                  _h_block("ik", IDX_DIM, Q_BLOCK, row), tab, tab, pl.BlockSpec((1, d), lambda b, j: (0, 0))],
        out_specs=[
            pl.BlockSpec((None, DSA_KV_HEADS, Q_BLOCK, d), lambda b, j: (b, 0, j, 0)),
            pl.BlockSpec((None, DSA_KV_HEADS, None, d, Q_BLOCK), lambda b, j: (b, 0, j, 0, 0)),
            pl.BlockSpec((None, Q_BLOCK, IDX_DIM), lambda b, j: (b, j, 0)),
            pl.BlockSpec((None, None, DSA_KV_HEADS, LANES), lambda b, j: (b, j, 0, 0)),
        ],
        out_shape=[
            jax.ShapeDtypeStruct((batch, DSA_KV_HEADS, seq, d), bf),
            jax.ShapeDtypeStruct((batch, DSA_KV_HEADS, nq, d, Q_BLOCK), bf),
            jax.ShapeDtypeStruct((batch, seq, IDX_DIM), bf),
            jax.ShapeDtypeStruct((batch, nq, DSA_KV_HEADS, LANES), jnp.float32),
        ],
        compiler_params=_cparams("parallel", "parallel"),
        name="dsa_prep",
    )(h, h, h, cos_b, sin_b, k_gain.reshape(1, d))


def _dsa_select_kernel(iq0_ref, iq1_ref, iq2_ref, iq3_ref, iw_ref, cos_ref, sin_ref, kidx_ref, mask_ref,
                       hi_ref, lo_ref, qi_ref, w_ref, *, topk, nq, per, nqb):
    jp = pl.program_id(1)
    c_ = Q_BLOCK
    tk = per * c_
    blocks = range(nqb)
    iq_refs = (iq0_ref, iq1_ref, iq2_ref, iq3_ref)
    for qb in blocks:
        tok = slice(qb * c_, (qb + 1) * c_)
        cos, sin = cos_ref[tok, :], sin_ref[tok, :]
        w = iw_ref[tok, :].astype(jnp.float32) * (IDX_DIM ** -0.5 * IDX_HEADS ** -0.5)
        for head in range(IDX_HEADS):
            part, col = divmod(head * IDX_DIM, IQ_BLOCK)
            x = iq_refs[part][tok, col:col + IDX_DIM].astype(jnp.float32)
            qi_ref[qb, head * c_:(head + 1) * c_, :] = _rope_half(x, cos, sin).astype(qi_ref.dtype)
            w_ref[qb, head * c_:(head + 1) * c_, :] = jnp.broadcast_to(w[:, head:head + 1], (c_, LANES))
    sub = lax.broadcasted_iota(jnp.int32, (c_, c_), 0)
    lane = lax.broadcasted_iota(jnp.int32, (c_, c_), 1)
    int_min = jnp.int32(-2 ** 31)
    i16_min = jnp.int32(-2 ** 15)
    n_tiles = (jp * nqb + nqb - 1 + per) // per

    def causal(qb, chunk):
        return (chunk * c_ + sub) <= ((jp * nqb + qb) * c_ + lane)

    def chunk_rows(t, u):
        return pl.ds(pl.multiple_of((t * per + u) * c_, c_), c_)

    def store_keys(qb, chunk, key):
        rows = pl.ds(pl.multiple_of(chunk * c_, c_), c_)
        hi_ref[qb, rows, :] = (key >> 16).astype(jnp.int16)
        lo_ref[qb, rows, :] = ((key & 0xFFFF) - 32768).astype(jnp.int16)

    def score_tile(t, carry):
        for u2 in range(per // 2):
            first = t * per + 2 * u2
            kt = kidx_ref[pl.ds(pl.multiple_of(first * c_, 2 * c_), 2 * c_), :]
            for qb in blocks:
                s = lax.dot_general(qi_ref[qb], kt, _NT, preferred_element_type=jnp.float32)
                acc = [jnp.zeros((c_, c_), jnp.float32), jnp.zeros((c_, c_), jnp.float32)]
                for head in range(IDX_HEADS):
                    rows = slice(head * c_, (head + 1) * c_)
                    w = w_ref[qb, rows, :]
                    for half in range(2):
                        acc[half] = acc[half] + jnp.maximum(s[rows, half * c_:(half + 1) * c_], 0.0) * w
                for half in range(2):
                    bits = lax.bitcast_convert_type(acc[half].T, jnp.int32)
                    key = bits ^ ((bits >> 31) & jnp.int32(0x7FFFFFFF))
                    store_keys(qb, first + half, jnp.where(causal(qb, first + half), key, int_min))
        return carry

    lax.fori_loop(0, n_tiles, score_tile, 0)

    def count16(hits):
        def body(t, accs):
            accs = list(accs)
            for u in range(per):
                rows = chunk_rows(t, u)
                for qb in blocks:
                    hit = jnp.where(hits(qb, rows), jnp.int16(1), jnp.int16(0)).reshape(c_ // 16, 16, c_)
                    accs[qb * per + u] = accs[qb * per + u] + functools.reduce(
                        lambda a, b: a + b, [hit[i] for i in range(c_ // 16)])
            return tuple(accs)
        accs = lax.fori_loop(0, n_tiles, body, (jnp.zeros((16, c_), jnp.int16),) * (per * nqb))
        out = []
        for qb in blocks:
            total = functools.reduce(lambda a, b: a + b, [a.astype(jnp.int32) for a in accs[qb * per:(qb + 1) * per]])
            out.append(jnp.sum(total.astype(jnp.float32), axis=0, keepdims=True))
        return out

    def kth_largest16(ref, ks):
        zero = jnp.zeros((1, c_), jnp.int32)

        def enough(cands):
            c16 = [c.astype(jnp.int16) for c in cands]
            counts = count16(lambda qb, rows: ref[qb, rows, :] >= c16[qb])
            return [n >= k for n, k in zip(counts, ks)]

        vs = tuple(jnp.where(ok, zero, i16_min) for ok in enough([zero] * nqb))

        def bit_step(i, vs):
            cands = [v | jnp.left_shift(jnp.int32(1), 14 - i) for v in vs]
            return tuple(jnp.where(ok, c, v) for ok, c, v in zip(enough(cands), cands, vs))

        return lax.fori_loop(0, 15, bit_step, vs)

    kf = jnp.float32(topk)
    thr_hi = kth_largest16(hi_ref, [kf] * nqb)
    th = [v.astype(jnp.int16) for v in thr_hi]
    k_lo = [kf - n for n in count16(lambda qb, rows: hi_ref[qb, rows, :] > th[qb])]

    def restrict_lo(t, carry):
        for u in range(per):
            rows = chunk_rows(t, u)
            for qb in blocks:
                lo_ref[qb, rows, :] = jnp.where(hi_ref[qb, rows, :] == th[qb], lo_ref[qb, rows, :],
                                                jnp.int16(-2 ** 15))
        return carry

    lax.fori_loop(0, n_tiles, restrict_lo, 0)
    thr_lo = kth_largest16(lo_ref, k_lo)
    thr_lo = [jnp.where((h == i16_min) & (l == i16_min), i16_min + 1, l) for h, l in zip(thr_hi, thr_lo)]
    tl = [v.astype(jnp.int16) for v in thr_lo]

    def cmp_key(qb, rows, lo_test):
        hi = hi_ref[qb, rows, :]
        return (hi > th[qb]) | ((hi == th[qb]) & lo_test(lo_ref[qb, rows, :], tl[qb]))

    n_ge = count16(lambda qb, rows: cmp_key(qb, rows, lambda lo, t: lo >= t))
    n_gt = count16(lambda qb, rows: cmp_key(qb, rows, lambda lo, t: lo > t))
    room = [kf - n for n in n_gt]
    left_out = functools.reduce(jnp.maximum, [jnp.where(n > kf, 1.0, 0.0) for n in n_ge])
    some_tie_left_out = jnp.max(left_out) > 0.0

    @pl.when(jnp.logical_not(some_tie_left_out))
    def _():
        def mask_tile(t, carry):
            for u in range(per):
                rows = chunk_rows(t, u)
                for qb in blocks:
                    take = cmp_key(qb, rows, lambda lo, t: lo >= t)
                    mask_ref[qb, rows, :] = jnp.where(take, jnp.ones((), mask_ref.dtype), jnp.zeros((), mask_ref.dtype))
            return carry
        lax.fori_loop(0, n_tiles, mask_tile, 0)

    @pl.when(some_tie_left_out)
    def _():
        tri = jnp.where(sub > lane, 1.0, 0.0).astype(jnp.bfloat16)
        thr = [jnp.left_shift(h, 16) | ((l + 32768) & 0xFFFF) for h, l in zip(thr_hi, thr_lo)]

        def mask_tile(t, seen):
            seen = list(seen)
            for u in range(per):
                rows = chunk_rows(t, u)
                for qb in blocks:
                    kc = (jnp.left_shift(hi_ref[qb, rows, :].astype(jnp.int32), 16)
                          | ((lo_ref[qb, rows, :].astype(jnp.int32) + 32768) & 0xFFFF))
                    eq = jnp.where(kc == thr[qb], 1.0, 0.0)
                    before = jnp.dot(tri, eq.astype(jnp.bfloat16), preferred_element_type=jnp.float32) + seen[qb]
                    take = jnp.where(kc > thr[qb], 1.0, jnp.where(before < room[qb], eq, 0.0))
                    mask_ref[qb, rows, :] = take.astype(mask_ref.dtype)
                    seen[qb] = seen[qb] + jnp.sum(eq, axis=0, keepdims=True)
            return tuple(seen)
        lax.fori_loop(0, n_tiles, mask_tile, (jnp.zeros((1, c_), jnp.float32),) * nqb)

    def zero_tile(t, carry):
        for qb in blocks:
            mask_ref[qb, pl.ds(pl.multiple_of(t * tk, tk), tk), :] = jnp.zeros((tk, c_), mask_ref.dtype)
        return carry

    lax.fori_loop(n_tiles, nq // per, zero_tile, 0)


def _dsa_select(h, cos_b, sin_b, kidx, topk):
    batch, seq = kidx.shape[0], kidx.shape[1]
    nq = seq // Q_BLOCK
    per = min(4, nq)
    nqb = 2
    assert nq % per == 0 and per % 2 == 0 and nq % nqb == 0 and seq < 2 ** 15
    rows = nqb * Q_BLOCK
    row = lambda b, j: b * (nq // nqb) + j
    n_iq = IDX_HEADS * IDX_DIM // IQ_BLOCK
    assert n_iq == 4
    tab = pl.BlockSpec((rows, LANES), lambda b, j: (row(b, j), 0))
    return pl.pallas_call(
        functools.partial(_dsa_select_kernel, topk=topk, nq=nq, per=per, nqb=nqb),
        grid=(batch, nq // nqb),
        in_specs=[_h_block("iq", IQ_BLOCK, rows, row, part) for part in range(n_iq)]
                 + [_h_block("iw", LANES, rows, row), tab, tab,
                    pl.BlockSpec((None, seq, IDX_DIM), lambda b, j: (b, 0, 0), pipeline_mode=pl.Buffered(1))],
        out_specs=pl.BlockSpec((None, nqb, seq, Q_BLOCK), lambda b, j: (b, j, 0, 0)),
        out_shape=jax.ShapeDtypeStruct((batch, nq, seq, Q_BLOCK), jnp.bfloat16),
        scratch_shapes=[pltpu.VMEM((nqb, seq, Q_BLOCK), jnp.int16),
                        pltpu.VMEM((nqb, seq, Q_BLOCK), jnp.int16),
                        pltpu.VMEM((nqb, IDX_HEADS * Q_BLOCK, IDX_DIM), jnp.bfloat16),
                        pltpu.VMEM((nqb, IDX_HEADS * Q_BLOCK, LANES), jnp.float32)],
        compiler_params=_cparams("parallel", "arbitrary"),
        name="dsa_select",
    )(*([h] * (n_iq + 1)), cos_b, sin_b, kidx)


def _dsa_attn_kernel(dq_ref, cos_ref, sin_ref, qgain_ref, k_ref, vt_ref, kn_ref, mask_ref, dg_ref, o_ref,
                     qg_ref, m_ref, acc_ref, s_ref, p_ref, alpha_ref, *, tk, n_key_tiles):
    j = pl.program_id(1)
    per = tk // Q_BLOCK
    d = DSA_HEAD_DIM
    gw = DSA_GROUP * Q_BLOCK
    heads = range(DSA_KV_HEADS)
    n_live = (j + per) // per
    n_pairs = (n_live + 1) // 2
    neg_inf = jnp.float32(-jnp.inf)
    q_scale = DSA_HEAD_DIM ** -0.5 * np.log2(np.e).astype(np.float32)
    cos, sin, qgain = cos_ref[...], sin_ref[...], qgain_ref[...]
    for kv in heads:
        for g in range(DSA_GROUP):
            hh = kv * DSA_GROUP + g
            x = dq_ref[:, hh * d:(hh + 1) * d].astype(jnp.float32)
            qg_ref[kv, g * Q_BLOCK:(g + 1) * Q_BLOCK, :] = (
                _rope_half(_rms_head(x, qgain), cos, sin) * q_scale).astype(qg_ref.dtype)
    acc_ref[...] = jnp.zeros_like(acc_ref)
    p_ref[:, 1] = jnp.zeros((DSA_KV_HEADS,) + p_ref.shape[2:], p_ref.dtype)
    ones_rows = jnp.ones((acc_ref.shape[1] - d, tk), jnp.bfloat16)

    k_max2 = jnp.max(kn_ref[...], axis=0)
    ones8 = jnp.ones((8, d), jnp.bfloat16)
    bound = []
    for kv in heads:
        qf = qg_ref[kv].astype(jnp.float32)
        q_norm2 = lax.dot_general(ones8, (qf * qf).astype(jnp.bfloat16), _NT,
                                  preferred_element_type=jnp.float32)[0:1, :]
        bound.append(jnp.sqrt(q_norm2 * jnp.concatenate([k_max2[kv:kv + 1, :]] * DSA_GROUP, axis=1)) * 1.03 + 1e-3)
    bounded_is_safe = jnp.max(functools.reduce(jnp.maximum, bound)) <= 50.0

    def logits(kv, t):
        r0 = pl.multiple_of(t * tk, tk)
        return lax.dot_general(k_ref[kv, pl.ds(r0, tk), :], qg_ref[kv], _NT, preferred_element_type=jnp.float32)

    def softmax_running_max(kv, sel, slot):
        s = s_ref[kv, slot]
        s = jnp.concatenate([jnp.where(sel, s[:, g * Q_BLOCK:(g + 1) * Q_BLOCK], neg_inf)
                             for g in range(DSA_GROUP)], axis=1)
        m_old = m_ref[kv]
        m_new = jnp.maximum(m_old, jnp.max(s, axis=0, keepdims=True))
        m_safe = jnp.where(m_new == neg_inf, 0.0, m_new)
        p_ref[kv, slot] = jnp.exp2(s - m_safe).astype(p_ref.dtype)
        alpha_ref[kv, slot] = jnp.exp2(m_old - m_safe)
        m_ref[kv] = m_new

    def softmax_bounded(kv, sel, slot):
        s = s_ref[kv, slot]
        parts = []
        for g in range(DSA_GROUP):
            cols = slice(g * Q_BLOCK, (g + 1) * Q_BLOCK)
            shift = jnp.where(sel, -bound[kv][:, cols], neg_inf)
            parts.append(jnp.exp2(s[:, cols] + shift))
        p_ref[kv, slot] = jnp.concatenate(parts, axis=1).astype(p_ref.dtype)

    def pv_stage(kv, t, slot, rescale):
        vt = jnp.concatenate([vt_ref[kv, t * per + i] for i in range(per)], axis=1)
        lhs = jnp.concatenate([vt, ones_rows], axis=0)
        pv = jnp.dot(lhs, p_ref[kv, slot], preferred_element_type=jnp.float32)
        acc_ref[kv] = (acc_ref[kv] * alpha_ref[kv, slot] if rescale else acc_ref[kv]) + pv

    def run(softmax_stage, rescale):
        for kv in heads:
            s_ref[kv, 0] = logits(kv, 0)

        def pair(i, carry):
            for u in range(2):
                t = 2 * i + u
                r0 = pl.multiple_of(t * tk, tk)
                sel = mask_ref[pl.ds(r0, tk), :].astype(jnp.float32) > 0.5
                for kv in heads:
                    s_ref[kv, 1 - u] = logits(kv, jnp.minimum(t + 1, n_key_tiles - 1))
                    softmax_stage(kv, sel, u)
                    pv_stage(kv, jnp.maximum(t - 1, 0), 1 - u, rescale)
            return carry

        lax.fori_loop(0, n_pairs, pair, 0)
        for kv in heads:
            pv_stage(kv, 2 * n_pairs - 1, 1, rescale)

    @pl.when(bounded_is_safe)
    def _():
        run(softmax_bounded, rescale=False)

    @pl.when(jnp.logical_not(bounded_is_safe))
    def _():
        m_ref[...] = jnp.full_like(m_ref, neg_inf)
        alpha_ref[:, 1] = jnp.ones((DSA_KV_HEADS,) + alpha_ref.shape[2:], alpha_ref.dtype)
        run(softmax_running_max, rescale=True)

    for kv in heads:
        out_t = acc_ref[kv, :d, :] / acc_ref[kv, d:d + 1, :]
        for g in range(DSA_GROUP):
            cols = slice(kv * gw + g * Q_BLOCK, kv * gw + (g + 1) * Q_BLOCK)
            gate = dg_ref[:, cols].astype(jnp.float32)
            o_ref[:, cols] = (gate * jax.nn.sigmoid(gate)
                              * out_t[:, g * Q_BLOCK:(g + 1) * Q_BLOCK].T).astype(o_ref.dtype)


def _dsa_attention(k, vt, kn, mask, h, cos_b, sin_b, q_gain, batch, seq):
    nq = seq // Q_BLOCK
    d = DSA_HEAD_DIM
    gw = DSA_GROUP * d
    tk = min(256, seq // 2)
    assert seq % (2 * tk) == 0
    assert SEG["dg"] % BRANCH_W == 0
    kvh = DSA_KV_HEADS
    once = pl.Buffered(1)
    row = lambda b, j: b * nq + j
    tab = pl.BlockSpec((Q_BLOCK, LANES), lambda b, j: (row(b, j), 0))
    return pl.pallas_call(
        functools.partial(_dsa_attn_kernel, tk=tk, n_key_tiles=seq // tk),
        grid=(batch, nq),
        in_specs=[_h_block("dq", BRANCH_W, Q_BLOCK, row), tab, tab, pl.BlockSpec((1, d), lambda b, j: (0, 0)),
                  pl.BlockSpec((None, kvh, seq, d), lambda b, j: (b, 0, 0, 0), pipeline_mode=once),
                  pl.BlockSpec((None, kvh, nq, d, Q_BLOCK), lambda b, j: (b, 0, 0, 0, 0), pipeline_mode=once),
                  pl.BlockSpec((None, nq, kvh, LANES), lambda b, j: (b, 0, 0, 0), pipeline_mode=once),
                  pl.BlockSpec((None, None, seq, Q_BLOCK), lambda b, j: (b, j, 0, 0)),
                  pl.BlockSpec((Q_BLOCK, BRANCH_W), lambda b, j: (b * nq + j, SEG["dg"] // BRANCH_W))],
        out_specs=pl.BlockSpec((Q_BLOCK, BRANCH_W), lambda b, j: (b * nq + j, 0)),
        out_shape=jax.ShapeDtypeStruct((batch * seq, BRANCH_W), jnp.bfloat16),
        scratch_shapes=[pltpu.VMEM((kvh, gw, d), jnp.bfloat16),
                        pltpu.VMEM((kvh, 1, gw), jnp.float32),
                        pltpu.VMEM((kvh, d + 16, gw), jnp.float32),
                        pltpu.VMEM((kvh, 2, tk, gw), jnp.float32),
                        pltpu.VMEM((kvh, 2, tk, gw), jnp.bfloat16),
                        pltpu.VMEM((kvh, 2, 1, gw), jnp.float32)],
        compiler_params=_cparams("parallel", "arbitrary"),
        name="dsa_attn",
    )(h, cos_b, sin_b, q_gain.reshape(1, d), k, vt, kn, mask, h)


def _gelu_tanh(x):
    return 0.5 * x * (1.0 + jnp.tanh(np.sqrt(2.0 / np.pi).astype(np.float32) * (x + 0.044715 * (x * x * x))))


def _gmlp_kernel(u_ref, v_ref, g_ref, gain_ref, w_ref, b_ref, o_ref):
    c_ = GM_CHUNK
    v = _gelu_tanh(v_ref[...].astype(jnp.float32))
    mu = jnp.mean(v, axis=-1, keepdims=True)
    vc = v - mu
    vn = (vc * lax.rsqrt(jnp.mean(vc * vc, axis=-1, keepdims=True) + EPS) * gain_ref[...]).astype(jnp.bfloat16)
    sub = lax.broadcasted_iota(jnp.int32, (c_, c_), 0)
    lane = lax.broadcasted_iota(jnp.int32, (c_, c_), 1)
    tril = sub >= lane
    for g in range(GM_GROUPS):
        cols = slice(g * GM_GROUP_DIM, (g + 1) * GM_GROUP_DIM)
        w = jnp.where(tril, w_ref[g], jnp.zeros((), w_ref.dtype))
        mixed = jnp.dot(w, vn[:, cols], preferred_element_type=jnp.float32) + b_ref[g]
        u = _gelu_tanh(u_ref[:, cols].astype(jnp.float32))
        gate = g_ref[:, cols].astype(jnp.float32)
        o_ref[:, cols] = (gate * jax.nn.sigmoid(gate) * (u * mixed)).astype(o_ref.dtype)


def _gmlp(h, gm_gain, w_spatial, b_spatial, m):
    c_ = GM_CHUNK
    w = BRANCH_W

    def seg(name):
        assert SEG[name] % w == 0
        base = SEG[name] // w
        return pl.BlockSpec((c_, w), lambda i: (i, base))

    b_b = jnp.broadcast_to(b_spatial[:, :, None], (GM_GROUPS, c_, GM_GROUP_DIM))
    whole = lambda shape: pl.BlockSpec(shape, lambda i: (0,) * len(shape))
    return pl.pallas_call(
        _gmlp_kernel,
        grid=(m // c_,),
        in_specs=[seg("gu"), seg("gv"), seg("gg"), whole((1, w)), whole((GM_GROUPS, c_, c_)),
                  whole((GM_GROUPS, c_, GM_GROUP_DIM))],
        out_specs=pl.BlockSpec((c_, w), lambda i: (i, 0)),
        out_shape=jax.ShapeDtypeStruct((m, w), jnp.bfloat16),
        compiler_params=_cparams("parallel"),
        name="gmlp",
    )(h, h, h, gm_gain.reshape(1, w), w_spatial.astype(jnp.bfloat16), b_b)


def _merge_kernel(y0_ref, y1_ref, y2_ref, w_ref, g0_ref, g1_ref, g2_ref, o_ref):
    acc = None
    for b, (y_ref, g_ref) in enumerate(((y0_ref, g0_ref), (y1_ref, g1_ref), (y2_ref, g2_ref))):
        proj = jnp.dot(y_ref[...], w_ref[b], preferred_element_type=jnp.float32)
        term = jax.nn.sigmoid(g_ref[...].astype(jnp.float32)) * proj
        acc = term if acc is None else acc + term
    o_ref[...] = acc.astype(o_ref.dtype)


def _merge(ys, w_branch, h):
    m = h.shape[0]
    tm, tn = min(1024, m), 512
    ybs = pl.BlockSpec((tm, BRANCH_W), lambda i, j: (i, 0))

    def gate(b):
        base = (SEG["mg"] + b * D_MODEL) // tn
        return pl.BlockSpec((tm, tn), lambda i, j: (i, base + j))

    return pl.pallas_call(
        _merge_kernel,
        grid=(m // tm, D_MODEL // tn),
        in_specs=[ybs, ybs, ybs, pl.BlockSpec((N_BRANCH, BRANCH_W, tn), lambda i, j: (0, 0, j)),
                  gate(0), gate(1), gate(2)],
        out_specs=pl.BlockSpec((tm, tn), lambda i, j: (i, j)),
        out_shape=jax.ShapeDtypeStruct((m, D_MODEL), jnp.bfloat16),
        compiler_params=_cparams("parallel", "arbitrary"),
        name="merge",
    )(*ys, w_branch, h, h, h)


def _out_proj_kernel(a_ref, w_ref, x_ref, o_ref):
    o_ref[...] = x_ref[...] + jnp.dot(a_ref[...], w_ref[...], preferred_element_type=jnp.float32)


def _out_proj(merged, w_out, x2):
    m, d = x2.shape
    tm, tn = min(1024, m), 512
    return pl.pallas_call(
        _out_proj_kernel,
        grid=(m // tm, d // tn),
        in_specs=[pl.BlockSpec((tm, d), lambda i, j: (i, 0)), pl.BlockSpec((d, tn), lambda i, j: (0, j)),
                  pl.BlockSpec((tm, tn), lambda i, j: (i, j))],
        out_specs=pl.BlockSpec((tm, tn), lambda i, j: (i, j)),
        out_shape=jax.ShapeDtypeStruct((m, d), jnp.float32),
        compiler_params=_cparams("parallel", "arbitrary"),
        name="out_proj",
    )(merged, w_out, x2)


def _w_in_runs():
    runs, o = [], 0
    for name in _MY_ORDER:
        off, width = _REF_OFF[name]
        assert SEG[name] == o
        if runs and runs[-1][1] is not None and runs[-1][1] + runs[-1][2] == off and runs[-1][0] + runs[-1][2] == o:
            runs[-1][2] += width
        else:
            runs.append([o, off, width])
        padded = -(-width // LANES) * LANES
        if padded != width:
            runs.append([o + width, None, padded - width])
        o += padded
    if N_IN_PAD != o:
        runs.append([o, None, N_IN_PAD - o])
    return runs


def _relayout_plan():
    src_of = np.full((N_IN_PAD,), -1, np.int64)
    for dst, src, width in _w_in_runs():
        if src is not None:
            src_of[dst:dst + width] = np.arange(src, src + width)
    first, valid = [], []
    for c in range(N_IN_PAD // LANES):
        cols = src_of[c * LANES:(c + 1) * LANES]
        n = int((cols >= 0).sum())
        assert n > 0 and (cols[:n] == cols[0] + np.arange(n)).all() and (cols[n:] < 0).all()
        first.append(int(cols[0]))
        valid.append(n)
    return np.asarray(first, np.int32), np.asarray(valid, np.int32)


def _relayout_kernel(first_ref, valid_ref, wt_ref, o_ref):
    x = wt_ref[0].T
    lane = lax.broadcasted_iota(jnp.int32, x.shape, 1)
    o_ref[...] = jnp.where(lane < valid_ref[pl.program_id(0)], x, 0.0).astype(o_ref.dtype)


def _relayout_w_in(w_in, layer):
    _, d, n = w_in.shape
    first, valid = _relayout_plan()
    assert int((first + LANES).max()) <= n
    wt = jnp.swapaxes(w_in, 1, 2)
    return pl.pallas_call(
        _relayout_kernel,
        grid_spec=pltpu.PrefetchScalarGridSpec(
            num_scalar_prefetch=2,
            grid=(N_IN_PAD // LANES,),
            in_specs=[pl.BlockSpec((pl.Element(1), pl.Element(LANES), pl.Element(d)),
                                   lambda c, first, valid: (layer, pl.multiple_of(first[c], 8), 0))],
            out_specs=pl.BlockSpec((d, LANES), lambda c, first, valid: (0, c)),
        ),
        out_shape=jax.ShapeDtypeStruct((d, N_IN_PAD), jnp.bfloat16),
        compiler_params=_cparams("parallel"),
        name="relayout_w_in",
    )(jnp.asarray(first), jnp.asarray(valid), wt)


def kernel(x, positions, norm_gain, w_in, ret_norm_gain, q_norm_gain, k_norm_gain, gm_norm_gain, w_spatial,
           b_spatial, w_branch, w_out):
    batch, seq, d = x.shape
    assert d == D_MODEL and seq % Q_BLOCK == 0
    depth = w_in.shape[0]
    m = batch * seq
    topk = min(TOPK_MAX, seq // 4)
    cos_a, sin_a, cos_b, sin_b = _rope_tables(positions)
    x2 = x.reshape(m, d)
    for l in range(depth):
        xn = _rmsnorm(x2, norm_gain[l])
        h = _in_proj(xn, _relayout_w_in(w_in, l))
        y_ret = _retention(h, cos_a, sin_a, ret_norm_gain[l], batch, seq)
        k, vt, kidx, kn = _dsa_prep(h, cos_b, sin_b, k_norm_gain[l], batch, seq)
        mask = _dsa_select(h, cos_b, sin_b, kidx, topk)
        y_dsa = _dsa_attention(k, vt, kn, mask, h, cos_b, sin_b, q_norm_gain[l], batch, seq)
        y_gm = _gmlp(h, gm_norm_gain[l], w_spatial[l], b_spatial[l], m)
        merged = _merge((y_ret, y_dsa, y_gm), w_branch[l].astype(jnp.bfloat16), h)
        x2 = _out_proj(merged, w_out[l].astype(jnp.bfloat16), x2)
    return x2.reshape(batch, seq, d)
```

```python
import functools

import numpy as np
import jax
import jax.numpy as jnp
from jax import lax
from jax.experimental import pallas as pl
from jax.experimental.pallas import tpu as pltpu

D_MODEL = 4096
BRANCH_W = D_MODEL // 2
N_BRANCH = 3
RET_HEAD_DIM = 256
RET_HEADS = BRANCH_W // RET_HEAD_DIM
RET_CHUNK = 128
DSA_HEAD_DIM = 128
DSA_HEADS = BRANCH_W // DSA_HEAD_DIM
DSA_KV_HEADS = 4
DSA_GROUP = DSA_HEADS // DSA_KV_HEADS
DSA_KV_W = DSA_KV_HEADS * DSA_HEAD_DIM
IDX_HEADS = 32
IDX_DIM = 128
TOPK_MAX = 256
Q_BLOCK = 128
GM_GROUPS = 16
GM_GROUP_DIM = BRANCH_W // GM_GROUPS
GM_CHUNK = 128
ROPE_THETA = 10000.0
EPS = 1e-6

LANES = 128
VMEM_LIMIT = 56 * 1024 * 1024

_REF_SEGS = (("rq", BRANCH_W), ("rk", BRANCH_W), ("rv", BRANCH_W), ("rg", BRANCH_W),
             ("dq", BRANCH_W), ("dk", DSA_KV_W), ("dv", DSA_KV_W),
             ("iq", IDX_HEADS * IDX_DIM), ("ik", IDX_DIM), ("iw", IDX_HEADS),
             ("dg", BRANCH_W), ("gu", BRANCH_W), ("gv", BRANCH_W), ("gg", BRANCH_W),
             ("mg", N_BRANCH * D_MODEL))
_MY_ORDER = ("dg", "gu", "gv", "gg", "mg", "rq", "rk", "rv", "rg", "dq", "dk", "dv", "iq", "ik", "iw")
N_TILE = 768
IQ_BLOCK = 1024


def _layout():
    ref_off, o = {}, 0
    for name, w in _REF_SEGS:
        ref_off[name] = (o, w)
        o += w
    my_off, o = {}, 0
    for name in _MY_ORDER:
        w = ref_off[name][1]
        my_off[name] = o
        o += -(-w // LANES) * LANES
    total = -(-o // N_TILE) * N_TILE
    return ref_off, my_off, total


_REF_OFF, SEG, N_IN_PAD = _layout()

_NT = (((1,), (1,)), ((), ()))


def _cparams(*sem):
    return pltpu.CompilerParams(dimension_semantics=sem, vmem_limit_bytes=VMEM_LIMIT)


def _rmsnorm_kernel(x_ref, g_ref, o_ref):
    x = x_ref[...]
    ms = jnp.mean(x * x, axis=-1, keepdims=True)
    o_ref[...] = (x * lax.rsqrt(ms + EPS) * g_ref[...]).astype(o_ref.dtype)


def _rmsnorm(x2, gain):
    m, d = x2.shape
    tm = min(512, m)
    return pl.pallas_call(
        _rmsnorm_kernel,
        grid=(m // tm,),
        in_specs=[pl.BlockSpec((tm, d), lambda i: (i, 0)), pl.BlockSpec((1, d), lambda i: (0, 0))],
        out_specs=pl.BlockSpec((tm, d), lambda i: (i, 0)),
        out_shape=jax.ShapeDtypeStruct((m, d), jnp.bfloat16),
        compiler_params=_cparams("parallel"),
        name="rmsnorm",
    )(x2, gain.reshape(1, d))


def _matmul_kernel(a_ref, w_ref, o_ref):
    o_ref[...] = jnp.dot(a_ref[...], w_ref[...], preferred_element_type=jnp.float32).astype(o_ref.dtype)


def _in_proj(xn, w):
    m, k = xn.shape
    n = w.shape[1]
    tm, tn = min(1024, m), N_TILE
    return pl.pallas_call(
        _matmul_kernel,
        grid=(m // tm, n // tn),
        in_specs=[pl.BlockSpec((tm, k), lambda i, j: (i, 0)), pl.BlockSpec((k, tn), lambda i, j: (0, j))],
        out_specs=pl.BlockSpec((tm, tn), lambda i, j: (i, j)),
        out_shape=jax.ShapeDtypeStruct((m, n), jnp.bfloat16),
        compiler_params=_cparams("parallel", "arbitrary"),
        name="in_proj",
    )(xn, w)


def _rope_tables_kernel(pos_ref, inv_a_ref, inv_b_ref, sign_ref, ca_ref, sa_ref, cb_ref, sb_ref):
    p = pos_ref[...].astype(jnp.float32)
    ang_a = p * inv_a_ref[...]
    ca_ref[...] = jnp.cos(ang_a)
    sa_ref[...] = jnp.sin(ang_a)
    ang_b = p * inv_b_ref[...]
    cb_ref[...] = jnp.cos(ang_b)
    sb_ref[...] = jnp.sin(ang_b) * sign_ref[...]


def _rope_tables(positions):
    m = positions.size
    tr = min(1024, m)

    def inv_freq(d):
        half = d // 2
        return 1.0 / (ROPE_THETA ** (jnp.arange(half, dtype=jnp.float32) * 2.0 / d))

    inv_a = inv_freq(RET_HEAD_DIM).reshape(1, LANES)
    inv_b = jnp.tile(inv_freq(DSA_HEAD_DIM), 2).reshape(1, LANES)
    sign = jnp.concatenate([-jnp.ones((LANES // 2,), jnp.float32), jnp.ones((LANES // 2,), jnp.float32)]).reshape(1, LANES)
    row = pl.BlockSpec((1, LANES), lambda i: (0, 0))
    tab = pl.BlockSpec((tr, LANES), lambda i: (i, 0))
    sds = jax.ShapeDtypeStruct((m, LANES), jnp.float32)
    return pl.pallas_call(
        _rope_tables_kernel,
        grid=(m // tr,),
        in_specs=[pl.BlockSpec((tr, 1), lambda i: (i, 0)), row, row, row],
        out_specs=[tab, tab, tab, tab],
        out_shape=[sds, sds, sds, sds],
        compiler_params=_cparams("parallel"),
        name="rope_tables",
    )(positions.reshape(m, 1), inv_a, inv_b, sign)


def _retention_kernel(q_ref, k_ref, v_ref, g_ref, cos_ref, sin_ref, decay_ref, xi_ref, zeta_ref, cd_ref,
                      gain_ref, o_ref, state_ref, *, n_chunk):
    @pl.when(pl.program_id(2) == 0)
    def _():
        state_ref[...] = jnp.zeros_like(state_ref)

    half = RET_HEAD_DIM // 2
    decay = decay_ref[...]
    xi = xi_ref[...]
    zeta = zeta_ref[...]
    cd = cd_ref[0:1, :]
    gain = gain_ref[...]
    k_scale = RET_HEAD_DIM ** -0.5

    def rot(x, cos, sin):
        x1, x2 = x[:, :half], x[:, half:]
        return x1 * cos - x2 * sin, x2 * cos + x1 * sin

    for c in range(n_chunk):
        rows = slice(c * RET_CHUNK, (c + 1) * RET_CHUNK)
        cos, sin = cos_ref[rows, :], sin_ref[rows, :]
        q1, q2 = rot(q_ref[rows, :].astype(jnp.float32), cos, sin)
        k1, k2 = rot(k_ref[rows, :].astype(jnp.float32), cos, sin)
        k1, k2 = k1 * k_scale, k2 * k_scale
        v = v_ref[rows, :]
        qb = jnp.concatenate([q1, q2], axis=1).astype(jnp.bfloat16)
        kb = jnp.concatenate([k1, k2], axis=1).astype(jnp.bfloat16)
        sc = lax.dot_general(qb, kb, _NT, preferred_element_type=jnp.float32) * decay
        inner = jnp.dot(sc.astype(jnp.bfloat16), v, preferred_element_type=jnp.float32)
        state = state_ref[...]
        qx = jnp.concatenate([q1 * xi, q2 * xi], axis=1).astype(jnp.bfloat16)
        cross = jnp.dot(qx, state.astype(jnp.bfloat16), preferred_element_type=jnp.float32)
        out = inner + cross
        kz_t = jnp.concatenate([(k1 * zeta).T, (k2 * zeta).T], axis=0).astype(jnp.bfloat16)
        state_ref[...] = (state * jnp.concatenate([cd, cd], axis=1)
                          + jnp.dot(kz_t, v, preferred_element_type=jnp.float32))
        mu = jnp.mean(out, axis=-1, keepdims=True)
        oc = out - mu
        y = oc * lax.rsqrt(jnp.mean(oc * oc, axis=-1, keepdims=True) + EPS) * gain
        g = g_ref[rows, :].astype(jnp.float32)
        o_ref[rows, :] = (g * jax.nn.sigmoid(g) * y).astype(o_ref.dtype)


def _retention(h, cos_a, sin_a, ret_gain, batch, seq):
    m = batch * seq
    hd = RET_HEAD_DIM
    rb = min(4096, seq)
    nrb = seq // rb
    c = RET_CHUNK
    log_g = jnp.log(1.0 - jnp.power(2.0, -5.0 - jnp.arange(RET_HEADS, dtype=jnp.float32)))
    n = jnp.arange(c, dtype=jnp.float32)
    diff = n[:, None] - n[None, :]
    decay = jnp.where(diff >= 0, jnp.exp(log_g[:, None, None] * jnp.maximum(diff, 0.0)), 0.0)
    xi = jnp.broadcast_to(jnp.exp(log_g[:, None] * (n + 1.0))[:, :, None], (RET_HEADS, c, LANES))
    zeta = jnp.broadcast_to(jnp.exp(log_g[:, None] * (c - 1.0 - n))[:, :, None], (RET_HEADS, c, LANES))
    cd = jnp.broadcast_to(jnp.exp(log_g * c)[:, None, None], (RET_HEADS, 8, LANES))

    def seg(name):
        assert SEG[name] % hd == 0
        base = SEG[name] // hd
        return pl.BlockSpec((rb, hd), lambda b, hh, r: (b * nrb + r, base + hh))

    tab = pl.BlockSpec((rb, LANES), lambda b, hh, r: (b * nrb + r, 0))
    per_head = lambda rows: pl.BlockSpec((None, rows, LANES), lambda b, hh, r: (hh, 0, 0))
    return pl.pallas_call(
        functools.partial(_retention_kernel, n_chunk=rb // c),
        grid=(batch, RET_HEADS, nrb),
        in_specs=[seg("rq"), seg("rk"), seg("rv"), seg("rg"), tab, tab,
                  per_head(c), per_head(c), per_head(c), per_head(8),
                  pl.BlockSpec((1, hd), lambda b, hh, r: (0, hh))],
        out_specs=pl.BlockSpec((rb, hd), lambda b, hh, r: (b * nrb + r, hh)),
        out_shape=jax.ShapeDtypeStruct((m, BRANCH_W), jnp.bfloat16),
        scratch_shapes=[pltpu.VMEM((hd, hd), jnp.float32)],
        compiler_params=_cparams("parallel", "parallel", "arbitrary"),
        name="retention",
    )(h, h, h, h, cos_a, sin_a, decay, xi, zeta, cd, ret_gain.reshape(1, BRANCH_W))


def _rope_half(x, cos, sin):
    return x * cos + pltpu.roll(x, DSA_HEAD_DIM // 2, 1) * sin


def _rms_head(x, g):
    return x * lax.rsqrt(jnp.mean(x * x, axis=-1, keepdims=True) + EPS) * g


def _dsa_prep_kernel(dk_ref, dv_ref, ik_ref, cos_ref, sin_ref, kgain_ref, k_ref, vt_ref, kidx_ref, kn_ref):
    cos, sin = cos_ref[...], sin_ref[...]
    d = DSA_HEAD_DIM
    kgain = kgain_ref[...]
    for kvh in range(DSA_KV_HEADS):
        x = dk_ref[:, kvh * d:(kvh + 1) * d].astype(jnp.float32)
        kb = _rope_half(_rms_head(x, kgain), cos, sin).astype(k_ref.dtype)
        k_ref[kvh] = kb
        kf = kb.astype(jnp.float32)
        kn2 = jnp.max(jnp.sum(kf * kf, axis=-1, keepdims=True), axis=0, keepdims=True)
        kn_ref[kvh:kvh + 1, :] = jnp.broadcast_to(kn2, (1, LANES))
        vt_ref[kvh] = dv_ref[:, kvh * d:(kvh + 1) * d].astype(jnp.float32).T.astype(vt_ref.dtype)
    kidx_ref[...] = _rope_half(ik_ref[...].astype(jnp.float32), cos, sin).astype(kidx_ref.dtype)


def _h_block(name, width, rows, row_index, part=0):
    assert SEG[name] % width == 0
    base = SEG[name] // width + part
    return pl.BlockSpec((rows, width), lambda b, j: (row_index(b, j), base))


def _dsa_prep(h, cos_b, sin_b, k_gain, batch, seq):
    nq = seq // Q_BLOCK
    d = DSA_HEAD_DIM
    row = lambda b, j: b * nq + j
    tab = pl.BlockSpec((Q_BLOCK, LANES), lambda b, j: (row(b, j), 0))
    bf = jnp.bfloat16
    return pl.pallas_call(
        _dsa_prep_kernel,
        grid=(batch, nq),
        in_specs=[_h_block("dk", DSA_KV_W, Q_BLOCK, row), _h_block("dv", DSA_KV_W, Q_BLOCK, row),
                  _h_block("ik", IDX_DIM, Q_BLOCK, row), tab, tab, pl.BlockSpec((1, d), lambda b, j: (0, 0))],
        out_specs=[
            pl.BlockSpec((None, DSA_KV_HEADS, Q_BLOCK, d), lambda b, j: (b, 0, j, 0)),
            pl.BlockSpec((None, DSA_KV_HEADS, None, d, Q_BLOCK), lambda b, j: (b, 0, j, 0, 0)),
            pl.BlockSpec((None, Q_BLOCK, IDX_DIM), lambda b, j: (b, j, 0)),
            pl.BlockSpec((None, None, DSA_KV_HEADS, LANES), lambda b, j: (b, j, 0, 0)),
        ],
        out_shape=[
            jax.ShapeDtypeStruct((batch, DSA_KV_HEADS, seq, d), bf),
            jax.ShapeDtypeStruct((batch, DSA_KV_HEADS, nq, d, Q_BLOCK), bf),
            jax.ShapeDtypeStruct((batch, seq, IDX_DIM), bf),
            jax.ShapeDtypeStruct((batch, nq, DSA_KV_HEADS, LANES), jnp.float32),
        ],
        compiler_params=_cparams("parallel", "parallel"),
        name="dsa_prep",
    )(h, h, h, cos_b, sin_b, k_gain.reshape(1, d))


def _dsa_select_kernel(iq0_ref, iq1_ref, iq2_ref, iq3_ref, iw_ref, cos_ref, sin_ref, kidx_ref, mask_ref,
                       hi_ref, lo_ref, qi_ref, w_ref, *, topk, nq, per, nqb):
    jp = pl.program_id(1)
    c_ = Q_BLOCK
    tk = per * c_
    blocks = range(nqb)
    iq_refs = (iq0_ref, iq1_ref, iq2_ref, iq3_ref)
    for qb in blocks:
        tok = slice(qb * c_, (qb + 1) * c_)
        cos, sin = cos_ref[tok, :], sin_ref[tok, :]
        w = iw_ref[tok, :].astype(jnp.float32) * (IDX_DIM ** -0.5 * IDX_HEADS ** -0.5)
        for head in range(IDX_HEADS):
            part, col = divmod(head * IDX_DIM, IQ_BLOCK)
            x = iq_refs[part][tok, col:col + IDX_DIM].astype(jnp.float32)
            qi_ref[qb, head * c_:(head + 1) * c_, :] = _rope_half(x, cos, sin).astype(qi_ref.dtype)
            w_ref[qb, head * c_:(head + 1) * c_, :] = jnp.broadcast_to(w[:, head:head + 1], (c_, LANES))
    sub = lax.broadcasted_iota(jnp.int32, (c_, c_), 0)
    lane = lax.broadcasted_iota(jnp.int32, (c_, c_), 1)
    int_min = jnp.int32(-2 ** 31)
    i16_min = jnp.int32(-2 ** 15)
    n_tiles = (jp * nqb + nqb - 1 + per) // per

    def causal(qb, chunk):
        return (chunk * c_ + sub) <= ((jp * nqb + qb) * c_ + lane)

    def chunk_rows(t, u):
        return pl.ds(pl.multiple_of((t * per + u) * c_, c_), c_)

    def store_keys(qb, chunk, key):
        rows = pl.ds(pl.multiple_of(chunk * c_, c_), c_)
        hi_ref[qb, rows, :] = (key >> 16).astype(jnp.int16)
        lo_ref[qb, rows, :] = ((key & 0xFFFF) - 32768).astype(jnp.int16)

    def score_tile(t, carry):
        for u2 in range(per // 2):
            first = t * per + 2 * u2
            kt = kidx_ref[pl.ds(pl.multiple_of(first * c_, 2 * c_), 2 * c_), :]
            for qb in blocks:
                s = lax.dot_general(qi_ref[qb], kt, _NT, preferred_element_type=jnp.float32)
                acc = [jnp.zeros((c_, c_), jnp.float32), jnp.zeros((c_, c_), jnp.float32)]
                for head in range(IDX_HEADS):
                    rows = slice(head * c_, (head + 1) * c_)
                    w = w_ref[qb, rows, :]
                    for half in range(2):
                        acc[half] = acc[half] + jnp.maximum(s[rows, half * c_:(half + 1) * c_], 0.0) * w
                for half in range(2):
                    bits = lax.bitcast_convert_type(acc[half].T, jnp.int32)
                    key = bits ^ ((bits >> 31) & jnp.int32(0x7FFFFFFF))
                    store_keys(qb, first + half, jnp.where(causal(qb, first + half), key, int_min))
        return carry

    lax.fori_loop(0, n_tiles, score_tile, 0)

    def count16(hits):
        def body(t, accs):
            accs = list(accs)
            for u in range(per):
                rows = chunk_rows(t, u)
                for qb in blocks:
                    hit = jnp.where(hits(qb, rows), jnp.int16(1), jnp.int16(0)).reshape(c_ // 16, 16, c_)
                    accs[qb * per + u] = accs[qb * per + u] + functools.reduce(
                        lambda a, b: a + b, [hit[i] for i in range(c_ // 16)])
            return tuple(accs)
        accs = lax.fori_loop(0, n_tiles, body, (jnp.zeros((16, c_), jnp.int16),) * (per * nqb))
        out = []
        for qb in blocks:
            total = functools.reduce(lambda a, b: a + b, [a.astype(jnp.int32) for a in accs[qb * per:(qb + 1) * per]])
            out.append(jnp.sum(total.astype(jnp.float32), axis=0, keepdims=True))
        return out

    def kth_largest16(ref, ks):
        zero = jnp.zeros((1, c_), jnp.int32)

        def enough(cands):
            c16 = [c.astype(jnp.int16) for c in cands]
            counts = count16(lambda qb, rows: ref[qb, rows, :] >= c16[qb])
            return [n >= k for n, k in zip(counts, ks)]

        vs = tuple(jnp.where(ok, zero, i16_min) for ok in enough([zero] * nqb))

        def bit_step(i, vs):
            cands = [v | jnp.left_shift(jnp.int32(1), 14 - i) for v in vs]
            return tuple(jnp.where(ok, c, v) for ok, c, v in zip(enough(cands), cands, vs))

        return lax.fori_loop(0, 15, bit_step, vs)

    kf = jnp.float32(topk)
    thr_hi = kth_largest16(hi_ref, [kf] * nqb)
    th = [v.astype(jnp.int16) for v in thr_hi]
    k_lo = [kf - n for n in count16(lambda qb, rows: hi_ref[qb, rows, :] > th[qb])]

    def restrict_lo(t, carry):
        for u in range(per):
            rows = chunk_rows(t, u)
            for qb in blocks:
                lo_ref[qb, rows, :] = jnp.where(hi_ref[qb, rows, :] == th[qb], lo_ref[qb, rows, :],
                                                jnp.int16(-2 ** 15))
        return carry

    lax.fori_loop(0, n_tiles, restrict_lo, 0)
    thr_lo = kth_largest16(lo_ref, k_lo)
    thr_lo = [jnp.where((h == i16_min) & (l == i16_min), i16_min + 1, l) for h, l in zip(thr_hi, thr_lo)]
    tl = [v.astype(jnp.int16) for v in thr_lo]

    def cmp_key(qb, rows, lo_test):
        hi = hi_ref[qb, rows, :]
        return (hi > th[qb]) | ((hi == th[qb]) & lo_test(lo_ref[qb, rows, :], tl[qb]))

    n_ge = count16(lambda qb, rows: cmp_key(qb, rows, lambda lo, t: lo >= t))
    n_gt = count16(lambda qb, rows: cmp_key(qb, rows, lambda lo, t: lo > t))
    room = [kf - n for n in n_gt]
    left_out = functools.reduce(jnp.maximum, [jnp.where(n > kf, 1.0, 0.0) for n in n_ge])
    some_tie_left_out = jnp.max(left_out) > 0.0

    @pl.when(jnp.logical_not(some_tie_left_out))
    def _():
        def mask_tile(t, carry):
            for u in range(per):
                rows = chunk_rows(t, u)
                for qb in blocks:
                    take = cmp_key(qb, rows, lambda lo, t: lo >= t)
                    mask_ref[qb, rows, :] = jnp.where(take, jnp.ones((), mask_ref.dtype), jnp.zeros((), mask_ref.dtype))
            return carry
        lax.fori_loop(0, n_tiles, mask_tile, 0)

    @pl.when(some_tie_left_out)
    def _():
        tri = jnp.where(sub > lane, 1.0, 0.0).astype(jnp.bfloat16)
        thr = [jnp.left_shift(h, 16) | ((l + 32768) & 0xFFFF) for h, l in zip(thr_hi, thr_lo)]

        def mask_tile(t, seen):
            seen = list(seen)
            for u in range(per):
                rows = chunk_rows(t, u)
                for qb in blocks:
                    kc = (jnp.left_shift(hi_ref[qb, rows, :].astype(jnp.int32), 16)
                          | ((lo_ref[qb, rows, :].astype(jnp.int32) + 32768) & 0xFFFF))
                    eq = jnp.where(kc == thr[qb], 1.0, 0.0)
                    before = jnp.dot(tri, eq.astype(jnp.bfloat16), preferred_element_type=jnp.float32) + seen[qb]
                    take = jnp.where(kc > thr[qb], 1.0, jnp.where(before < room[qb], eq, 0.0))
                    mask_ref[qb, rows, :] = take.astype(mask_ref.dtype)
                    seen[qb] = seen[qb] + jnp.sum(eq, axis=0, keepdims=True)
            return tuple(seen)
        lax.fori_loop(0, n_tiles, mask_tile, (jnp.zeros((1, c_), jnp.float32),) * nqb)

    def zero_tile(t, carry):
        for qb in blocks:
            mask_ref[qb, pl.ds(pl.multiple_of(t * tk, tk), tk), :] = jnp.zeros((tk, c_), mask_ref.dtype)
        return carry

    lax.fori_loop(n_tiles, nq // per, zero_tile, 0)


def _dsa_select(h, cos_b, sin_b, kidx, topk):
    batch, seq = kidx.shape[0], kidx.shape[1]
    nq = seq // Q_BLOCK
    per = min(4, nq)
    nqb = 2
    assert nq % per == 0 and per % 2 == 0 and nq % nqb == 0 and seq < 2 ** 15
    rows = nqb * Q_BLOCK
    row = lambda b, j: b * (nq // nqb) + j
    n_iq = IDX_HEADS * IDX_DIM // IQ_BLOCK
    assert n_iq == 4
    tab = pl.BlockSpec((rows, LANES), lambda b, j: (row(b, j), 0))
    return pl.pallas_call(
        functools.partial(_dsa_select_kernel, topk=topk, nq=nq, per=per, nqb=nqb),
        grid=(batch, nq // nqb),
        in_specs=[_h_block("iq", IQ_BLOCK, rows, row, part) for part in range(n_iq)]
                 + [_h_block("iw", LANES, rows, row), tab, tab,
                    pl.BlockSpec((None, seq, IDX_DIM), lambda b, j: (b, 0, 0), pipeline_mode=pl.Buffered(1))],
        out_specs=pl.BlockSpec((None, nqb, seq, Q_BLOCK), lambda b, j: (b, j, 0, 0)),
        out_shape=jax.ShapeDtypeStruct((batch, nq, seq, Q_BLOCK), jnp.bfloat16),
        scratch_shapes=[pltpu.VMEM((nqb, seq, Q_BLOCK), jnp.int16),
                        pltpu.VMEM((nqb, seq, Q_BLOCK), jnp.int16),
                        pltpu.VMEM((nqb, IDX_HEADS * Q_BLOCK, IDX_DIM), jnp.bfloat16),
                        pltpu.VMEM((nqb, IDX_HEADS * Q_BLOCK, LANES), jnp.float32)],
        compiler_params=_cparams("parallel", "arbitrary"),
        name="dsa_select",
    )(*([h] * (n_iq + 1)), cos_b, sin_b, kidx)


def _dsa_attn_kernel(dq_ref, cos_ref, sin_ref, qgain_ref, k_ref, vt_ref, kn_ref, mask_ref, dg_ref, o_ref,
                     qg_ref, m_ref, acc_ref, s_ref, p_ref, alpha_ref, *, tk, n_key_tiles):
    j = pl.program_id(1)
    per = tk // Q_BLOCK
    d = DSA_HEAD_DIM
    gw = DSA_GROUP * Q_BLOCK
    heads = range(DSA_KV_HEADS)
    n_live = (j + per) // per
    n_pairs = (n_live + 1) // 2
    neg_inf = jnp.float32(-jnp.inf)
    q_scale = DSA_HEAD_DIM ** -0.5 * np.log2(np.e).astype(np.float32)
    cos, sin, qgain = cos_ref[...], sin_ref[...], qgain_ref[...]
    for kv in heads:
        for g in range(DSA_GROUP):
            hh = kv * DSA_GROUP + g
            x = dq_ref[:, hh * d:(hh + 1) * d].astype(jnp.float32)
            qg_ref[kv, g * Q_BLOCK:(g + 1) * Q_BLOCK, :] = (
                _rope_half(_rms_head(x, qgain), cos, sin) * q_scale).astype(qg_ref.dtype)
    acc_ref[...] = jnp.zeros_like(acc_ref)
    p_ref[:, 1] = jnp.zeros((DSA_KV_HEADS,) + p_ref.shape[2:], p_ref.dtype)
    ones_rows = jnp.ones((acc_ref.shape[1] - d, tk), jnp.bfloat16)

    k_max2 = jnp.max(kn_ref[...], axis=0)
    ones8 = jnp.ones((8, d), jnp.bfloat16)
    bound = []
    for kv in heads:
        qf = qg_ref[kv].astype(jnp.float32)
        q_norm2 = lax.dot_general(ones8, (qf * qf).astype(jnp.bfloat16), _NT,
                                  preferred_element_type=jnp.float32)[0:1, :]
        bound.append(jnp.sqrt(q_norm2 * jnp.concatenate([k_max2[kv:kv + 1, :]] * DSA_GROUP, axis=1)) * 1.03 + 1e-3)
    bounded_is_safe = jnp.max(functools.reduce(jnp.maximum, bound)) <= 50.0

    def logits(kv, t):
        r0 = pl.multiple_of(t * tk, tk)
        return lax.dot_general(k_ref[kv, pl.ds(r0, tk), :], qg_ref[kv], _NT, preferred_element_type=jnp.float32)

    def softmax_running_max(kv, sel, slot):
        s = s_ref[kv, slot]
        s = jnp.concatenate([jnp.where(sel, s[:, g * Q_BLOCK:(g + 1) * Q_BLOCK], neg_inf)
                             for g in range(DSA_GROUP)], axis=1)
        m_old = m_ref[kv]
        m_new = jnp.maximum(m_old, jnp.max(s, axis=0, keepdims=True))
        m_safe = jnp.where(m_new == neg_inf, 0.0, m_new)
        p_ref[kv, slot] = jnp.exp2(s - m_safe).astype(p_ref.dtype)
        alpha_ref[kv, slot] = jnp.exp2(m_old - m_safe)
        m_ref[kv] = m_new

    def softmax_bounded(kv, sel, slot):
        s = s_ref[kv, slot]
        parts = []
        for g in range(DSA_GROUP):
            cols = slice(g * Q_BLOCK, (g + 1) * Q_BLOCK)
            shift = jnp.where(sel, -bound[kv][:, cols], neg_inf)
            parts.append(jnp.exp2(s[:, cols] + shift))
        p_ref[kv, slot] = jnp.concatenate(parts, axis=1).astype(p_ref.dtype)

    def pv_stage(kv, t, slot, rescale):
        vt = jnp.concatenate([vt_ref[kv, t * per + i] for i in range(per)], axis=1)
        lhs = jnp.concatenate([vt, ones_rows], axis=0)
        pv = jnp.dot(lhs, p_ref[kv, slot], preferred_element_type=jnp.float32)
        acc_ref[kv] = (acc_ref[kv] * alpha_ref[kv, slot] if rescale else acc_ref[kv]) + pv

    def run(softmax_stage, rescale):
        for kv in heads:
            s_ref[kv, 0] = logits(kv, 0)

        def pair(i, carry):
            for u in range(2):
                t = 2 * i + u
                r0 = pl.multiple_of(t * tk, tk)
                sel = mask_ref[pl.ds(r0, tk), :].astype(jnp.float32) > 0.5
                for kv in heads:
                    s_ref[kv, 1 - u] = logits(kv, jnp.minimum(t + 1, n_key_tiles - 1))
                    softmax_stage(kv, sel, u)
                    pv_stage(kv, jnp.maximum(t - 1, 0), 1 - u, rescale)
            return carry

        lax.fori_loop(0, n_pairs, pair, 0)
        for kv in heads:
            pv_stage(kv, 2 * n_pairs - 1, 1, rescale)

    @pl.when(bounded_is_safe)
    def _():
        run(softmax_bounded, rescale=False)

    @pl.when(jnp.logical_not(bounded_is_safe))
    def _():
        m_ref[...] = jnp.full_like(m_ref, neg_inf)
        alpha_ref[:, 1] = jnp.ones((DSA_KV_HEADS,) + alpha_ref.shape[2:], alpha_ref.dtype)
        run(softmax_running_max, rescale=True)

    for kv in heads:
        out_t = acc_ref[kv, :d, :] / acc_ref[kv, d:d + 1, :]
        for g in range(DSA_GROUP):
            cols = slice(kv * gw + g * Q_BLOCK, kv * gw + (g + 1) * Q_BLOCK)
            gate = dg_ref[:, cols].astype(jnp.float32)
            o_ref[:, cols] = (gate * jax.nn.sigmoid(gate)
                              * out_t[:, g * Q_BLOCK:(g + 1) * Q_BLOCK].T).astype(o_ref.dtype)


def _dsa_attention(k, vt, kn, mask, h, cos_b, sin_b, q_gain, batch, seq):
    nq = seq // Q_BLOCK
    d = DSA_HEAD_DIM
    gw = DSA_GROUP * d
    tk = min(256, seq // 2)
    assert seq % (2 * tk) == 0
    assert SEG["dg"] % BRANCH_W == 0
    kvh = DSA_KV_HEADS
    once = pl.Buffered(1)
    row = lambda b, j: b * nq + j
    tab = pl.BlockSpec((Q_BLOCK, LANES), lambda b, j: (row(b, j), 0))
    return pl.pallas_call(
        functools.partial(_dsa_attn_kernel, tk=tk, n_key_tiles=seq // tk),
        grid=(batch, nq),
        in_specs=[_h_block("dq", BRANCH_W, Q_BLOCK, row), tab, tab, pl.BlockSpec((1, d), lambda b, j: (0, 0)),
                  pl.BlockSpec((None, kvh, seq, d), lambda b, j: (b, 0, 0, 0), pipeline_mode=once),
                  pl.BlockSpec((None, kvh, nq, d, Q_BLOCK), lambda b, j: (b, 0, 0, 0, 0), pipeline_mode=once),
                  pl.BlockSpec((None, nq, kvh, LANES), lambda b, j: (b, 0, 0, 0), pipeline_mode=once),
                  pl.BlockSpec((None, None, seq, Q_BLOCK), lambda b, j: (b, j, 0, 0)),
                  pl.BlockSpec((Q_BLOCK, BRANCH_W), lambda b, j: (b * nq + j, SEG["dg"] // BRANCH_W))],
        out_specs=pl.BlockSpec((Q_BLOCK, BRANCH_W), lambda b, j: (b * nq + j, 0)),
        out_shape=jax.ShapeDtypeStruct((batch * seq, BRANCH_W), jnp.bfloat16),
        scratch_shapes=[pltpu.VMEM((kvh, gw, d), jnp.bfloat16),
                        pltpu.VMEM((kvh, 1, gw), jnp.float32),
                        pltpu.VMEM((kvh, d + 16, gw), jnp.float32),
                        pltpu.VMEM((kvh, 2, tk, gw), jnp.float32),
                        pltpu.VMEM((kvh, 2, tk, gw), jnp.bfloat16),
                        pltpu.VMEM((kvh, 2, 1, gw), jnp.float32)],
        compiler_params=_cparams("parallel", "arbitrary"),
        name="dsa_attn",
    )(h, cos_b, sin_b, q_gain.reshape(1, d), k, vt, kn, mask, h)


def _gelu_tanh(x):
    return 0.5 * x * (1.0 + jnp.tanh(np.sqrt(2.0 / np.pi).astype(np.float32) * (x + 0.044715 * (x * x * x))))


def _gmlp_kernel(u_ref, v_ref, g_ref, gain_ref, w_ref, b_ref, o_ref):
    c_ = GM_CHUNK
    v = _gelu_tanh(v_ref[...].astype(jnp.float32))
    mu = jnp.mean(v, axis=-1, keepdims=True)
    vc = v - mu
    vn = (vc * lax.rsqrt(jnp.mean(vc * vc, axis=-1, keepdims=True) + EPS) * gain_ref[...]).astype(jnp.bfloat16)
    sub = lax.broadcasted_iota(jnp.int32, (c_, c_), 0)
    lane = lax.broadcasted_iota(jnp.int32, (c_, c_), 1)
    tril = sub >= lane
    for g in range(GM_GROUPS):
        cols = slice(g * GM_GROUP_DIM, (g + 1) * GM_GROUP_DIM)
        w = jnp.where(tril, w_ref[g], jnp.zeros((), w_ref.dtype))
        mixed = jnp.dot(w, vn[:, cols], preferred_element_type=jnp.float32) + b_ref[g]
        u = _gelu_tanh(u_ref[:, cols].astype(jnp.float32))
        gate = g_ref[:, cols].astype(jnp.float32)
        o_ref[:, cols] = (gate * jax.nn.sigmoid(gate) * (u * mixed)).astype(o_ref.dtype)


def _gmlp(h, gm_gain, w_spatial, b_spatial, m):
    c_ = GM_CHUNK
    w = BRANCH_W

    def seg(name):
        assert SEG[name] % w == 0
        base = SEG[name] // w
        return pl.BlockSpec((c_, w), lambda i: (i, base))

    b_b = jnp.broadcast_to(b_spatial[:, :, None], (GM_GROUPS, c_, GM_GROUP_DIM))
    whole = lambda shape: pl.BlockSpec(shape, lambda i: (0,) * len(shape))
    return pl.pallas_call(
        _gmlp_kernel,
        grid=(m // c_,),
        in_specs=[seg("gu"), seg("gv"), seg("gg"), whole((1, w)), whole((GM_GROUPS, c_, c_)),
                  whole((GM_GROUPS, c_, GM_GROUP_DIM))],
        out_specs=pl.BlockSpec((c_, w), lambda i: (i, 0)),
        out_shape=jax.ShapeDtypeStruct((m, w), jnp.bfloat16),
        compiler_params=_cparams("parallel"),
        name="gmlp",
    )(h, h, h, gm_gain.reshape(1, w), w_spatial.astype(jnp.bfloat16), b_b)


def _merge_kernel(y0_ref, y1_ref, y2_ref, w_ref, g0_ref, g1_ref, g2_ref, o_ref):
    acc = None
    for b, (y_ref, g_ref) in enumerate(((y0_ref, g0_ref), (y1_ref, g1_ref), (y2_ref, g2_ref))):
        proj = jnp.dot(y_ref[...], w_ref[b], preferred_element_type=jnp.float32)
        term = jax.nn.sigmoid(g_ref[...].astype(jnp.float32)) * proj
        acc = term if acc is None else acc + term
    o_ref[...] = acc.astype(o_ref.dtype)


def _merge(ys, w_branch, h):
    m = h.shape[0]
    tm, tn = min(1024, m), 512
    ybs = pl.BlockSpec((tm, BRANCH_W), lambda i, j: (i, 0))

    def gate(b):
        base = (SEG["mg"] + b * D_MODEL) // tn
        return pl.BlockSpec((tm, tn), lambda i, j: (i, base + j))

    return pl.pallas_call(
        _merge_kernel,
        grid=(m // tm, D_MODEL // tn),
        in_specs=[ybs, ybs, ybs, pl.BlockSpec((N_BRANCH, BRANCH_W, tn), lambda i, j: (0, 0, j)),
                  gate(0), gate(1), gate(2)],
        out_specs=pl.BlockSpec((tm, tn), lambda i, j: (i, j)),
        out_shape=jax.ShapeDtypeStruct((m, D_MODEL), jnp.bfloat16),
        compiler_params=_cparams("parallel", "arbitrary"),
        name="merge",
    )(*ys, w_branch, h, h, h)


def _out_proj_kernel(a_ref, w_ref, x_ref, o_ref):
    o_ref[...] = x_ref[...] + jnp.dot(a_ref[...], w_ref[...], preferred_element_type=jnp.float32)


def _out_proj(merged, w_out, x2):
    m, d = x2.shape
    tm, tn = min(1024, m), 1024
    return pl.pallas_call(
        _out_proj_kernel,
        grid=(m // tm, d // tn),
        in_specs=[pl.BlockSpec((tm, d), lambda i, j: (i, 0)), pl.BlockSpec((d, tn), lambda i, j: (0, j)),
                  pl.BlockSpec((tm, tn), lambda i, j: (i, j))],
        out_specs=pl.BlockSpec((tm, tn), lambda i, j: (i, j)),
        out_shape=jax.ShapeDtypeStruct((m, d), jnp.float32),
        compiler_params=_cparams("parallel", "arbitrary"),
        name="out_proj",
    )(merged, w_out, x2)


def _w_in_runs():
    runs, o = [], 0
    for name in _MY_ORDER:
        off, width = _REF_OFF[name]
        assert SEG[name] == o
        if runs and runs[-1][1] is not None and runs[-1][1] + runs[-1][2] == off and runs[-1][0] + runs[-1][2] == o:
            runs[-1][2] += width
        else:
            runs.append([o, off, width])
        padded = -(-width // LANES) * LANES
        if padded != width:
            runs.append([o + width, None, padded - width])
        o += padded
    if N_IN_PAD != o:
        runs.append([o, None, N_IN_PAD - o])
    return runs


RELAYOUT_COLS = 256


def _relayout_plan():
    src_of = np.full((N_IN_PAD,), -1, np.int64)
    for dst, src, width in _w_in_runs():
        if src is not None:
            src_of[dst:dst + width] = np.arange(src, src + width)
    first, valid = [], []
    for c in range(N_IN_PAD // RELAYOUT_COLS):
        cols = src_of[c * RELAYOUT_COLS:(c + 1) * RELAYOUT_COLS]
        n = int((cols >= 0).sum())
        assert n > 0 and (cols[:n] == cols[0] + np.arange(n)).all() and (cols[n:] < 0).all()
        first.append(int(cols[0]))
        valid.append(n)
    return np.asarray(first, np.int32), np.asarray(valid, np.int32)


def _relayout_kernel(first_ref, valid_ref, wt_ref, o_ref):
    x = wt_ref[0].T
    lane = lax.broadcasted_iota(jnp.int32, x.shape, 1)
    o_ref[...] = jnp.where(lane < valid_ref[pl.program_id(0)], x, 0.0).astype(o_ref.dtype)


def _relayout_w_in(w_in, layer):
    _, d, n = w_in.shape
    first, valid = _relayout_plan()
    assert N_IN_PAD % RELAYOUT_COLS == 0 and int((first + RELAYOUT_COLS).max()) <= n
    wt = jnp.swapaxes(w_in, 1, 2)
    return pl.pallas_call(
        _relayout_kernel,
        grid_spec=pltpu.PrefetchScalarGridSpec(
            num_scalar_prefetch=2,
            grid=(N_IN_PAD // RELAYOUT_COLS,),
            in_specs=[pl.BlockSpec((pl.Element(1), pl.Element(RELAYOUT_COLS), pl.Element(d)),
                                   lambda c, first, valid: (layer, pl.multiple_of(first[c], 8), 0))],
            out_specs=pl.BlockSpec((d, RELAYOUT_COLS), lambda c, first, valid: (0, c)),
        ),
        out_shape=jax.ShapeDtypeStruct((d, N_IN_PAD), jnp.bfloat16),
        compiler_params=_cparams("parallel"),
        name="relayout_w_in",
    )(jnp.asarray(first), jnp.asarray(valid), wt)


def kernel(x, positions, norm_gain, w_in, ret_norm_gain, q_norm_gain, k_norm_gain, gm_norm_gain, w_spatial,
           b_spatial, w_branch, w_out):
    batch, seq, d = x.shape
    assert d == D_MODEL and seq % Q_BLOCK == 0
    depth = w_in.shape[0]
    m = batch * seq
    topk = min(TOPK_MAX, seq // 4)
    cos_a, sin_a, cos_b, sin_b = _rope_tables(positions)
    x2 = x.reshape(m, d)
    for l in range(depth):
        xn = _rmsnorm(x2, norm_gain[l])
        h = _in_proj(xn, _relayout_w_in(w_in, l))
        y_ret = _retention(h, cos_a, sin_a, ret_norm_gain[l], batch, seq)
        k, vt, kidx, kn = _dsa_prep(h, cos_b, sin_b, k_norm_gain[l], batch, seq)
        mask = _dsa_select(h, cos_b, sin_b, kidx, topk)
        y_dsa = _dsa_attention(k, vt, kn, mask, h, cos_b, sin_b, q_norm_gain[l], batch, seq)
        y_gm = _gmlp(h, gm_norm_gain[l], w_spatial[l], b_spatial[l], m)
        merged = _merge((y_ret, y_dsa, y_gm), w_branch[l].astype(jnp.bfloat16), h)
        x2 = _out_proj(merged, w_out[l].astype(jnp.bfloat16), x2)
    return x2.reshape(batch, seq, d)
```

```python
import functools

import numpy as np
import jax
import jax.numpy as jnp
from jax import lax
from jax.experimental import pallas as pl
from jax.experimental.pallas import tpu as pltpu

D_MODEL = 4096
BRANCH_W = D_MODEL // 2
N_BRANCH = 3
RET_HEAD_DIM = 256
RET_HEADS = BRANCH_W // RET_HEAD_DIM
RET_CHUNK = 128
DSA_HEAD_DIM = 128
DSA_HEADS = BRANCH_W // DSA_HEAD_DIM
DSA_KV_HEADS = 4
DSA_GROUP = DSA_HEADS // DSA_KV_HEADS
DSA_KV_W = DSA_KV_HEADS * DSA_HEAD_DIM
IDX_HEADS = 32
IDX_DIM = 128
TOPK_MAX = 256
Q_BLOCK = 128
GM_GROUPS = 16
GM_GROUP_DIM = BRANCH_W // GM_GROUPS
GM_CHUNK = 128
ROPE_THETA = 10000.0
EPS = 1e-6

LANES = 128
VMEM_LIMIT = 56 * 1024 * 1024

_REF_SEGS = (("rq", BRANCH_W), ("rk", BRANCH_W), ("rv", BRANCH_W), ("rg", BRANCH_W),
             ("dq", BRANCH_W), ("dk", DSA_KV_W), ("dv", DSA_KV_W),
             ("iq", IDX_HEADS * IDX_DIM), ("ik", IDX_DIM), ("iw", IDX_HEADS),
             ("dg", BRANCH_W), ("gu", BRANCH_W), ("gv", BRANCH_W), ("gg", BRANCH_W),
             ("mg", N_BRANCH * D_MODEL))
_MY_ORDER = ("dg", "gu", "gv", "gg", "mg", "rq", "rk", "rv", "rg", "dq", "dk", "dv", "iq", "ik", "iw")
N_TILE = 768
IQ_BLOCK = 1024


def _layout():
    ref_off, o = {}, 0
    for name, w in _REF_SEGS:
        ref_off[name] = (o, w)
        o += w
    my_off, o = {}, 0
    for name in _MY_ORDER:
        w = ref_off[name][1]
        my_off[name] = o
        o += -(-w // LANES) * LANES
    total = -(-o // N_TILE) * N_TILE
    return ref_off, my_off, total


_REF_OFF, SEG, N_IN_PAD = _layout()

_NT = (((1,), (1,)), ((), ()))


def _cparams(*sem):
    return pltpu.CompilerParams(dimension_semantics=sem, vmem_limit_bytes=VMEM_LIMIT)


def _rmsnorm_kernel(x_ref, g_ref, o_ref):
    x = x_ref[...]
    ms = jnp.mean(x * x, axis=-1, keepdims=True)
    o_ref[...] = (x * lax.rsqrt(ms + EPS) * g_ref[...]).astype(o_ref.dtype)


def _rmsnorm(x2, gain):
    m, d = x2.shape
    tm = min(512, m)
    return pl.pallas_call(
        _rmsnorm_kernel,
        grid=(m // tm,),
        in_specs=[pl.BlockSpec((tm, d), lambda i: (i, 0)), pl.BlockSpec((1, d), lambda i: (0, 0))],
        out_specs=pl.BlockSpec((tm, d), lambda i: (i, 0)),
        out_shape=jax.ShapeDtypeStruct((m, d), jnp.bfloat16),
        compiler_params=_cparams("parallel"),
        name="rmsnorm",
    )(x2, gain.reshape(1, d))


def _matmul_kernel(a_ref, w_ref, o_ref):
    o_ref[...] = jnp.dot(a_ref[...], w_ref[...], preferred_element_type=jnp.float32).astype(o_ref.dtype)


def _in_proj(xn, w):
    m, k = xn.shape
    n = w.shape[1]
    tm, tn = min(1024, m), N_TILE
    return pl.pallas_call(
        _matmul_kernel,
        grid=(m // tm, n // tn),
        in_specs=[pl.BlockSpec((tm, k), lambda i, j: (i, 0)), pl.BlockSpec((k, tn), lambda i, j: (0, j))],
        out_specs=pl.BlockSpec((tm, tn), lambda i, j: (i, j)),
        out_shape=jax.ShapeDtypeStruct((m, n), jnp.bfloat16),
        compiler_params=_cparams("parallel", "arbitrary"),
        name="in_proj",
    )(xn, w)


def _rope_tables_kernel(pos_ref, inv_a_ref, inv_b_ref, sign_ref, ca_ref, sa_ref, cb_ref, sb_ref):
    p = pos_ref[...].astype(jnp.float32)
    ang_a = p * inv_a_ref[...]
    ca_ref[...] = jnp.cos(ang_a)
    sa_ref[...] = jnp.sin(ang_a)
    ang_b = p * inv_b_ref[...]
    cb_ref[...] = jnp.cos(ang_b)
    sb_ref[...] = jnp.sin(ang_b) * sign_ref[...]


def _rope_tables(positions):
    m = positions.size
    tr = min(1024, m)

    def inv_freq(d):
        half = d // 2
        return 1.0 / (ROPE_THETA ** (jnp.arange(half, dtype=jnp.float32) * 2.0 / d))

    inv_a = inv_freq(RET_HEAD_DIM).reshape(1, LANES)
    inv_b = jnp.tile(inv_freq(DSA_HEAD_DIM), 2).reshape(1, LANES)
    sign = jnp.concatenate([-jnp.ones((LANES // 2,), jnp.float32), jnp.ones((LANES // 2,), jnp.float32)]).reshape(1, LANES)
    row = pl.BlockSpec((1, LANES), lambda i: (0, 0))
    tab = pl.BlockSpec((tr, LANES), lambda i: (i, 0))
    sds = jax.ShapeDtypeStruct((m, LANES), jnp.float32)
    return pl.pallas_call(
        _rope_tables_kernel,
        grid=(m // tr,),
        in_specs=[pl.BlockSpec((tr, 1), lambda i: (i, 0)), row, row, row],
        out_specs=[tab, tab, tab, tab],
        out_shape=[sds, sds, sds, sds],
        compiler_params=_cparams("parallel"),
        name="rope_tables",
    )(positions.reshape(m, 1), inv_a, inv_b, sign)


def _retention_kernel(q_ref, k_ref, v_ref, g_ref, cos_ref, sin_ref, decay_ref, xi_ref, zeta_ref, cd_ref,
                      gain_ref, o_ref, state_ref, *, n_chunk):
    @pl.when(pl.program_id(2) == 0)
    def _():
        state_ref[...] = jnp.zeros_like(state_ref)

    half = RET_HEAD_DIM // 2
    decay = decay_ref[...]
    xi = xi_ref[...]
    zeta = zeta_ref[...]
    cd = cd_ref[0:1, :]
    gain = gain_ref[...]
    k_scale = RET_HEAD_DIM ** -0.5

    def rot(x, cos, sin):
        x1, x2 = x[:, :half], x[:, half:]
        return x1 * cos - x2 * sin, x2 * cos + x1 * sin

    for c in range(n_chunk):
        rows = slice(c * RET_CHUNK, (c + 1) * RET_CHUNK)
        cos, sin = cos_ref[rows, :], sin_ref[rows, :]
        q1, q2 = rot(q_ref[rows, :].astype(jnp.float32), cos, sin)
        k1, k2 = rot(k_ref[rows, :].astype(jnp.float32), cos, sin)
        k1, k2 = k1 * k_scale, k2 * k_scale
        v = v_ref[rows, :]
        qb = jnp.concatenate([q1, q2], axis=1).astype(jnp.bfloat16)
        kb = jnp.concatenate([k1, k2], axis=1).astype(jnp.bfloat16)
        sc = lax.dot_general(qb, kb, _NT, preferred_element_type=jnp.float32) * decay
        inner = jnp.dot(sc.astype(jnp.bfloat16), v, preferred_element_type=jnp.float32)
        state = state_ref[...]
        qx = jnp.concatenate([q1 * xi, q2 * xi], axis=1).astype(jnp.bfloat16)
        cross = jnp.dot(qx, state.astype(jnp.bfloat16), preferred_element_type=jnp.float32)
        out = inner + cross
        kz_t = jnp.concatenate([(k1 * zeta).T, (k2 * zeta).T], axis=0).astype(jnp.bfloat16)
        state_ref[...] = (state * jnp.concatenate([cd, cd], axis=1)
                          + jnp.dot(kz_t, v, preferred_element_type=jnp.float32))
        mu = jnp.mean(out, axis=-1, keepdims=True)
        oc = out - mu
        y = oc * lax.rsqrt(jnp.mean(oc * oc, axis=-1, keepdims=True) + EPS) * gain
        g = g_ref[rows, :].astype(jnp.float32)
        o_ref[rows, :] = (g * jax.nn.sigmoid(g) * y).astype(o_ref.dtype)


def _retention(h, cos_a, sin_a, ret_gain, batch, seq):
    m = batch * seq
    hd = RET_HEAD_DIM
    rb = min(4096, seq)
    nrb = seq // rb
    c = RET_CHUNK
    log_g = jnp.log(1.0 - jnp.power(2.0, -5.0 - jnp.arange(RET_HEADS, dtype=jnp.float32)))
    n = jnp.arange(c, dtype=jnp.float32)
    diff = n[:, None] - n[None, :]
    decay = jnp.where(diff >= 0, jnp.exp(log_g[:, None, None] * jnp.maximum(diff, 0.0)), 0.0)
    xi = jnp.broadcast_to(jnp.exp(log_g[:, None] * (n + 1.0))[:, :, None], (RET_HEADS, c, LANES))
    zeta = jnp.broadcast_to(jnp.exp(log_g[:, None] * (c - 1.0 - n))[:, :, None], (RET_HEADS, c, LANES))
    cd = jnp.broadcast_to(jnp.exp(log_g * c)[:, None, None], (RET_HEADS, 8, LANES))

    def seg(name):
        assert SEG[name] % hd == 0
        base = SEG[name] // hd
        return pl.BlockSpec((rb, hd), lambda b, hh, r: (b * nrb + r, base + hh))

    tab = pl.BlockSpec((rb, LANES), lambda b, hh, r: (b * nrb + r, 0))
    per_head = lambda rows: pl.BlockSpec((None, rows, LANES), lambda b, hh, r: (hh, 0, 0))
    return pl.pallas_call(
        functools.partial(_retention_kernel, n_chunk=rb // c),
        grid=(batch, RET_HEADS, nrb),
        in_specs=[seg("rq"), seg("rk"), seg("rv"), seg("rg"), tab, tab,
                  per_head(c), per_head(c), per_head(c), per_head(8),
                  pl.BlockSpec((1, hd), lambda b, hh, r: (0, hh))],
        out_specs=pl.BlockSpec((rb, hd), lambda b, hh, r: (b * nrb + r, hh)),
        out_shape=jax.ShapeDtypeStruct((m, BRANCH_W), jnp.bfloat16),
        scratch_shapes=[pltpu.VMEM((hd, hd), jnp.float32)],
        compiler_params=_cparams("parallel", "parallel", "arbitrary"),
        name="retention",
    )(h, h, h, h, cos_a, sin_a, decay, xi, zeta, cd, ret_gain.reshape(1, BRANCH_W))


def _rope_half(x, cos, sin):
    return x * cos + pltpu.roll(x, DSA_HEAD_DIM // 2, 1) * sin


def _rms_head(x, g):
    return x * lax.rsqrt(jnp.mean(x * x, axis=-1, keepdims=True) + EPS) * g


def _dsa_prep_kernel(dk_ref, dv_ref, ik_ref, cos_ref, sin_ref, kgain_ref, k_ref, vt_ref, kidx_ref, kn_ref):
    cos, sin = cos_ref[...], sin_ref[...]
    d = DSA_HEAD_DIM
    kgain = kgain_ref[...]
    for kvh in range(DSA_KV_HEADS):
        x = dk_ref[:, kvh * d:(kvh + 1) * d].astype(jnp.float32)
        kb = _rope_half(_rms_head(x, kgain), cos, sin).astype(k_ref.dtype)
        k_ref[kvh] = kb
        kf = kb.astype(jnp.float32)
        kn2 = jnp.max(jnp.sum(kf * kf, axis=-1, keepdims=True), axis=0, keepdims=True)
        kn_ref[kvh:kvh + 1, :] = jnp.broadcast_to(kn2, (1, LANES))
        vt_ref[kvh] = dv_ref[:, kvh * d:(kvh + 1) * d].astype(jnp.float32).T.astype(vt_ref.dtype)
    kidx_ref[...] = _rope_half(ik_ref[...].astype(jnp.float32), cos, sin).astype(kidx_ref.dtype)


def _h_block(name, width, rows, row_index, part=0):
    assert SEG[name] % width == 0
    base = SEG[name] // width + part
    return pl.BlockSpec((rows, width), lambda b, j: (row_index(b, j), base))


def _dsa_prep(h, cos_b, sin_b, k_gain, batch, seq):
    nq = seq // Q_BLOCK
    d = DSA_HEAD_DIM
    row = lambda b, j: b * nq + j
    tab = pl.BlockSpec((Q_BLOCK, LANES), lambda b, j: (row(b, j), 0))
    bf = jnp.bfloat16
    return pl.pallas_call(
        _dsa_prep_kernel,
        grid=(batch, nq),
        in_specs=[_h_block("dk", DSA_KV_W, Q_BLOCK, row), _h_block("dv", DSA_KV_W, Q_BLOCK, row),
                  _h_block("ik", IDX_DIM, Q_BLOCK, row), tab, tab, pl.BlockSpec((1, d), lambda b, j: (0, 0))],
        out_specs=[
            pl.BlockSpec((None, DSA_KV_HEADS, Q_BLOCK, d), lambda b, j: (b, 0, j, 0)),
            pl.BlockSpec((None, DSA_KV_HEADS, None, d, Q_BLOCK), lambda b, j: (b, 0, j, 0, 0)),
            pl.BlockSpec((None, Q_BLOCK, IDX_DIM), lambda b, j: (b, j, 0)),
            pl.BlockSpec((None, None, DSA_KV_HEADS, LANES), lambda b, j: (b, j, 0, 0)),
        ],
        out_shape=[
            jax.ShapeDtypeStruct((batch, DSA_KV_HEADS, seq, d), bf),
            jax.ShapeDtypeStruct((batch, DSA_KV_HEADS, nq, d, Q_BLOCK), bf),
            jax.ShapeDtypeStruct((batch, seq, IDX_DIM), bf),
            jax.ShapeDtypeStruct((batch, nq, DSA_KV_HEADS, LANES), jnp.float32),
        ],
        compiler_params=_cparams("parallel", "parallel"),
        name="dsa_prep",
    )(h, h, h, cos_b, sin_b, k_gain.reshape(1, d))


def _dsa_select_kernel(iq0_ref, iq1_ref, iq2_ref, iq3_ref, iw_ref, cos_ref, sin_ref, kidx_ref, mask_ref,
                       hi_ref, lo_ref, qi_ref, w_ref, *, topk, nq, per, nqb):
    jp = pl.program_id(1)
    c_ = Q_BLOCK
    tk = per * c_
    blocks = range(nqb)
    iq_refs = (iq0_ref, iq1_ref, iq2_ref, iq3_ref)
    for qb in blocks:
        tok = slice(qb * c_, (qb + 1) * c_)
        cos, sin = cos_ref[tok, :], sin_ref[tok, :]
        w = iw_ref[tok, :].astype(jnp.float32) * (IDX_DIM ** -0.5 * IDX_HEADS ** -0.5)
        for head in range(IDX_HEADS):
            part, col = divmod(head * IDX_DIM, IQ_BLOCK)
            x = iq_refs[part][tok, col:col + IDX_DIM].astype(jnp.float32)
            qi_ref[qb, head * c_:(head + 1) * c_, :] = _rope_half(x, cos, sin).astype(qi_ref.dtype)
            w_ref[qb, head * c_:(head + 1) * c_, :] = jnp.broadcast_to(w[:, head:head + 1], (c_, LANES))
    sub = lax.broadcasted_iota(jnp.int32, (c_, c_), 0)
    lane = lax.broadcasted_iota(jnp.int32, (c_, c_), 1)
    int_min = jnp.int32(-2 ** 31)
    i16_min = jnp.int32(-2 ** 15)
    n_tiles = (jp * nqb + nqb - 1 + per) // per

    def causal(qb, chunk):
        return (chunk * c_ + sub) <= ((jp * nqb + qb) * c_ + lane)

    def chunk_rows(t, u):
        return pl.ds(pl.multiple_of((t * per + u) * c_, c_), c_)

    def store_keys(qb, chunk, key):
        rows = pl.ds(pl.multiple_of(chunk * c_, c_), c_)
        hi_ref[qb, rows, :] = (key >> 16).astype(jnp.int16)
        lo_ref[qb, rows, :] = ((key & 0xFFFF) - 32768).astype(jnp.int16)

    def score_tile(t, carry):
        for u2 in range(per // 2):
            first = t * per + 2 * u2
            kt = kidx_ref[pl.ds(pl.multiple_of(first * c_, 2 * c_), 2 * c_), :]
            s = lax.dot_general(qi_ref[...].reshape(nqb * IDX_HEADS * c_, IDX_DIM), kt, _NT,
                                preferred_element_type=jnp.float32)
            for qb in blocks:
                acc = [jnp.zeros((c_, c_), jnp.float32), jnp.zeros((c_, c_), jnp.float32)]
                for head in range(IDX_HEADS):
                    rows = slice((qb * IDX_HEADS + head) * c_, (qb * IDX_HEADS + head + 1) * c_)
                    w = w_ref[qb, head * c_:(head + 1) * c_, :]
                    for half in range(2):
                        acc[half] = acc[half] + jnp.maximum(s[rows, half * c_:(half + 1) * c_], 0.0) * w
                for half in range(2):
                    bits = lax.bitcast_convert_type(acc[half].T, jnp.int32)
                    key = bits ^ ((bits >> 31) & jnp.int32(0x7FFFFFFF))
                    store_keys(qb, first + half, jnp.where(causal(qb, first + half), key, int_min))
        return carry

    lax.fori_loop(0, n_tiles, score_tile, 0)

    def count16(hits):
        def body(t, accs):
            accs = list(accs)
            for u in range(per):
                rows = chunk_rows(t, u)
                for qb in blocks:
                    hit = jnp.where(hits(qb, rows), jnp.int16(1), jnp.int16(0)).reshape(c_ // 16, 16, c_)
                    accs[qb * per + u] = accs[qb * per + u] + functools.reduce(
                        lambda a, b: a + b, [hit[i] for i in range(c_ // 16)])
            return tuple(accs)
        accs = lax.fori_loop(0, n_tiles, body, (jnp.zeros((16, c_), jnp.int16),) * (per * nqb))
        out = []
        for qb in blocks:
            total = functools.reduce(lambda a, b: a + b, [a.astype(jnp.int32) for a in accs[qb * per:(qb + 1) * per]])
            out.append(jnp.sum(total.astype(jnp.float32), axis=0, keepdims=True))
        return out

    def kth_largest16(ref, ks):
        zero = jnp.zeros((1, c_), jnp.int32)

        def counts_ge(cands):
            c16 = [c.astype(jnp.int16) for c in cands]
            return count16(lambda qb, rows: ref[qb, rows, :] >= c16[qb])

        first = counts_ge([zero] * nqb)
        vs = tuple(jnp.where(n >= k, zero, i16_min) for n, k in zip(first, ks))
        ns = tuple(jnp.where(n >= k, n, jnp.inf) for n, k in zip(first, ks))

        def bit_step(i, carry):
            vs, ns = carry
            cands = [v | jnp.left_shift(jnp.int32(1), 14 - i) for v in vs]
            counts = counts_ge(cands)
            ok = [n >= k for n, k in zip(counts, ks)]
            return (tuple(jnp.where(o, c, v) for o, c, v in zip(ok, cands, vs)),
                    tuple(jnp.where(o, n, m) for o, n, m in zip(ok, counts, ns)))

        return lax.fori_loop(0, 15, bit_step, (vs, ns))

    kf = jnp.float32(topk)
    thr_hi, _ = kth_largest16(hi_ref, [kf] * nqb)
    th = [v.astype(jnp.int16) for v in thr_hi]
    k_lo = [kf - n for n in count16(lambda qb, rows: hi_ref[qb, rows, :] > th[qb])]

    def restrict_lo(t, carry):
        for u in range(per):
            rows = chunk_rows(t, u)
            for qb in blocks:
                lo_ref[qb, rows, :] = jnp.where(hi_ref[qb, rows, :] == th[qb], lo_ref[qb, rows, :],
                                                jnp.int16(-2 ** 15))
        return carry

    lax.fori_loop(0, n_tiles, restrict_lo, 0)
    thr_lo, n_lo = kth_largest16(lo_ref, k_lo)
    n_ge = [kf - k + n for k, n in zip(k_lo, n_lo)]
    some_tie_left_out = jnp.max(functools.reduce(jnp.maximum, [jnp.where(n > kf, 1.0, 0.0) for n in n_ge])) > 0.0
    thr_lo = [jnp.where((h == i16_min) & (l == i16_min), i16_min + 1, l) for h, l in zip(thr_hi, thr_lo)]
    tl = [v.astype(jnp.int16) for v in thr_lo]

    def cmp_key(qb, rows, lo_test):
        hi = hi_ref[qb, rows, :]
        return (hi > th[qb]) | ((hi == th[qb]) & lo_test(lo_ref[qb, rows, :], tl[qb]))

    @pl.when(jnp.logical_not(some_tie_left_out))
    def _():
        def mask_tile(t, carry):
            for u in range(per):
                rows = chunk_rows(t, u)
                for qb in blocks:
                    take = cmp_key(qb, rows, lambda lo, t: lo >= t)
                    mask_ref[qb, rows, :] = jnp.where(take, jnp.ones((), mask_ref.dtype), jnp.zeros((), mask_ref.dtype))
            return carry
        lax.fori_loop(0, n_tiles, mask_tile, 0)

    @pl.when(some_tie_left_out)
    def _():
        tri = jnp.where(sub > lane, 1.0, 0.0).astype(jnp.bfloat16)
        thr = [jnp.left_shift(h, 16) | ((l + 32768) & 0xFFFF) for h, l in zip(thr_hi, thr_lo)]
        room = [kf - n for n in count16(lambda qb, rows: cmp_key(qb, rows, lambda lo, t: lo > t))]

        def mask_tile(t, seen):
            seen = list(seen)
            for u in range(per):
                rows = chunk_rows(t, u)
                for qb in blocks:
                    kc = (jnp.left_shift(hi_ref[qb, rows, :].astype(jnp.int32), 16)
                          | ((lo_ref[qb, rows, :].astype(jnp.int32) + 32768) & 0xFFFF))
                    eq = jnp.where(kc == thr[qb], 1.0, 0.0)
                    before = jnp.dot(tri, eq.astype(jnp.bfloat16), preferred_element_type=jnp.float32) + seen[qb]
                    take = jnp.where(kc > thr[qb], 1.0, jnp.where(before < room[qb], eq, 0.0))
                    mask_ref[qb, rows, :] = take.astype(mask_ref.dtype)
                    seen[qb] = seen[qb] + jnp.sum(eq, axis=0, keepdims=True)
            return tuple(seen)
        lax.fori_loop(0, n_tiles, mask_tile, (jnp.zeros((1, c_), jnp.float32),) * nqb)

    def zero_tile(t, carry):
        for qb in blocks:
            mask_ref[qb, pl.ds(pl.multiple_of(t * tk, tk), tk), :] = jnp.zeros((tk, c_), mask_ref.dtype)
        return carry

    lax.fori_loop(n_tiles, nq // per, zero_tile, 0)


def _dsa_select(h, cos_b, sin_b, kidx, topk):
    batch, seq = kidx.shape[0], kidx.shape[1]
    nq = seq // Q_BLOCK
    per = min(4, nq)
    nqb = 2
    assert nq % per == 0 and per % 2 == 0 and nq % nqb == 0 and seq < 2 ** 15
    rows = nqb * Q_BLOCK
    row = lambda b, j: b * (nq // nqb) + j
    n_iq = IDX_HEADS * IDX_DIM // IQ_BLOCK
    assert n_iq == 4
    tab = pl.BlockSpec((rows, LANES), lambda b, j: (row(b, j), 0))
    return pl.pallas_call(
        functools.partial(_dsa_select_kernel, topk=topk, nq=nq, per=per, nqb=nqb),
        grid=(batch, nq // nqb),
        in_specs=[_h_block("iq", IQ_BLOCK, rows, row, part) for part in range(n_iq)]
                 + [_h_block("iw", LANES, rows, row), tab, tab,
                    pl.BlockSpec((None, seq, IDX_DIM), lambda b, j: (b, 0, 0), pipeline_mode=pl.Buffered(1))],
        out_specs=pl.BlockSpec((None, nqb, seq, Q_BLOCK), lambda b, j: (b, j, 0, 0)),
        out_shape=jax.ShapeDtypeStruct((batch, nq, seq, Q_BLOCK), jnp.bfloat16),
        scratch_shapes=[pltpu.VMEM((nqb, seq, Q_BLOCK), jnp.int16),
                        pltpu.VMEM((nqb, seq, Q_BLOCK), jnp.int16),
                        pltpu.VMEM((nqb, IDX_HEADS * Q_BLOCK, IDX_DIM), jnp.bfloat16),
                        pltpu.VMEM((nqb, IDX_HEADS * Q_BLOCK, LANES), jnp.float32)],
        compiler_params=_cparams("parallel", "arbitrary"),
        name="dsa_select",
    )(*([h] * (n_iq + 1)), cos_b, sin_b, kidx)


def _dsa_attn_kernel(dq_ref, cos_ref, sin_ref, qgain_ref, k_ref, vt_ref, kn_ref, mask_ref, dg_ref, o_ref,
                     qg_ref, m_ref, acc_ref, s_ref, p_ref, alpha_ref, *, tk, n_key_tiles):
    j = pl.program_id(1)
    per = tk // Q_BLOCK
    d = DSA_HEAD_DIM
    gw = DSA_GROUP * Q_BLOCK
    heads = range(DSA_KV_HEADS)
    n_live = (j + per) // per
    n_pairs = (n_live + 1) // 2
    neg_inf = jnp.float32(-jnp.inf)
    q_scale = DSA_HEAD_DIM ** -0.5 * np.log2(np.e).astype(np.float32)
    cos, sin, qgain = cos_ref[...], sin_ref[...], qgain_ref[...]
    for kv in heads:
        for g in range(DSA_GROUP):
            hh = kv * DSA_GROUP + g
            x = dq_ref[:, hh * d:(hh + 1) * d].astype(jnp.float32)
            qg_ref[kv, g * Q_BLOCK:(g + 1) * Q_BLOCK, :] = (
                _rope_half(_rms_head(x, qgain), cos, sin) * q_scale).astype(qg_ref.dtype)
    acc_ref[...] = jnp.zeros_like(acc_ref)
    p_ref[:, 1] = jnp.zeros((DSA_KV_HEADS,) + p_ref.shape[2:], p_ref.dtype)
    ones_rows = jnp.ones((acc_ref.shape[1] - d, tk), jnp.bfloat16)

    k_max2 = jnp.max(kn_ref[...], axis=0)
    ones8 = jnp.ones((8, d), jnp.bfloat16)
    bound = []
    for kv in heads:
        qf = qg_ref[kv].astype(jnp.float32)
        q_norm2 = lax.dot_general(ones8, (qf * qf).astype(jnp.bfloat16), _NT,
                                  preferred_element_type=jnp.float32)[0:1, :]
        bound.append(jnp.sqrt(q_norm2 * jnp.concatenate([k_max2[kv:kv + 1, :]] * DSA_GROUP, axis=1)) * 1.03 + 1e-3)
    bounded_is_safe = jnp.max(functools.reduce(jnp.maximum, bound)) <= 50.0

    def logits(kv, t):
        r0 = pl.multiple_of(t * tk, tk)
        return lax.dot_general(k_ref[kv, pl.ds(r0, tk), :], qg_ref[kv], _NT, preferred_element_type=jnp.float32)

    def softmax_running_max(kv, sel, slot):
        s = s_ref[kv, slot]
        s = jnp.concatenate([jnp.where(sel, s[:, g * Q_BLOCK:(g + 1) * Q_BLOCK], neg_inf)
                             for g in range(DSA_GROUP)], axis=1)
        m_old = m_ref[kv]
        m_new = jnp.maximum(m_old, jnp.max(s, axis=0, keepdims=True))
        m_safe = jnp.where(m_new == neg_inf, 0.0, m_new)
        p_ref[kv, slot] = jnp.exp2(s - m_safe).astype(p_ref.dtype)
        alpha_ref[kv, slot] = jnp.exp2(m_old - m_safe)
        m_ref[kv] = m_new

    def softmax_bounded(kv, sel, slot):
        s = s_ref[kv, slot]
        parts = []
        for g in range(DSA_GROUP):
            cols = slice(g * Q_BLOCK, (g + 1) * Q_BLOCK)
            shift = jnp.where(sel, -bound[kv][:, cols], neg_inf)
            parts.append(jnp.exp2(s[:, cols] + shift))
        p_ref[kv, slot] = jnp.concatenate(parts, axis=1).astype(p_ref.dtype)

    def pv_stage(kv, t, slot, rescale):
        vt = jnp.concatenate([vt_ref[kv, t * per + i] for i in range(per)], axis=1)
        lhs = jnp.concatenate([vt, ones_rows], axis=0)
        pv = jnp.dot(lhs, p_ref[kv, slot], preferred_element_type=jnp.float32)
        acc_ref[kv] = (acc_ref[kv] * alpha_ref[kv, slot] if rescale else acc_ref[kv]) + pv

    def run(softmax_stage, rescale):
        for kv in heads:
            s_ref[kv, 0] = logits(kv, 0)

        def pair(i, carry):
            for u in range(2):
                t = 2 * i + u
                r0 = pl.multiple_of(t * tk, tk)
                sel = mask_ref[pl.ds(r0, tk), :].astype(jnp.float32) > 0.5
                for kv in heads:
                    s_ref[kv, 1 - u] = logits(kv, jnp.minimum(t + 1, n_key_tiles - 1))
                    softmax_stage(kv, sel, u)
                    pv_stage(kv, jnp.maximum(t - 1, 0), 1 - u, rescale)
            return carry

        lax.fori_loop(0, n_pairs, pair, 0)
        for kv in heads:
            pv_stage(kv, 2 * n_pairs - 1, 1, rescale)

    @pl.when(bounded_is_safe)
    def _():
        run(softmax_bounded, rescale=False)

    @pl.when(jnp.logical_not(bounded_is_safe))
    def _():
        m_ref[...] = jnp.full_like(m_ref, neg_inf)
        alpha_ref[:, 1] = jnp.ones((DSA_KV_HEADS,) + alpha_ref.shape[2:], alpha_ref.dtype)
        run(softmax_running_max, rescale=True)

    for kv in heads:
        out_t = acc_ref[kv, :d, :] / acc_ref[kv, d:d + 1, :]
        for g in range(DSA_GROUP):
            cols = slice(kv * gw + g * Q_BLOCK, kv * gw + (g + 1) * Q_BLOCK)
            gate = dg_ref[:, cols].astype(jnp.float32)
            o_ref[:, cols] = (gate * jax.nn.sigmoid(gate)
                              * out_t[:, g * Q_BLOCK:(g + 1) * Q_BLOCK].T).astype(o_ref.dtype)


def _dsa_attention(k, vt, kn, mask, h, cos_b, sin_b, q_gain, batch, seq):
    nq = seq // Q_BLOCK
    d = DSA_HEAD_DIM
    gw = DSA_GROUP * d
    tk = min(256, seq // 2)
    assert seq % (2 * tk) == 0
    assert SEG["dg"] % BRANCH_W == 0
    kvh = DSA_KV_HEADS
    once = pl.Buffered(1)
    row = lambda b, j: b * nq + j
    tab = pl.BlockSpec((Q_BLOCK, LANES), lambda b, j: (row(b, j), 0))
    return pl.pallas_call(
        functools.partial(_dsa_attn_kernel, tk=tk, n_key_tiles=seq // tk),
        grid=(batch, nq),
        in_specs=[_h_block("dq", BRANCH_W, Q_BLOCK, row), tab, tab, pl.BlockSpec((1, d), lambda b, j: (0, 0)),
                  pl.BlockSpec((None, kvh, seq, d), lambda b, j: (b, 0, 0, 0), pipeline_mode=once),
                  pl.BlockSpec((None, kvh, nq, d, Q_BLOCK), lambda b, j: (b, 0, 0, 0, 0), pipeline_mode=once),
                  pl.BlockSpec((None, nq, kvh, LANES), lambda b, j: (b, 0, 0, 0), pipeline_mode=once),
                  pl.BlockSpec((None, None, seq, Q_BLOCK), lambda b, j: (b, j, 0, 0)),
                  pl.BlockSpec((Q_BLOCK, BRANCH_W), lambda b, j: (b * nq + j, SEG["dg"] // BRANCH_W))],
        out_specs=pl.BlockSpec((Q_BLOCK, BRANCH_W), lambda b, j: (b * nq + j, 0)),
        out_shape=jax.ShapeDtypeStruct((batch * seq, BRANCH_W), jnp.bfloat16),
        scratch_shapes=[pltpu.VMEM((kvh, gw, d), jnp.bfloat16),
                        pltpu.VMEM((kvh, 1, gw), jnp.float32),
                        pltpu.VMEM((kvh, d + 16, gw), jnp.float32),
                        pltpu.VMEM((kvh, 2, tk, gw), jnp.float32),
                        pltpu.VMEM((kvh, 2, tk, gw), jnp.bfloat16),
                        pltpu.VMEM((kvh, 2, 1, gw), jnp.float32)],
        compiler_params=_cparams("parallel", "arbitrary"),
        name="dsa_attn",
    )(h, cos_b, sin_b, q_gain.reshape(1, d), k, vt, kn, mask, h)


def _gelu_tanh(x):
    return 0.5 * x * (1.0 + jnp.tanh(np.sqrt(2.0 / np.pi).astype(np.float32) * (x + 0.044715 * (x * x * x))))


def _gmlp_kernel(u_ref, v_ref, g_ref, gain_ref, w_ref, b_ref, o_ref):
    c_ = GM_CHUNK
    v = _gelu_tanh(v_ref[...].astype(jnp.float32))
    mu = jnp.mean(v, axis=-1, keepdims=True)
    vc = v - mu
    vn = (vc * lax.rsqrt(jnp.mean(vc * vc, axis=-1, keepdims=True) + EPS) * gain_ref[...]).astype(jnp.bfloat16)
    sub = lax.broadcasted_iota(jnp.int32, (c_, c_), 0)
    lane = lax.broadcasted_iota(jnp.int32, (c_, c_), 1)
    tril = sub >= lane
    for g in range(GM_GROUPS):
        cols = slice(g * GM_GROUP_DIM, (g + 1) * GM_GROUP_DIM)
        w = jnp.where(tril, w_ref[g], jnp.zeros((), w_ref.dtype))
        mixed = jnp.dot(w, vn[:, cols], preferred_element_type=jnp.float32) + b_ref[g]
        u = _gelu_tanh(u_ref[:, cols].astype(jnp.float32))
        gate = g_ref[:, cols].astype(jnp.float32)
        o_ref[:, cols] = (gate * jax.nn.sigmoid(gate) * (u * mixed)).astype(o_ref.dtype)


def _gmlp(h, gm_gain, w_spatial, b_spatial, m):
    c_ = GM_CHUNK
    w = BRANCH_W

    def seg(name):
        assert SEG[name] % w == 0
        base = SEG[name] // w
        return pl.BlockSpec((c_, w), lambda i: (i, base))

    b_b = jnp.broadcast_to(b_spatial[:, :, None], (GM_GROUPS, c_, GM_GROUP_DIM))
    whole = lambda shape: pl.BlockSpec(shape, lambda i: (0,) * len(shape))
    return pl.pallas_call(
        _gmlp_kernel,
        grid=(m // c_,),
        in_specs=[seg("gu"), seg("gv"), seg("gg"), whole((1, w)), whole((GM_GROUPS, c_, c_)),
                  whole((GM_GROUPS, c_, GM_GROUP_DIM))],
        out_specs=pl.BlockSpec((c_, w), lambda i: (i, 0)),
        out_shape=jax.ShapeDtypeStruct((m, w), jnp.bfloat16),
        compiler_params=_cparams("parallel"),
        name="gmlp",
    )(h, h, h, gm_gain.reshape(1, w), w_spatial.astype(jnp.bfloat16), b_b)


def _merge_kernel(y0_ref, y1_ref, y2_ref, w_ref, g0_ref, g1_ref, g2_ref, o_ref):
    acc = None
    for b, (y_ref, g_ref) in enumerate(((y0_ref, g0_ref), (y1_ref, g1_ref), (y2_ref, g2_ref))):
        proj = jnp.dot(y_ref[...], w_ref[b], preferred_element_type=jnp.float32)
        term = jax.nn.sigmoid(g_ref[...].astype(jnp.float32)) * proj
        acc = term if acc is None else acc + term
    o_ref[...] = acc.astype(o_ref.dtype)


def _merge(ys, w_branch, h):
    m = h.shape[0]
    tm, tn = min(1024, m), 512
    ybs = pl.BlockSpec((tm, BRANCH_W), lambda i, j: (i, 0))

    def gate(b):
        base = (SEG["mg"] + b * D_MODEL) // tn
        return pl.BlockSpec((tm, tn), lambda i, j: (i, base + j))

    return pl.pallas_call(
        _merge_kernel,
        grid=(m // tm, D_MODEL // tn),
        in_specs=[ybs, ybs, ybs, pl.BlockSpec((N_BRANCH, BRANCH_W, tn), lambda i, j: (0, 0, j)),
                  gate(0), gate(1), gate(2)],
        out_specs=pl.BlockSpec((tm, tn), lambda i, j: (i, j)),
        out_shape=jax.ShapeDtypeStruct((m, D_MODEL), jnp.bfloat16),
        compiler_params=_cparams("parallel", "arbitrary"),
        name="merge",
    )(*ys, w_branch, h, h, h)


def _out_proj_kernel(a_ref, w_ref, x_ref, o_ref):
    o_ref[...] = x_ref[...] + jnp.dot(a_ref[...], w_ref[...], preferred_element_type=jnp.float32)


def _out_proj(merged, w_out, x2):
    m, d = x2.shape
    tm, tn = min(1024, m), 1024
    return pl.pallas_call(
        _out_proj_kernel,
        grid=(m // tm, d // tn),
        in_specs=[pl.BlockSpec((tm, d), lambda i, j: (i, 0)), pl.BlockSpec((d, tn), lambda i, j: (0, j)),
                  pl.BlockSpec((tm, tn), lambda i, j: (i, j))],
        out_specs=pl.BlockSpec((tm, tn), lambda i, j: (i, j)),
        out_shape=jax.ShapeDtypeStruct((m, d), jnp.float32),
        compiler_params=_cparams("parallel", "arbitrary"),
        name="out_proj",
    )(merged, w_out, x2)


def _w_in_runs():
    runs, o = [], 0
    for name in _MY_ORDER:
        off, width = _REF_OFF[name]
        assert SEG[name] == o
        if runs and runs[-1][1] is not None and runs[-1][1] + runs[-1][2] == off and runs[-1][0] + runs[-1][2] == o:
            runs[-1][2] += width
        else:
            runs.append([o, off, width])
        padded = -(-width // LANES) * LANES
        if padded != width:
            runs.append([o + width, None, padded - width])
        o += padded
    if N_IN_PAD != o:
        runs.append([o, None, N_IN_PAD - o])
    return runs


RELAYOUT_COLS = 256


def _relayout_plan():
    src_of = np.full((N_IN_PAD,), -1, np.int64)
    for dst, src, width in _w_in_runs():
        if src is not None:
            src_of[dst:dst + width] = np.arange(src, src + width)
    first, valid = [], []
    for c in range(N_IN_PAD // RELAYOUT_COLS):
        cols = src_of[c * RELAYOUT_COLS:(c + 1) * RELAYOUT_COLS]
        n = int((cols >= 0).sum())
        assert n > 0 and (cols[:n] == cols[0] + np.arange(n)).all() and (cols[n:] < 0).all()
        first.append(int(cols[0]))
        valid.append(n)
    return np.asarray(first, np.int32), np.asarray(valid, np.int32)


def _relayout_kernel(first_ref, valid_ref, wt_ref, o_ref):
    x = wt_ref[0].T
    lane = lax.broadcasted_iota(jnp.int32, x.shape, 1)
    o_ref[...] = jnp.where(lane < valid_ref[pl.program_id(0)], x, 0.0).astype(o_ref.dtype)


def _relayout_w_in(w_in, layer):
    _, d, n = w_in.shape
    first, valid = _relayout_plan()
    assert N_IN_PAD % RELAYOUT_COLS == 0 and int((first + RELAYOUT_COLS).max()) <= n
    wt = jnp.swapaxes(w_in, 1, 2)
    return pl.pallas_call(
        _relayout_kernel,
        grid_spec=pltpu.PrefetchScalarGridSpec(
            num_scalar_prefetch=2,
            grid=(N_IN_PAD // RELAYOUT_COLS,),
            in_specs=[pl.BlockSpec((pl.Element(1), pl.Element(RELAYOUT_COLS), pl.Element(d)),
                                   lambda c, first, valid: (layer, pl.multiple_of(first[c], 8), 0))],
            out_specs=pl.BlockSpec((d, RELAYOUT_COLS), lambda c, first, valid: (0, c)),
        ),
        out_shape=jax.ShapeDtypeStruct((d, N_IN_PAD), jnp.bfloat16),
        compiler_params=_cparams("parallel"),
        name="relayout_w_in",
    )(jnp.asarray(first), jnp.asarray(valid), wt)


def kernel(x, positions, norm_gain, w_in, ret_norm_gain, q_norm_gain, k_norm_gain, gm_norm_gain, w_spatial,
           b_spatial, w_branch, w_out):
    batch, seq, d = x.shape
    assert d == D_MODEL and seq % Q_BLOCK == 0
    depth = w_in.shape[0]
    m = batch * seq
    topk = min(TOPK_MAX, seq // 4)
    cos_a, sin_a, cos_b, sin_b = _rope_tables(positions)
    x2 = x.reshape(m, d)
    for l in range(depth):
        xn = _rmsnorm(x2, norm_gain[l])
        h = _in_proj(xn, _relayout_w_in(w_in, l))
        y_ret = _retention(h, cos_a, sin_a, ret_norm_gain[l], batch, seq)
        k, vt, kidx, kn = _dsa_prep(h, cos_b, sin_b, k_norm_gain[l], batch, seq)
        mask = _dsa_select(h, cos_b, sin_b, kidx, topk)
        y_dsa = _dsa_attention(k, vt, kn, mask, h, cos_b, sin_b, q_norm_gain[l], batch, seq)
        y_gm = _gmlp(h, gm_norm_gain[l], w_spatial[l], b_spatial[l], m)
        merged = _merge((y_ret, y_dsa, y_gm), w_branch[l].astype(jnp.bfloat16), h)
        x2 = _out_proj(merged, w_out[l].astype(jnp.bfloat16), x2)
    return x2.reshape(batch, seq, d)
```

```python
import functools

import numpy as np
import jax
import jax.numpy as jnp
from jax import lax
from jax.experimental import pallas as pl
from jax.experimental.pallas import tpu as pltpu

D_MODEL = 4096
BRANCH_W = D_MODEL // 2
N_BRANCH = 3
RET_HEAD_DIM = 256
RET_HEADS = BRANCH_W // RET_HEAD_DIM
RET_CHUNK = 128
DSA_HEAD_DIM = 128
DSA_HEADS = BRANCH_W // DSA_HEAD_DIM
DSA_KV_HEADS = 4
DSA_GROUP = DSA_HEADS // DSA_KV_HEADS
DSA_KV_W = DSA_KV_HEADS * DSA_HEAD_DIM
IDX_HEADS = 32
IDX_DIM = 128
TOPK_MAX = 256
Q_BLOCK = 128
GM_GROUPS = 16
GM_GROUP_DIM = BRANCH_W // GM_GROUPS
GM_CHUNK = 128
ROPE_THETA = 10000.0
EPS = 1e-6

LANES = 128
VMEM_LIMIT = 56 * 1024 * 1024

_REF_SEGS = (("rq", BRANCH_W), ("rk", BRANCH_W), ("rv", BRANCH_W), ("rg", BRANCH_W),
             ("dq", BRANCH_W), ("dk", DSA_KV_W), ("dv", DSA_KV_W),
             ("iq", IDX_HEADS * IDX_DIM), ("ik", IDX_DIM), ("iw", IDX_HEADS),
             ("dg", BRANCH_W), ("gu", BRANCH_W), ("gv", BRANCH_W), ("gg", BRANCH_W),
             ("mg", N_BRANCH * D_MODEL))
_MY_ORDER = ("dg", "gu", "gv", "gg", "mg", "rq", "rk", "rv", "rg", "dq", "dk", "dv", "iq", "ik", "iw")
N_TILE = 768
IQ_BLOCK = 1024


def _layout():
    ref_off, o = {}, 0
    for name, w in _REF_SEGS:
        ref_off[name] = (o, w)
        o += w
    my_off, o = {}, 0
    for name in _MY_ORDER:
        w = ref_off[name][1]
        my_off[name] = o
        o += -(-w // LANES) * LANES
    total = -(-o // N_TILE) * N_TILE
    return ref_off, my_off, total


_REF_OFF, SEG, N_IN_PAD = _layout()

_NT = (((1,), (1,)), ((), ()))


def _cparams(*sem):
    return pltpu.CompilerParams(dimension_semantics=sem, vmem_limit_bytes=VMEM_LIMIT)


def _rmsnorm_kernel(x_ref, g_ref, o_ref):
    x = x_ref[...]
    ms = jnp.mean(x * x, axis=-1, keepdims=True)
    o_ref[...] = (x * lax.rsqrt(ms + EPS) * g_ref[...]).astype(o_ref.dtype)


def _rmsnorm(x2, gain):
    m, d = x2.shape
    tm = min(512, m)
    return pl.pallas_call(
        _rmsnorm_kernel,
        grid=(m // tm,),
        in_specs=[pl.BlockSpec((tm, d), lambda i: (i, 0)), pl.BlockSpec((1, d), lambda i: (0, 0))],
        out_specs=pl.BlockSpec((tm, d), lambda i: (i, 0)),
        out_shape=jax.ShapeDtypeStruct((m, d), jnp.bfloat16),
        compiler_params=_cparams("parallel"),
        name="rmsnorm",
    )(x2, gain.reshape(1, d))


def _matmul_kernel(a_ref, w_ref, o_ref):
    o_ref[...] = jnp.dot(a_ref[...], w_ref[...], preferred_element_type=jnp.float32).astype(o_ref.dtype)


def _in_proj(xn, w):
    m, k = xn.shape
    n = w.shape[1]
    tm, tn = min(1024, m), N_TILE
    return pl.pallas_call(
        _matmul_kernel,
        grid=(m // tm, n // tn),
        in_specs=[pl.BlockSpec((tm, k), lambda i, j: (i, 0)), pl.BlockSpec((k, tn), lambda i, j: (0, j))],
        out_specs=pl.BlockSpec((tm, tn), lambda i, j: (i, j)),
        out_shape=jax.ShapeDtypeStruct((m, n), jnp.bfloat16),
        compiler_params=_cparams("parallel", "arbitrary"),
        name="in_proj",
    )(xn, w)


def _rope_tables_kernel(pos_ref, inv_a_ref, inv_b_ref, sign_ref, ca_ref, sa_ref, cb_ref, sb_ref):
    p = pos_ref[...].astype(jnp.float32)
    ang_a = p * inv_a_ref[...]
    ca_ref[...] = jnp.cos(ang_a)
    sa_ref[...] = jnp.sin(ang_a)
    ang_b = p * inv_b_ref[...]
    cb_ref[...] = jnp.cos(ang_b)
    sb_ref[...] = jnp.sin(ang_b) * sign_ref[...]


def _rope_tables(positions):
    m = positions.size
    tr = min(1024, m)

    def inv_freq(d):
        half = d // 2
        return 1.0 / (ROPE_THETA ** (jnp.arange(half, dtype=jnp.float32) * 2.0 / d))

    inv_a = inv_freq(RET_HEAD_DIM).reshape(1, LANES)
    inv_b = jnp.tile(inv_freq(DSA_HEAD_DIM), 2).reshape(1, LANES)
    sign = jnp.concatenate([-jnp.ones((LANES // 2,), jnp.float32), jnp.ones((LANES // 2,), jnp.float32)]).reshape(1, LANES)
    row = pl.BlockSpec((1, LANES), lambda i: (0, 0))
    tab = pl.BlockSpec((tr, LANES), lambda i: (i, 0))
    sds = jax.ShapeDtypeStruct((m, LANES), jnp.float32)
    return pl.pallas_call(
        _rope_tables_kernel,
        grid=(m // tr,),
        in_specs=[pl.BlockSpec((tr, 1), lambda i: (i, 0)), row, row, row],
        out_specs=[tab, tab, tab, tab],
        out_shape=[sds, sds, sds, sds],
        compiler_params=_cparams("parallel"),
        name="rope_tables",
    )(positions.reshape(m, 1), inv_a, inv_b, sign)


def _retention_kernel(q_ref, k_ref, v_ref, g_ref, cos_ref, sin_ref, decay_ref, xi_ref, zeta_ref, cd_ref,
                      gain_ref, o_ref, state_ref, *, n_chunk):
    @pl.when(pl.program_id(2) == 0)
    def _():
        state_ref[...] = jnp.zeros_like(state_ref)

    half = RET_HEAD_DIM // 2
    decay = decay_ref[...]
    xi = xi_ref[...]
    zeta = zeta_ref[...]
    cd = cd_ref[0:1, :]
    gain = gain_ref[...]
    k_scale = RET_HEAD_DIM ** -0.5

    def rot(x, cos, sin):
        x1, x2 = x[:, :half], x[:, half:]
        return x1 * cos - x2 * sin, x2 * cos + x1 * sin

    for c in range(n_chunk):
        rows = slice(c * RET_CHUNK, (c + 1) * RET_CHUNK)
        cos, sin = cos_ref[rows, :], sin_ref[rows, :]
        q1, q2 = rot(q_ref[rows, :].astype(jnp.float32), cos, sin)
        k1, k2 = rot(k_ref[rows, :].astype(jnp.float32), cos, sin)
        k1, k2 = k1 * k_scale, k2 * k_scale
        v = v_ref[rows, :]
        qb = jnp.concatenate([q1, q2], axis=1).astype(jnp.bfloat16)
        kb = jnp.concatenate([k1, k2], axis=1).astype(jnp.bfloat16)
        sc = lax.dot_general(qb, kb, _NT, preferred_element_type=jnp.float32) * decay
        inner = jnp.dot(sc.astype(jnp.bfloat16), v, preferred_element_type=jnp.float32)
        state = state_ref[...]
        qx = jnp.concatenate([q1 * xi, q2 * xi], axis=1).astype(jnp.bfloat16)
        cross = jnp.dot(qx, state.astype(jnp.bfloat16), preferred_element_type=jnp.float32)
        out = inner + cross
        kz_t = jnp.concatenate([(k1 * zeta).T, (k2 * zeta).T], axis=0).astype(jnp.bfloat16)
        state_ref[...] = (state * jnp.concatenate([cd, cd], axis=1)
                          + jnp.dot(kz_t, v, preferred_element_type=jnp.float32))
        mu = jnp.mean(out, axis=-1, keepdims=True)
        oc = out - mu
        y = oc * lax.rsqrt(jnp.mean(oc * oc, axis=-1, keepdims=True) + EPS) * gain
        g = g_ref[rows, :].astype(jnp.float32)
        o_ref[rows, :] = (g * jax.nn.sigmoid(g) * y).astype(o_ref.dtype)


def _retention(h, cos_a, sin_a, ret_gain, batch, seq):
    m = batch * seq
    hd = RET_HEAD_DIM
    rb = min(4096, seq)
    nrb = seq // rb
    c = RET_CHUNK
    log_g = jnp.log(1.0 - jnp.power(2.0, -5.0 - jnp.arange(RET_HEADS, dtype=jnp.float32)))
    n = jnp.arange(c, dtype=jnp.float32)
    diff = n[:, None] - n[None, :]
    decay = jnp.where(diff >= 0, jnp.exp(log_g[:, None, None] * jnp.maximum(diff, 0.0)), 0.0)
    xi = jnp.broadcast_to(jnp.exp(log_g[:, None] * (n + 1.0))[:, :, None], (RET_HEADS, c, LANES))
    zeta = jnp.broadcast_to(jnp.exp(log_g[:, None] * (c - 1.0 - n))[:, :, None], (RET_HEADS, c, LANES))
    cd = jnp.broadcast_to(jnp.exp(log_g * c)[:, None, None], (RET_HEADS, 8, LANES))

    def seg(name):
        assert SEG[name] % hd == 0
        base = SEG[name] // hd
        return pl.BlockSpec((rb, hd), lambda b, hh, r: (b * nrb + r, base + hh))

    tab = pl.BlockSpec((rb, LANES), lambda b, hh, r: (b * nrb + r, 0))
    per_head = lambda rows: pl.BlockSpec((None, rows, LANES), lambda b, hh, r: (hh, 0, 0))
    return pl.pallas_call(
        functools.partial(_retention_kernel, n_chunk=rb // c),
        grid=(batch, RET_HEADS, nrb),
        in_specs=[seg("rq"), seg("rk"), seg("rv"), seg("rg"), tab, tab,
                  per_head(c), per_head(c), per_head(c), per_head(8),
                  pl.BlockSpec((1, hd), lambda b, hh, r: (0, hh))],
        out_specs=pl.BlockSpec((rb, hd), lambda b, hh, r: (b * nrb + r, hh)),
        out_shape=jax.ShapeDtypeStruct((m, BRANCH_W), jnp.bfloat16),
        scratch_shapes=[pltpu.VMEM((hd, hd), jnp.float32)],
        compiler_params=_cparams("parallel", "parallel", "arbitrary"),
        name="retention",
    )(h, h, h, h, cos_a, sin_a, decay, xi, zeta, cd, ret_gain.reshape(1, BRANCH_W))


def _rope_half(x, cos, sin):
    return x * cos + pltpu.roll(x, DSA_HEAD_DIM // 2, 1) * sin


def _rms_head(x, g):
    return x * lax.rsqrt(jnp.mean(x * x, axis=-1, keepdims=True) + EPS) * g


def _dsa_prep_kernel(dk_ref, dv_ref, ik_ref, cos_ref, sin_ref, kgain_ref, k_ref, vt_ref, kidx_ref, kn_ref):
    cos, sin = cos_ref[...], sin_ref[...]
    d = DSA_HEAD_DIM
    kgain = kgain_ref[...]
    for kvh in range(DSA_KV_HEADS):
        x = dk_ref[:, kvh * d:(kvh + 1) * d].astype(jnp.float32)
        kb = _rope_half(_rms_head(x, kgain), cos, sin).astype(k_ref.dtype)
        k_ref[kvh] = kb
        kf = kb.astype(jnp.float32)
        kn2 = jnp.max(jnp.sum(kf * kf, axis=-1, keepdims=True), axis=0, keepdims=True)
        kn_ref[kvh:kvh + 1, :] = jnp.broadcast_to(kn2, (1, LANES))
        vt_ref[kvh] = dv_ref[:, kvh * d:(kvh + 1) * d].astype(jnp.float32).T.astype(vt_ref.dtype)
    kidx_ref[...] = _rope_half(ik_ref[...].astype(jnp.float32), cos, sin).astype(kidx_ref.dtype)


def _h_block(name, width, rows, row_index, part=0):
    assert SEG[name] % width == 0
    base = SEG[name] // width + part
    return pl.BlockSpec((rows, width), lambda b, j: (row_index(b, j), base))


def _dsa_prep(h, cos_b, sin_b, k_gain, batch, seq):
    nq = seq // Q_BLOCK
    d = DSA_HEAD_DIM
    row = lambda b, j: b * nq + j
    tab = pl.BlockSpec((Q_BLOCK, LANES), lambda b, j: (row(b, j), 0))
    bf = jnp.bfloat16
    return pl.pallas_call(
        _dsa_prep_kernel,
        grid=(batch, nq),
        in_specs=[_h_block("dk", DSA_KV_W, Q_BLOCK, row), _h_block("dv", DSA_KV_W, Q_BLOCK, row),
                  _h_block("ik", IDX_DIM, Q_BLOCK, row), tab, tab, pl.BlockSpec((1, d), lambda b, j: (0, 0))],
        out_specs=[
            pl.BlockSpec((None, DSA_KV_HEADS, Q_BLOCK, d), lambda b, j: (b, 0, j, 0)),
            pl.BlockSpec((None, DSA_KV_HEADS, None, d, Q_BLOCK), lambda b, j: (b, 0, j, 0, 0)),
            pl.BlockSpec((None, Q_BLOCK, IDX_DIM), lambda b, j: (b, j, 0)),
            pl.BlockSpec((None, None, DSA_KV_HEADS, LANES), lambda b, j: (b, j, 0, 0)),
        ],
        out_shape=[
            jax.ShapeDtypeStruct((batch, DSA_KV_HEADS, seq, d), bf),
            jax.ShapeDtypeStruct((batch, DSA_KV_HEADS, nq, d, Q_BLOCK), bf),
            jax.ShapeDtypeStruct((batch, seq, IDX_DIM), bf),
            jax.ShapeDtypeStruct((batch, nq, DSA_KV_HEADS, LANES), jnp.float32),
        ],
        compiler_params=_cparams("parallel", "parallel"),
        name="dsa_prep",
    )(h, h, h, cos_b, sin_b, k_gain.reshape(1, d))


def _dsa_select_kernel(iq0_ref, iq1_ref, iq2_ref, iq3_ref, iw_ref, cos_ref, sin_ref, kidx_ref, mask_ref,
                       hi_ref, lo_ref, qi_ref, w_ref, *, topk, nq, per, nqb):
    jp = pl.program_id(1)
    c_ = Q_BLOCK
    tk = per * c_
    blocks = range(nqb)
    iq_refs = (iq0_ref, iq1_ref, iq2_ref, iq3_ref)
    for qb in blocks:
        tok = slice(qb * c_, (qb + 1) * c_)
        cos, sin = cos_ref[tok, :], sin_ref[tok, :]
        w = iw_ref[tok, :].astype(jnp.float32) * (IDX_DIM ** -0.5 * IDX_HEADS ** -0.5)
        for head in range(IDX_HEADS):
            part, col = divmod(head * IDX_DIM, IQ_BLOCK)
            x = iq_refs[part][tok, col:col + IDX_DIM].astype(jnp.float32)
            qi_ref[qb, head * c_:(head + 1) * c_, :] = _rope_half(x, cos, sin).astype(qi_ref.dtype)
            w_ref[qb, head * c_:(head + 1) * c_, :] = jnp.broadcast_to(w[:, head:head + 1], (c_, LANES))
    sub = lax.broadcasted_iota(jnp.int32, (c_, c_), 0)
    lane = lax.broadcasted_iota(jnp.int32, (c_, c_), 1)
    int_min = jnp.int32(-2 ** 31)
    i16_min = jnp.int32(-2 ** 15)
    n_tiles = (jp * nqb + nqb - 1 + per) // per

    def causal(qb, chunk):
        return (chunk * c_ + sub) <= ((jp * nqb + qb) * c_ + lane)

    def chunk_rows(t, u):
        return pl.ds(pl.multiple_of((t * per + u) * c_, c_), c_)

    def store_keys(qb, chunk, key):
        rows = pl.ds(pl.multiple_of(chunk * c_, c_), c_)
        hi_ref[qb, rows, :] = (key >> 16).astype(jnp.int16)
        lo_ref[qb, rows, :] = ((key & 0xFFFF) - 32768).astype(jnp.int16)

    def score_tile(t, carry):
        for u2 in range(per // 2):
            first = t * per + 2 * u2
            kt = kidx_ref[pl.ds(pl.multiple_of(first * c_, 2 * c_), 2 * c_), :]
            s = lax.dot_general(qi_ref[...].reshape(nqb * IDX_HEADS * c_, IDX_DIM), kt, _NT,
                                preferred_element_type=jnp.float32)
            for qb in blocks:
                acc = [jnp.zeros((c_, c_), jnp.float32), jnp.zeros((c_, c_), jnp.float32)]
                for head in range(IDX_HEADS):
                    rows = slice((qb * IDX_HEADS + head) * c_, (qb * IDX_HEADS + head + 1) * c_)
                    w = w_ref[qb, head * c_:(head + 1) * c_, :]
                    for half in range(2):
                        acc[half] = acc[half] + jnp.maximum(s[rows, half * c_:(half + 1) * c_], 0.0) * w
                for half in range(2):
                    bits = lax.bitcast_convert_type(acc[half].T, jnp.int32)
                    key = bits ^ ((bits >> 31) & jnp.int32(0x7FFFFFFF))
                    store_keys(qb, first + half, jnp.where(causal(qb, first + half), key, int_min))
        return carry

    lax.fori_loop(0, n_tiles, score_tile, 0)

    def count16(hits):
        def body(t, accs):
            accs = list(accs)
            for u in range(per):
                rows = chunk_rows(t, u)
                for qb in blocks:
                    hit = jnp.where(hits(qb, rows), jnp.int16(1), jnp.int16(0)).reshape(c_ // 16, 16, c_)
                    accs[qb * per + u] = accs[qb * per + u] + functools.reduce(
                        lambda a, b: a + b, [hit[i] for i in range(c_ // 16)])
            return tuple(accs)
        accs = lax.fori_loop(0, n_tiles, body, (jnp.zeros((16, c_), jnp.int16),) * (per * nqb))
        out = []
        for qb in blocks:
            total = functools.reduce(lambda a, b: a + b, [a.astype(jnp.int32) for a in accs[qb * per:(qb + 1) * per]])
            out.append(jnp.sum(total.astype(jnp.float32), axis=0, keepdims=True))
        return out

    def kth_largest16(ref, ks):
        zero = jnp.zeros((1, c_), jnp.int32)

        def counts_ge(cands):
            c16 = [c.astype(jnp.int16) for c in cands]
            return count16(lambda qb, rows: ref[qb, rows, :] >= c16[qb])

        first = counts_ge([zero] * nqb)
        vs = tuple(jnp.where(n >= k, zero, i16_min) for n, k in zip(first, ks))
        ns = tuple(jnp.where(n >= k, n, jnp.inf) for n, k in zip(first, ks))

        def bit_step(i, carry):
            vs, ns = carry
            cands = [v | jnp.left_shift(jnp.int32(1), 14 - i) for v in vs]
            counts = counts_ge(cands)
            ok = [n >= k for n, k in zip(counts, ks)]
            return (tuple(jnp.where(o, c, v) for o, c, v in zip(ok, cands, vs)),
                    tuple(jnp.where(o, n, m) for o, n, m in zip(ok, counts, ns)))

        return lax.fori_loop(0, 15, bit_step, (vs, ns))

    kf = jnp.float32(topk)
    thr_hi, _ = kth_largest16(hi_ref, [kf] * nqb)
    th = [v.astype(jnp.int16) for v in thr_hi]
    k_lo = [kf - n for n in count16(lambda qb, rows: hi_ref[qb, rows, :] > th[qb])]

    def restrict_lo(t, carry):
        for u in range(per):
            rows = chunk_rows(t, u)
            for qb in blocks:
                lo_ref[qb, rows, :] = jnp.where(hi_ref[qb, rows, :] == th[qb], lo_ref[qb, rows, :],
                                                jnp.int16(-2 ** 15))
        return carry

    lax.fori_loop(0, n_tiles, restrict_lo, 0)
    thr_lo, n_lo = kth_largest16(lo_ref, k_lo)
    n_ge = [kf - k + n for k, n in zip(k_lo, n_lo)]
    some_tie_left_out = jnp.max(functools.reduce(jnp.maximum, [jnp.where(n > kf, 1.0, 0.0) for n in n_ge])) > 0.0
    thr_lo = [jnp.where((h == i16_min) & (l == i16_min), i16_min + 1, l) for h, l in zip(thr_hi, thr_lo)]
    tl = [v.astype(jnp.int16) for v in thr_lo]

    def cmp_key(qb, rows, lo_test):
        hi = hi_ref[qb, rows, :]
        return (hi > th[qb]) | ((hi == th[qb]) & lo_test(lo_ref[qb, rows, :], tl[qb]))

    @pl.when(jnp.logical_not(some_tie_left_out))
    def _():
        def mask_tile(t, carry):
            for u in range(per):
                rows = chunk_rows(t, u)
                for qb in blocks:
                    take = cmp_key(qb, rows, lambda lo, t: lo >= t)
                    mask_ref[qb, rows, :] = jnp.where(take, jnp.ones((), mask_ref.dtype), jnp.zeros((), mask_ref.dtype))
            return carry
        lax.fori_loop(0, n_tiles, mask_tile, 0)

    @pl.when(some_tie_left_out)
    def _():
        tri = jnp.where(sub > lane, 1.0, 0.0).astype(jnp.bfloat16)
        thr = [jnp.left_shift(h, 16) | ((l + 32768) & 0xFFFF) for h, l in zip(thr_hi, thr_lo)]
        room = [kf - n for n in count16(lambda qb, rows: cmp_key(qb, rows, lambda lo, t: lo > t))]

        def mask_tile(t, seen):
            seen = list(seen)
            for u in range(per):
                rows = chunk_rows(t, u)
                for qb in blocks:
                    kc = (jnp.left_shift(hi_ref[qb, rows, :].astype(jnp.int32), 16)
                          | ((lo_ref[qb, rows, :].astype(jnp.int32) + 32768) & 0xFFFF))
                    eq = jnp.where(kc == thr[qb], 1.0, 0.0)
                    before = jnp.dot(tri, eq.astype(jnp.bfloat16), preferred_element_type=jnp.float32) + seen[qb]
                    take = jnp.where(kc > thr[qb], 1.0, jnp.where(before < room[qb], eq, 0.0))
                    mask_ref[qb, rows, :] = take.astype(mask_ref.dtype)
                    seen[qb] = seen[qb] + jnp.sum(eq, axis=0, keepdims=True)
            return tuple(seen)
        lax.fori_loop(0, n_tiles, mask_tile, (jnp.zeros((1, c_), jnp.float32),) * nqb)

    def zero_tile(t, carry):
        for qb in blocks:
            mask_ref[qb, pl.ds(pl.multiple_of(t * tk, tk), tk), :] = jnp.zeros((tk, c_), mask_ref.dtype)
        return carry

    lax.fori_loop(n_tiles, nq // per, zero_tile, 0)


def _dsa_select(h, cos_b, sin_b, kidx, topk):
    batch, seq = kidx.shape[0], kidx.shape[1]
    nq = seq // Q_BLOCK
    per = min(4, nq)
    nqb = 2
    assert nq % per == 0 and per % 2 == 0 and nq % nqb == 0 and seq < 2 ** 15
    rows = nqb * Q_BLOCK
    row = lambda b, j: b * (nq // nqb) + j
    n_iq = IDX_HEADS * IDX_DIM // IQ_BLOCK
    assert n_iq == 4
    tab = pl.BlockSpec((rows, LANES), lambda b, j: (row(b, j), 0))
    return pl.pallas_call(
        functools.partial(_dsa_select_kernel, topk=topk, nq=nq, per=per, nqb=nqb),
        grid=(batch, nq // nqb),
        in_specs=[_h_block("iq", IQ_BLOCK, rows, row, part) for part in range(n_iq)]
                 + [_h_block("iw", LANES, rows, row), tab, tab,
                    pl.BlockSpec((None, seq, IDX_DIM), lambda b, j: (b, 0, 0), pipeline_mode=pl.Buffered(1))],
        out_specs=pl.BlockSpec((None, nqb, seq, Q_BLOCK), lambda b, j: (b, j, 0, 0)),
        out_shape=jax.ShapeDtypeStruct((batch, nq, seq, Q_BLOCK), jnp.bfloat16),
        scratch_shapes=[pltpu.VMEM((nqb, seq, Q_BLOCK), jnp.int16),
                        pltpu.VMEM((nqb, seq, Q_BLOCK), jnp.int16),
                        pltpu.VMEM((nqb, IDX_HEADS * Q_BLOCK, IDX_DIM), jnp.bfloat16),
                        pltpu.VMEM((nqb, IDX_HEADS * Q_BLOCK, LANES), jnp.float32)],
        compiler_params=_cparams("parallel", "arbitrary"),
        name="dsa_select",
    )(*([h] * (n_iq + 1)), cos_b, sin_b, kidx)


def _dsa_attn_kernel(dq_ref, cos_ref, sin_ref, qgain_ref, k_ref, vt_ref, kn_ref, mask_ref, dg_ref, o_ref,
                     qg_ref, m_ref, acc_ref, s_ref, p_ref, alpha_ref, *, tk, n_key_tiles):
    j = pl.program_id(1)
    per = tk // Q_BLOCK
    d = DSA_HEAD_DIM
    gw = DSA_GROUP * Q_BLOCK
    heads = range(DSA_KV_HEADS)
    n_live = (j + per) // per
    n_pairs = (n_live + 1) // 2
    neg_inf = jnp.float32(-jnp.inf)
    q_scale = DSA_HEAD_DIM ** -0.5 * np.log2(np.e).astype(np.float32)
    cos, sin, qgain = cos_ref[...], sin_ref[...], qgain_ref[...]
    for kv in heads:
        for g in range(DSA_GROUP):
            hh = kv * DSA_GROUP + g
            x = dq_ref[:, hh * d:(hh + 1) * d].astype(jnp.float32)
            qg_ref[kv, g * Q_BLOCK:(g + 1) * Q_BLOCK, :] = (
                _rope_half(_rms_head(x, qgain), cos, sin) * q_scale).astype(qg_ref.dtype)
    acc_ref[...] = jnp.zeros_like(acc_ref)
    p_ref[:, 1] = jnp.zeros((DSA_KV_HEADS,) + p_ref.shape[2:], p_ref.dtype)
    ones_rows = jnp.ones((acc_ref.shape[1] - d, tk), jnp.bfloat16)

    k_max2 = jnp.max(kn_ref[...], axis=0)
    ones8 = jnp.ones((8, d), jnp.bfloat16)
    bound = []
    for kv in heads:
        qf = qg_ref[kv].astype(jnp.float32)
        q_norm2 = lax.dot_general(ones8, (qf * qf).astype(jnp.bfloat16), _NT,
                                  preferred_element_type=jnp.float32)[0:1, :]
        bound.append(jnp.sqrt(q_norm2 * jnp.concatenate([k_max2[kv:kv + 1, :]] * DSA_GROUP, axis=1)) * 1.03 + 1e-3)
    bounded_is_safe = jnp.max(functools.reduce(jnp.maximum, bound)) <= 50.0

    def logits(kv, t):
        r0 = pl.multiple_of(t * tk, tk)
        return lax.dot_general(k_ref[kv, pl.ds(r0, tk), :], qg_ref[kv], _NT, preferred_element_type=jnp.float32)

    def softmax_running_max(kv, sel, slot):
        s = s_ref[kv, slot]
        s = jnp.concatenate([jnp.where(sel, s[:, g * Q_BLOCK:(g + 1) * Q_BLOCK], neg_inf)
                             for g in range(DSA_GROUP)], axis=1)
        m_old = m_ref[kv]
        m_new = jnp.maximum(m_old, jnp.max(s, axis=0, keepdims=True))
        m_safe = jnp.where(m_new == neg_inf, 0.0, m_new)
        p_ref[kv, slot] = jnp.exp2(s - m_safe).astype(p_ref.dtype)
        alpha_ref[kv, slot] = jnp.exp2(m_old - m_safe)
        m_ref[kv] = m_new

    def softmax_bounded(kv, sel, slot):
        s = s_ref[kv, slot]
        parts = []
        for g in range(DSA_GROUP):
            cols = slice(g * Q_BLOCK, (g + 1) * Q_BLOCK)
            shift = jnp.where(sel, -bound[kv][:, cols], neg_inf)
            parts.append(jnp.exp2(s[:, cols] + shift))
        p_ref[kv, slot] = jnp.concatenate(parts, axis=1).astype(p_ref.dtype)

    def pv_stage(kv, t, slot, rescale):
        vt = jnp.concatenate([vt_ref[kv, t * per + i] for i in range(per)], axis=1)
        lhs = jnp.concatenate([vt, ones_rows], axis=0)
        pv = jnp.dot(lhs, p_ref[kv, slot], preferred_element_type=jnp.float32)
        acc_ref[kv] = (acc_ref[kv] * alpha_ref[kv, slot] if rescale else acc_ref[kv]) + pv

    def run(softmax_stage, rescale):
        for kv in heads:
            s_ref[kv, 0] = logits(kv, 0)

        def pair(i, carry):
            for u in range(2):
                t = 2 * i + u
                r0 = pl.multiple_of(t * tk, tk)
                sel = mask_ref[pl.ds(r0, tk), :].astype(jnp.float32) > 0.5
                for kv in heads:
                    s_ref[kv, 1 - u] = logits(kv, jnp.minimum(t + 1, n_key_tiles - 1))
                    softmax_stage(kv, sel, u)
                    pv_stage(kv, jnp.maximum(t - 1, 0), 1 - u, rescale)
            return carry

        lax.fori_loop(0, n_pairs, pair, 0)
        for kv in heads:
            pv_stage(kv, 2 * n_pairs - 1, 1, rescale)

    @pl.when(bounded_is_safe)
    def _():
        run(softmax_bounded, rescale=False)

    @pl.when(jnp.logical_not(bounded_is_safe))
    def _():
        m_ref[...] = jnp.full_like(m_ref, neg_inf)
        alpha_ref[:, 1] = jnp.ones((DSA_KV_HEADS,) + alpha_ref.shape[2:], alpha_ref.dtype)
        run(softmax_running_max, rescale=True)

    for kv in heads:
        out_t = acc_ref[kv, :d, :] / acc_ref[kv, d:d + 1, :]
        for g in range(DSA_GROUP):
            cols = slice(kv * gw + g * Q_BLOCK, kv * gw + (g + 1) * Q_BLOCK)
            gate = dg_ref[:, cols].astype(jnp.float32)
            o_ref[:, cols] = (gate * jax.nn.sigmoid(gate)
                              * out_t[:, g * Q_BLOCK:(g + 1) * Q_BLOCK].T).astype(o_ref.dtype)


def _dsa_attention(k, vt, kn, mask, h, cos_b, sin_b, q_gain, batch, seq):
    nq = seq // Q_BLOCK
    d = DSA_HEAD_DIM
    gw = DSA_GROUP * d
    tk = min(256, seq // 2)
    assert seq % (2 * tk) == 0
    assert SEG["dg"] % BRANCH_W == 0
    kvh = DSA_KV_HEADS
    once = pl.Buffered(1)
    row = lambda b, j: b * nq + j
    tab = pl.BlockSpec((Q_BLOCK, LANES), lambda b, j: (row(b, j), 0))
    return pl.pallas_call(
        functools.partial(_dsa_attn_kernel, tk=tk, n_key_tiles=seq // tk),
        grid=(batch, nq),
        in_specs=[_h_block("dq", BRANCH_W, Q_BLOCK, row), tab, tab, pl.BlockSpec((1, d), lambda b, j: (0, 0)),
                  pl.BlockSpec((None, kvh, seq, d), lambda b, j: (b, 0, 0, 0), pipeline_mode=once),
                  pl.BlockSpec((None, kvh, nq, d, Q_BLOCK), lambda b, j: (b, 0, 0, 0, 0), pipeline_mode=once),
                  pl.BlockSpec((None, nq, kvh, LANES), lambda b, j: (b, 0, 0, 0), pipeline_mode=once),
                  pl.BlockSpec((None, None, seq, Q_BLOCK), lambda b, j: (b, j, 0, 0)),
                  pl.BlockSpec((Q_BLOCK, BRANCH_W), lambda b, j: (b * nq + j, SEG["dg"] // BRANCH_W))],
        out_specs=pl.BlockSpec((Q_BLOCK, BRANCH_W), lambda b, j: (b * nq + j, 0)),
        out_shape=jax.ShapeDtypeStruct((batch * seq, BRANCH_W), jnp.bfloat16),
        scratch_shapes=[pltpu.VMEM((kvh, gw, d), jnp.bfloat16),
                        pltpu.VMEM((kvh, 1, gw), jnp.float32),
                        pltpu.VMEM((kvh, d + 16, gw), jnp.float32),
                        pltpu.VMEM((kvh, 2, tk, gw), jnp.float32),
                        pltpu.VMEM((kvh, 2, tk, gw), jnp.bfloat16),
                        pltpu.VMEM((kvh, 2, 1, gw), jnp.float32)],
        compiler_params=_cparams("parallel", "arbitrary"),
        name="dsa_attn",
    )(h, cos_b, sin_b, q_gain.reshape(1, d), k, vt, kn, mask, h)


def _gelu_tanh(x):
    return 0.5 * x * (1.0 + jnp.tanh(np.sqrt(2.0 / np.pi).astype(np.float32) * (x + 0.044715 * (x * x * x))))


def _gmlp_kernel(u_ref, v_ref, g_ref, gain_ref, w_ref, b_ref, o_ref):
    c_ = GM_CHUNK
    v = _gelu_tanh(v_ref[...].astype(jnp.float32))
    mu = jnp.mean(v, axis=-1, keepdims=True)
    vc = v - mu
    vn = (vc * lax.rsqrt(jnp.mean(vc * vc, axis=-1, keepdims=True) + EPS) * gain_ref[...]).astype(jnp.bfloat16)
    sub = lax.broadcasted_iota(jnp.int32, (c_, c_), 0)
    lane = lax.broadcasted_iota(jnp.int32, (c_, c_), 1)
    tril = sub >= lane
    for g in range(GM_GROUPS):
        cols = slice(g * GM_GROUP_DIM, (g + 1) * GM_GROUP_DIM)
        w = jnp.where(tril, w_ref[g], jnp.zeros((), w_ref.dtype))
        mixed = jnp.dot(w, vn[:, cols], preferred_element_type=jnp.float32) + b_ref[g]
        u = _gelu_tanh(u_ref[:, cols].astype(jnp.float32))
        gate = g_ref[:, cols].astype(jnp.float32)
        o_ref[:, cols] = (gate * jax.nn.sigmoid(gate) * (u * mixed)).astype(o_ref.dtype)


def _gmlp(h, gm_gain, w_spatial, b_spatial, m):
    c_ = GM_CHUNK
    w = BRANCH_W

    def seg(name):
        assert SEG[name] % w == 0
        base = SEG[name] // w
        return pl.BlockSpec((c_, w), lambda i: (i, base))

    b_b = jnp.broadcast_to(b_spatial[:, :, None], (GM_GROUPS, c_, GM_GROUP_DIM))
    whole = lambda shape: pl.BlockSpec(shape, lambda i: (0,) * len(shape))
    return pl.pallas_call(
        _gmlp_kernel,
        grid=(m // c_,),
        in_specs=[seg("gu"), seg("gv"), seg("gg"), whole((1, w)), whole((GM_GROUPS, c_, c_)),
                  whole((GM_GROUPS, c_, GM_GROUP_DIM))],
        out_specs=pl.BlockSpec((c_, w), lambda i: (i, 0)),
        out_shape=jax.ShapeDtypeStruct((m, w), jnp.bfloat16),
        compiler_params=_cparams("parallel"),
        name="gmlp",
    )(h, h, h, gm_gain.reshape(1, w), w_spatial.astype(jnp.bfloat16), b_b)


def _merge_kernel(y0_ref, y1_ref, y2_ref, w_ref, g0_ref, g1_ref, g2_ref, o_ref):
    acc = None
    for b, (y_ref, g_ref) in enumerate(((y0_ref, g0_ref), (y1_ref, g1_ref), (y2_ref, g2_ref))):
        proj = jnp.dot(y_ref[...], w_ref[b], preferred_element_type=jnp.float32)
        term = jax.nn.sigmoid(g_ref[...].astype(jnp.float32)) * proj
        acc = term if acc is None else acc + term
    o_ref[...] = acc.astype(o_ref.dtype)


def _merge(ys, w_branch, layer, h):
    m = h.shape[0]
    tm, tn = min(1024, m), 512
    ybs = pl.BlockSpec((tm, BRANCH_W), lambda i, j: (i, 0))

    def gate(b):
        base = (SEG["mg"] + b * D_MODEL) // tn
        return pl.BlockSpec((tm, tn), lambda i, j: (i, base + j))

    return pl.pallas_call(
        _merge_kernel,
        grid=(m // tm, D_MODEL // tn),
        in_specs=[ybs, ybs, ybs, pl.BlockSpec((None, N_BRANCH, BRANCH_W, tn), lambda i, j: (layer, 0, 0, j)),
                  gate(0), gate(1), gate(2)],
        out_specs=pl.BlockSpec((tm, tn), lambda i, j: (i, j)),
        out_shape=jax.ShapeDtypeStruct((m, D_MODEL), jnp.bfloat16),
        compiler_params=_cparams("parallel", "arbitrary"),
        name="merge",
    )(*ys, w_branch, h, h, h)


def _out_proj_kernel(a_ref, w_ref, x_ref, o_ref):
    o_ref[...] = x_ref[...] + jnp.dot(a_ref[...], w_ref[...], preferred_element_type=jnp.float32)


def _out_proj(merged, w_out, layer, x2):
    m, d = x2.shape
    tm, tn = min(1024, m), 1024
    return pl.pallas_call(
        _out_proj_kernel,
        grid=(m // tm, d // tn),
        in_specs=[pl.BlockSpec((tm, d), lambda i, j: (i, 0)),
                  pl.BlockSpec((None, d, tn), lambda i, j: (layer, 0, j)),
                  pl.BlockSpec((tm, tn), lambda i, j: (i, j))],
        out_specs=pl.BlockSpec((tm, tn), lambda i, j: (i, j)),
        out_shape=jax.ShapeDtypeStruct((m, d), jnp.float32),
        compiler_params=_cparams("parallel", "arbitrary"),
        name="out_proj",
    )(merged, w_out, x2)


def _w_in_runs():
    runs, o = [], 0
    for name in _MY_ORDER:
        off, width = _REF_OFF[name]
        assert SEG[name] == o
        if runs and runs[-1][1] is not None and runs[-1][1] + runs[-1][2] == off and runs[-1][0] + runs[-1][2] == o:
            runs[-1][2] += width
        else:
            runs.append([o, off, width])
        padded = -(-width // LANES) * LANES
        if padded != width:
            runs.append([o + width, None, padded - width])
        o += padded
    if N_IN_PAD != o:
        runs.append([o, None, N_IN_PAD - o])
    return runs


RELAYOUT_COLS = 256


def _relayout_plan():
    src_of = np.full((N_IN_PAD,), -1, np.int64)
    for dst, src, width in _w_in_runs():
        if src is not None:
            src_of[dst:dst + width] = np.arange(src, src + width)
    first, valid = [], []
    for c in range(N_IN_PAD // RELAYOUT_COLS):
        cols = src_of[c * RELAYOUT_COLS:(c + 1) * RELAYOUT_COLS]
        n = int((cols >= 0).sum())
        assert n > 0 and (cols[:n] == cols[0] + np.arange(n)).all() and (cols[n:] < 0).all()
        first.append(int(cols[0]))
        valid.append(n)
    return np.asarray(first, np.int32), np.asarray(valid, np.int32)


def _relayout_kernel(first_ref, valid_ref, wt_ref, o_ref):
    x = wt_ref[0].T
    lane = lax.broadcasted_iota(jnp.int32, x.shape, 1)
    o_ref[...] = jnp.where(lane < valid_ref[pl.program_id(0)], x, 0.0).astype(o_ref.dtype)


def _relayout_w_in(w_in, layer):
    _, d, n = w_in.shape
    first, valid = _relayout_plan()
    assert N_IN_PAD % RELAYOUT_COLS == 0 and int((first + RELAYOUT_COLS).max()) <= n
    wt = jnp.swapaxes(w_in, 1, 2)
    return pl.pallas_call(
        _relayout_kernel,
        grid_spec=pltpu.PrefetchScalarGridSpec(
            num_scalar_prefetch=2,
            grid=(N_IN_PAD // RELAYOUT_COLS,),
            in_specs=[pl.BlockSpec((pl.Element(1), pl.Element(RELAYOUT_COLS), pl.Element(d)),
                                   lambda c, first, valid: (layer, pl.multiple_of(first[c], 8), 0))],
            out_specs=pl.BlockSpec((d, RELAYOUT_COLS), lambda c, first, valid: (0, c)),
        ),
        out_shape=jax.ShapeDtypeStruct((d, N_IN_PAD), jnp.bfloat16),
        compiler_params=_cparams("parallel"),
        name="relayout_w_in",
    )(jnp.asarray(first), jnp.asarray(valid), wt)


def kernel(x, positions, norm_gain, w_in, ret_norm_gain, q_norm_gain, k_norm_gain, gm_norm_gain, w_spatial,
           b_spatial, w_branch, w_out):
    batch, seq, d = x.shape
    assert d == D_MODEL and seq % Q_BLOCK == 0
    depth = w_in.shape[0]
    m = batch * seq
    topk = min(TOPK_MAX, seq // 4)
    cos_a, sin_a, cos_b, sin_b = _rope_tables(positions)
    x2 = x.reshape(m, d)
    w_branch_bf, w_out_bf = w_branch.astype(jnp.bfloat16), w_out.astype(jnp.bfloat16)
    for l in range(depth):
        xn = _rmsnorm(x2, norm_gain[l])
        h = _in_proj(xn, _relayout_w_in(w_in, l))
        y_ret = _retention(h, cos_a, sin_a, ret_norm_gain[l], batch, seq)
        k, vt, kidx, kn = _dsa_prep(h, cos_b, sin_b, k_norm_gain[l], batch, seq)
        mask = _dsa_select(h, cos_b, sin_b, kidx, topk)
        y_dsa = _dsa_attention(k, vt, kn, mask, h, cos_b, sin_b, q_norm_gain[l], batch, seq)
        y_gm = _gmlp(h, gm_norm_gain[l], w_spatial[l], b_spatial[l], m)
        merged = _merge((y_ret, y_dsa, y_gm), w_branch_bf, l, h)
        x2 = _out_proj(merged, w_out_bf, l, x2)
    return x2.reshape(batch, seq, d)
```

```python
import functools

import numpy as np
import jax
import jax.numpy as jnp
from jax import lax
from jax.experimental import pallas as pl
from jax.experimental.pallas import tpu as pltpu

D_MODEL = 4096
BRANCH_W = D_MODEL // 2
N_BRANCH = 3
RET_HEAD_DIM = 256
RET_HEADS = BRANCH_W // RET_HEAD_DIM
RET_CHUNK = 128
DSA_HEAD_DIM = 128
DSA_HEADS = BRANCH_W // DSA_HEAD_DIM
DSA_KV_HEADS = 4
DSA_GROUP = DSA_HEADS // DSA_KV_HEADS
DSA_KV_W = DSA_KV_HEADS * DSA_HEAD_DIM
IDX_HEADS = 32
IDX_DIM = 128
TOPK_MAX = 256
Q_BLOCK = 128
GM_GROUPS = 16
GM_GROUP_DIM = BRANCH_W // GM_GROUPS
GM_CHUNK = 128
ROPE_THETA = 10000.0
EPS = 1e-6

LANES = 128
VMEM_LIMIT = 56 * 1024 * 1024

_REF_SEGS = (("rq", BRANCH_W), ("rk", BRANCH_W), ("rv", BRANCH_W), ("rg", BRANCH_W),
             ("dq", BRANCH_W), ("dk", DSA_KV_W), ("dv", DSA_KV_W),
             ("iq", IDX_HEADS * IDX_DIM), ("ik", IDX_DIM), ("iw", IDX_HEADS),
             ("dg", BRANCH_W), ("gu", BRANCH_W), ("gv", BRANCH_W), ("gg", BRANCH_W),
             ("mg", N_BRANCH * D_MODEL))
_MY_ORDER = ("dg", "gu", "gv", "gg", "mg", "rq", "rk", "rv", "rg", "dq", "dk", "dv", "iq", "ik", "iw")
N_TILE = 768
IQ_BLOCK = 1024


def _layout():
    ref_off, o = {}, 0
    for name, w in _REF_SEGS:
        ref_off[name] = (o, w)
        o += w
    my_off, o = {}, 0
    for name in _MY_ORDER:
        w = ref_off[name][1]
        my_off[name] = o
        o += -(-w // LANES) * LANES
    total = -(-o // N_TILE) * N_TILE
    return ref_off, my_off, total


_REF_OFF, SEG, N_IN_PAD = _layout()

_NT = (((1,), (1,)), ((), ()))


def _cparams(*sem):
    return pltpu.CompilerParams(dimension_semantics=sem, vmem_limit_bytes=VMEM_LIMIT)


def _rmsnorm_kernel(x_ref, g_ref, o_ref):
    x = x_ref[...]
    ms = jnp.mean(x * x, axis=-1, keepdims=True)
    o_ref[...] = (x * lax.rsqrt(ms + EPS) * g_ref[...]).astype(o_ref.dtype)


def _rmsnorm(x2, gain):
    m, d = x2.shape
    tm = min(512, m)
    return pl.pallas_call(
        _rmsnorm_kernel,
        grid=(m // tm,),
        in_specs=[pl.BlockSpec((tm, d), lambda i: (i, 0)), pl.BlockSpec((1, d), lambda i: (0, 0))],
        out_specs=pl.BlockSpec((tm, d), lambda i: (i, 0)),
        out_shape=jax.ShapeDtypeStruct((m, d), jnp.bfloat16),
        compiler_params=_cparams("parallel"),
        name="rmsnorm",
    )(x2, gain.reshape(1, d))


def _matmul_kernel(a_ref, w_ref, o_ref):
    o_ref[...] = jnp.dot(a_ref[...], w_ref[...], preferred_element_type=jnp.float32).astype(o_ref.dtype)


def _in_proj(xn, w):
    m, k = xn.shape
    n = w.shape[1]
    tm, tn = min(1024, m), N_TILE
    return pl.pallas_call(
        _matmul_kernel,
        grid=(m // tm, n // tn),
        in_specs=[pl.BlockSpec((tm, k), lambda i, j: (i, 0)), pl.BlockSpec((k, tn), lambda i, j: (0, j))],
        out_specs=pl.BlockSpec((tm, tn), lambda i, j: (i, j)),
        out_shape=jax.ShapeDtypeStruct((m, n), jnp.bfloat16),
        compiler_params=_cparams("parallel", "arbitrary"),
        name="in_proj",
    )(xn, w)


def _rope_tables_kernel(pos_ref, inv_a_ref, inv_b_ref, sign_ref, ca_ref, sa_ref, cb_ref, sb_ref):
    p = pos_ref[...].astype(jnp.float32)
    ang_a = p * inv_a_ref[...]
    ca_ref[...] = jnp.cos(ang_a)
    sa_ref[...] = jnp.sin(ang_a)
    ang_b = p * inv_b_ref[...]
    cb_ref[...] = jnp.cos(ang_b)
    sb_ref[...] = jnp.sin(ang_b) * sign_ref[...]


def _rope_tables(positions):
    m = positions.size
    tr = min(1024, m)

    def inv_freq(d):
        half = d // 2
        return 1.0 / (ROPE_THETA ** (jnp.arange(half, dtype=jnp.float32) * 2.0 / d))

    inv_a = inv_freq(RET_HEAD_DIM).reshape(1, LANES)
    inv_b = jnp.tile(inv_freq(DSA_HEAD_DIM), 2).reshape(1, LANES)
    sign = jnp.concatenate([-jnp.ones((LANES // 2,), jnp.float32), jnp.ones((LANES // 2,), jnp.float32)]).reshape(1, LANES)
    row = pl.BlockSpec((1, LANES), lambda i: (0, 0))
    tab = pl.BlockSpec((tr, LANES), lambda i: (i, 0))
    sds = jax.ShapeDtypeStruct((m, LANES), jnp.float32)
    return pl.pallas_call(
        _rope_tables_kernel,
        grid=(m // tr,),
        in_specs=[pl.BlockSpec((tr, 1), lambda i: (i, 0)), row, row, row],
        out_specs=[tab, tab, tab, tab],
        out_shape=[sds, sds, sds, sds],
        compiler_params=_cparams("parallel"),
        name="rope_tables",
    )(positions.reshape(m, 1), inv_a, inv_b, sign)


def _retention_kernel(q_ref, k_ref, v_ref, g_ref, cos_ref, sin_ref, decay_ref, xi_ref, zeta_ref, cd_ref,
                      gain_ref, o_ref, state_ref, *, n_chunk):
    @pl.when(pl.program_id(2) == 0)
    def _():
        state_ref[...] = jnp.zeros_like(state_ref)

    half = RET_HEAD_DIM // 2
    decay = decay_ref[...]
    xi = xi_ref[...]
    zeta = zeta_ref[...]
    cd = cd_ref[0:1, :]
    gain = gain_ref[...]
    k_scale = RET_HEAD_DIM ** -0.5

    def rot(x, cos, sin):
        x1, x2 = x[:, :half], x[:, half:]
        return x1 * cos - x2 * sin, x2 * cos + x1 * sin

    for c in range(n_chunk):
        rows = slice(c * RET_CHUNK, (c + 1) * RET_CHUNK)
        cos, sin = cos_ref[rows, :], sin_ref[rows, :]
        q1, q2 = rot(q_ref[rows, :].astype(jnp.float32), cos, sin)
        k1, k2 = rot(k_ref[rows, :].astype(jnp.float32), cos, sin)
        k1, k2 = k1 * k_scale, k2 * k_scale
        v = v_ref[rows, :]
        qb = jnp.concatenate([q1, q2], axis=1).astype(jnp.bfloat16)
        kb = jnp.concatenate([k1, k2], axis=1).astype(jnp.bfloat16)
        sc = lax.dot_general(qb, kb, _NT, preferred_element_type=jnp.float32) * decay
        inner = jnp.dot(sc.astype(jnp.bfloat16), v, preferred_element_type=jnp.float32)
        state = state_ref[...]
        qx = jnp.concatenate([q1 * xi, q2 * xi], axis=1).astype(jnp.bfloat16)
        cross = jnp.dot(qx, state.astype(jnp.bfloat16), preferred_element_type=jnp.float32)
        out = inner + cross
        kz_t = jnp.concatenate([(k1 * zeta).T, (k2 * zeta).T], axis=0).astype(jnp.bfloat16)
        state_ref[...] = (state * jnp.concatenate([cd, cd], axis=1)
                          + jnp.dot(kz_t, v, preferred_element_type=jnp.float32))
        mu = jnp.mean(out, axis=-1, keepdims=True)
        oc = out - mu
        y = oc * lax.rsqrt(jnp.mean(oc * oc, axis=-1, keepdims=True) + EPS) * gain
        g = g_ref[rows, :].astype(jnp.float32)
        o_ref[rows, :] = (g * jax.nn.sigmoid(g) * y).astype(o_ref.dtype)


def _retention(h, cos_a, sin_a, ret_gain, batch, seq):
    m = batch * seq
    hd = RET_HEAD_DIM
    rb = min(4096, seq)
    nrb = seq // rb
    c = RET_CHUNK
    log_g = jnp.log(1.0 - jnp.power(2.0, -5.0 - jnp.arange(RET_HEADS, dtype=jnp.float32)))
    n = jnp.arange(c, dtype=jnp.float32)
    diff = n[:, None] - n[None, :]
    decay = jnp.where(diff >= 0, jnp.exp(log_g[:, None, None] * jnp.maximum(diff, 0.0)), 0.0)
    xi = jnp.broadcast_to(jnp.exp(log_g[:, None] * (n + 1.0))[:, :, None], (RET_HEADS, c, LANES))
    zeta = jnp.broadcast_to(jnp.exp(log_g[:, None] * (c - 1.0 - n))[:, :, None], (RET_HEADS, c, LANES))
    cd = jnp.broadcast_to(jnp.exp(log_g * c)[:, None, None], (RET_HEADS, 8, LANES))

    def seg(name):
        assert SEG[name] % hd == 0
        base = SEG[name] // hd
        return pl.BlockSpec((rb, hd), lambda b, hh, r: (b * nrb + r, base + hh))

    tab = pl.BlockSpec((rb, LANES), lambda b, hh, r: (b * nrb + r, 0))
    per_head = lambda rows: pl.BlockSpec((None, rows, LANES), lambda b, hh, r: (hh, 0, 0))
    return pl.pallas_call(
        functools.partial(_retention_kernel, n_chunk=rb // c),
        grid=(batch, RET_HEADS, nrb),
        in_specs=[seg("rq"), seg("rk"), seg("rv"), seg("rg"), tab, tab,
                  per_head(c), per_head(c), per_head(c), per_head(8),
                  pl.BlockSpec((1, hd), lambda b, hh, r: (0, hh))],
        out_specs=pl.BlockSpec((rb, hd), lambda b, hh, r: (b * nrb + r, hh)),
        out_shape=jax.ShapeDtypeStruct((m, BRANCH_W), jnp.bfloat16),
        scratch_shapes=[pltpu.VMEM((hd, hd), jnp.float32)],
        compiler_params=_cparams("parallel", "parallel", "arbitrary"),
        name="retention",
    )(h, h, h, h, cos_a, sin_a, decay, xi, zeta, cd, ret_gain.reshape(1, BRANCH_W))


def _rope_half(x, cos, sin):
    return x * cos + pltpu.roll(x, DSA_HEAD_DIM // 2, 1) * sin


def _rms_head(x, g):
    return x * lax.rsqrt(jnp.mean(x * x, axis=-1, keepdims=True) + EPS) * g


def _dsa_prep_kernel(dk_ref, dv_ref, ik_ref, cos_ref, sin_ref, kgain_ref, k_ref, vt_ref, kidx_ref, kn_ref):
    cos, sin = cos_ref[...], sin_ref[...]
    d = DSA_HEAD_DIM
    kgain = kgain_ref[...]
    for kvh in range(DSA_KV_HEADS):
        x = dk_ref[:, kvh * d:(kvh + 1) * d].astype(jnp.float32)
        kb = _rope_half(_rms_head(x, kgain), cos, sin).astype(k_ref.dtype)
        k_ref[kvh] = kb
        kf = kb.astype(jnp.float32)
        kn2 = jnp.max(jnp.sum(kf * kf, axis=-1, keepdims=True), axis=0, keepdims=True)
        kn_ref[kvh:kvh + 1, :] = jnp.broadcast_to(kn2, (1, LANES))
        vt_ref[kvh] = dv_ref[:, kvh * d:(kvh + 1) * d].astype(jnp.float32).T.astype(vt_ref.dtype)
    kidx_ref[...] = _rope_half(ik_ref[...].astype(jnp.float32), cos, sin).astype(kidx_ref.dtype)


def _h_block(name, width, rows, row_index, part=0):
    assert SEG[name] % width == 0
    base = SEG[name] // width + part
    return pl.BlockSpec((rows, width), lambda b, j: (row_index(b, j), base))


def _dsa_prep(h, cos_b, sin_b, k_gain, batch, seq):
    nq = seq // Q_BLOCK
    d = DSA_HEAD_DIM
    row = lambda b, j: b * nq + j
    tab = pl.BlockSpec((Q_BLOCK, LANES), lambda b, j: (row(b, j), 0))
    bf = jnp.bfloat16
    return pl.pallas_call(
        _dsa_prep_kernel,
        grid=(batch, nq),
        in_specs=[_h_block("dk", DSA_KV_W, Q_BLOCK, row), _h_block("dv", DSA_KV_W, Q_BLOCK, row),
                  _h_block("ik", IDX_DIM, Q_BLOCK, row), tab, tab, pl.BlockSpec((1, d), lambda b, j: (0, 0))],
        out_specs=[
            pl.BlockSpec((None, DSA_KV_HEADS, Q_BLOCK, d), lambda b, j: (b, 0, j, 0)),
            pl.BlockSpec((None, DSA_KV_HEADS, None, d, Q_BLOCK), lambda b, j: (b, 0, j, 0, 0)),
            pl.BlockSpec((None, Q_BLOCK, IDX_DIM), lambda b, j: (b, j, 0)),
            pl.BlockSpec((None, None, DSA_KV_HEADS, LANES), lambda b, j: (b, j, 0, 0)),
        ],
        out_shape=[
            jax.ShapeDtypeStruct((batch, DSA_KV_HEADS, seq, d), bf),
            jax.ShapeDtypeStruct((batch, DSA_KV_HEADS, nq, d, Q_BLOCK), bf),
            jax.ShapeDtypeStruct((batch, seq, IDX_DIM), bf),
            jax.ShapeDtypeStruct((batch, nq, DSA_KV_HEADS, LANES), jnp.float32),
        ],
        compiler_params=_cparams("parallel", "parallel"),
        name="dsa_prep",
    )(h, h, h, cos_b, sin_b, k_gain.reshape(1, d))


def _dsa_select_kernel(iq0_ref, iq1_ref, iq2_ref, iq3_ref, iw_ref, cos_ref, sin_ref, kidx_ref, mask_ref,
                       hi_ref, lo_ref, qi_ref, w_ref, *, topk, nq, per, nqb):
    jp = pl.program_id(1)
    c_ = Q_BLOCK
    tk = per * c_
    blocks = range(nqb)
    iq_refs = (iq0_ref, iq1_ref, iq2_ref, iq3_ref)
    for qb in blocks:
        tok = slice(qb * c_, (qb + 1) * c_)
        cos, sin = cos_ref[tok, :], sin_ref[tok, :]
        w = iw_ref[tok, :].astype(jnp.float32) * (IDX_DIM ** -0.5 * IDX_HEADS ** -0.5)
        for head in range(IDX_HEADS):
            part, col = divmod(head * IDX_DIM, IQ_BLOCK)
            x = iq_refs[part][tok, col:col + IDX_DIM].astype(jnp.float32)
            qi_ref[qb, head * c_:(head + 1) * c_, :] = _rope_half(x, cos, sin).astype(qi_ref.dtype)
            w_ref[qb, head * c_:(head + 1) * c_, :] = jnp.broadcast_to(w[:, head:head + 1], (c_, LANES))
    sub = lax.broadcasted_iota(jnp.int32, (c_, c_), 0)
    lane = lax.broadcasted_iota(jnp.int32, (c_, c_), 1)
    int_min = jnp.int32(-2 ** 31)
    i16_min = jnp.int32(-2 ** 15)
    n_tiles = (jp * nqb + nqb - 1 + per) // per

    def causal(qb, chunk):
        return (chunk * c_ + sub) <= ((jp * nqb + qb) * c_ + lane)

    def chunk_rows(t, u):
        return pl.ds(pl.multiple_of((t * per + u) * c_, c_), c_)

    def store_keys(qb, chunk, key):
        rows = pl.ds(pl.multiple_of(chunk * c_, c_), c_)
        hi_ref[qb, rows, :] = (key >> 16).astype(jnp.int16)
        lo_ref[qb, rows, :] = ((key & 0xFFFF) - 32768).astype(jnp.int16)

    def score_tile(t, carry):
        for u2 in range(per // 2):
            first = t * per + 2 * u2
            kt = kidx_ref[pl.ds(pl.multiple_of(first * c_, 2 * c_), 2 * c_), :]
            s = lax.dot_general(qi_ref[...].reshape(nqb * IDX_HEADS * c_, IDX_DIM), kt, _NT,
                                preferred_element_type=jnp.float32)
            for qb in blocks:
                acc = [jnp.zeros((c_, c_), jnp.float32), jnp.zeros((c_, c_), jnp.float32)]
                for head in range(IDX_HEADS):
                    rows = slice((qb * IDX_HEADS + head) * c_, (qb * IDX_HEADS + head + 1) * c_)
                    w = w_ref[qb, head * c_:(head + 1) * c_, :]
                    for half in range(2):
                        acc[half] = acc[half] + jnp.maximum(s[rows, half * c_:(half + 1) * c_], 0.0) * w
                for half in range(2):
                    bits = lax.bitcast_convert_type(acc[half].T, jnp.int32)
                    key = bits ^ ((bits >> 31) & jnp.int32(0x7FFFFFFF))
                    store_keys(qb, first + half, jnp.where(causal(qb, first + half), key, int_min))
        return carry

    lax.fori_loop(0, n_tiles, score_tile, 0)

    def count16(hits):
        def body(t, accs):
            accs = list(accs)
            for u in range(per):
                rows = chunk_rows(t, u)
                for qb in blocks:
                    hit = jnp.where(hits(qb, rows), jnp.int16(1), jnp.int16(0)).reshape(c_ // 16, 16, c_)
                    accs[qb * per + u] = accs[qb * per + u] + functools.reduce(
                        lambda a, b: a + b, [hit[i] for i in range(c_ // 16)])
            return tuple(accs)
        accs = lax.fori_loop(0, n_tiles, body, (jnp.zeros((16, c_), jnp.int16),) * (per * nqb))
        out = []
        for qb in blocks:
            total = functools.reduce(lambda a, b: a + b, [a.astype(jnp.int32) for a in accs[qb * per:(qb + 1) * per]])
            out.append(jnp.sum(total.astype(jnp.float32), axis=0, keepdims=True))
        return out

    def kth_largest16(ref, ks):
        zero = jnp.zeros((1, c_), jnp.int32)

        def counts_ge(cands):
            c16 = [c.astype(jnp.int16) for c in cands]
            return count16(lambda qb, rows: ref[qb, rows, :] >= c16[qb])

        first = counts_ge([zero] * nqb)
        vs = tuple(jnp.where(n >= k, zero, i16_min) for n, k in zip(first, ks))
        ns = tuple(jnp.where(n >= k, n, jnp.inf) for n, k in zip(first, ks))

        def bit_step(i, carry):
            vs, ns = carry
            cands = [v | jnp.left_shift(jnp.int32(1), 14 - i) for v in vs]
            counts = counts_ge(cands)
            ok = [n >= k for n, k in zip(counts, ks)]
            return (tuple(jnp.where(o, c, v) for o, c, v in zip(ok, cands, vs)),
                    tuple(jnp.where(o, n, m) for o, n, m in zip(ok, counts, ns)))

        return lax.fori_loop(0, 15, bit_step, (vs, ns))

    kf = jnp.float32(topk)
    thr_hi, _ = kth_largest16(hi_ref, [kf] * nqb)
    th = [v.astype(jnp.int16) for v in thr_hi]
    k_lo = [kf - n for n in count16(lambda qb, rows: hi_ref[qb, rows, :] > th[qb])]

    def restrict_lo(t, carry):
        for u in range(per):
            rows = chunk_rows(t, u)
            for qb in blocks:
                lo_ref[qb, rows, :] = jnp.where(hi_ref[qb, rows, :] == th[qb], lo_ref[qb, rows, :],
                                                jnp.int16(-2 ** 15))
        return carry

    lax.fori_loop(0, n_tiles, restrict_lo, 0)
    thr_lo, n_lo = kth_largest16(lo_ref, k_lo)
    n_ge = [kf - k + n for k, n in zip(k_lo, n_lo)]
    some_tie_left_out = jnp.max(functools.reduce(jnp.maximum, [jnp.where(n > kf, 1.0, 0.0) for n in n_ge])) > 0.0
    thr_lo = [jnp.where((h == i16_min) & (l == i16_min), i16_min + 1, l) for h, l in zip(thr_hi, thr_lo)]
    tl = [v.astype(jnp.int16) for v in thr_lo]

    def cmp_key(qb, rows, lo_test):
        hi = hi_ref[qb, rows, :]
        return (hi > th[qb]) | ((hi == th[qb]) & lo_test(lo_ref[qb, rows, :], tl[qb]))

    @pl.when(jnp.logical_not(some_tie_left_out))
    def _():
        def mask_tile(t, carry):
            for u in range(per):
                rows = chunk_rows(t, u)
                for qb in blocks:
                    take = cmp_key(qb, rows, lambda lo, t: lo >= t)
                    mask_ref[qb, rows, :] = jnp.where(take, jnp.ones((), mask_ref.dtype), jnp.zeros((), mask_ref.dtype))
            return carry
        lax.fori_loop(0, n_tiles, mask_tile, 0)

    @pl.when(some_tie_left_out)
    def _():
        tri = jnp.where(sub > lane, 1.0, 0.0).astype(jnp.bfloat16)
        thr = [jnp.left_shift(h, 16) | ((l + 32768) & 0xFFFF) for h, l in zip(thr_hi, thr_lo)]
        room = [kf - n for n in count16(lambda qb, rows: cmp_key(qb, rows, lambda lo, t: lo > t))]

        def mask_tile(t, seen):
            seen = list(seen)
            for u in range(per):
                rows = chunk_rows(t, u)
                for qb in blocks:
                    kc = (jnp.left_shift(hi_ref[qb, rows, :].astype(jnp.int32), 16)
                          | ((lo_ref[qb, rows, :].astype(jnp.int32) + 32768) & 0xFFFF))
                    eq = jnp.where(kc == thr[qb], 1.0, 0.0)
                    before = jnp.dot(tri, eq.astype(jnp.bfloat16), preferred_element_type=jnp.float32) + seen[qb]
                    take = jnp.where(kc > thr[qb], 1.0, jnp.where(before < room[qb], eq, 0.0))
                    mask_ref[qb, rows, :] = take.astype(mask_ref.dtype)
                    seen[qb] = seen[qb] + jnp.sum(eq, axis=0, keepdims=True)
            return tuple(seen)
        lax.fori_loop(0, n_tiles, mask_tile, (jnp.zeros((1, c_), jnp.float32),) * nqb)

    def zero_tile(t, carry):
        for qb in blocks:
            mask_ref[qb, pl.ds(pl.multiple_of(t * tk, tk), tk), :] = jnp.zeros((tk, c_), mask_ref.dtype)
        return carry

    lax.fori_loop(n_tiles, nq // per, zero_tile, 0)


def _dsa_select(h, cos_b, sin_b, kidx, topk):
    batch, seq = kidx.shape[0], kidx.shape[1]
    nq = seq // Q_BLOCK
    per = min(4, nq)
    nqb = 2
    assert nq % per == 0 and per % 2 == 0 and nq % nqb == 0 and seq < 2 ** 15
    rows = nqb * Q_BLOCK
    row = lambda b, j: b * (nq // nqb) + j
    n_iq = IDX_HEADS * IDX_DIM // IQ_BLOCK
    assert n_iq == 4
    tab = pl.BlockSpec((rows, LANES), lambda b, j: (row(b, j), 0))
    return pl.pallas_call(
        functools.partial(_dsa_select_kernel, topk=topk, nq=nq, per=per, nqb=nqb),
        grid=(batch, nq // nqb),
        in_specs=[_h_block("iq", IQ_BLOCK, rows, row, part) for part in range(n_iq)]
                 + [_h_block("iw", LANES, rows, row), tab, tab,
                    pl.BlockSpec((None, seq, IDX_DIM), lambda b, j: (b, 0, 0), pipeline_mode=pl.Buffered(1))],
        out_specs=pl.BlockSpec((None, nqb, seq, Q_BLOCK), lambda b, j: (b, j, 0, 0)),
        out_shape=jax.ShapeDtypeStruct((batch, nq, seq, Q_BLOCK), jnp.bfloat16),
        scratch_shapes=[pltpu.VMEM((nqb, seq, Q_BLOCK), jnp.int16),
                        pltpu.VMEM((nqb, seq, Q_BLOCK), jnp.int16),
                        pltpu.VMEM((nqb, IDX_HEADS * Q_BLOCK, IDX_DIM), jnp.bfloat16),
                        pltpu.VMEM((nqb, IDX_HEADS * Q_BLOCK, LANES), jnp.float32)],
        compiler_params=_cparams("parallel", "arbitrary"),
        name="dsa_select",
    )(*([h] * (n_iq + 1)), cos_b, sin_b, kidx)


def _dsa_attn_kernel(dq_ref, cos_ref, sin_ref, qgain_ref, k_ref, vt_ref, kn_ref, mask_ref, dg_ref, o_ref,
                     qg_ref, m_ref, acc_ref, s_ref, p_ref, alpha_ref, *, tk, n_key_tiles):
    j = pl.program_id(1)
    per = tk // Q_BLOCK
    d = DSA_HEAD_DIM
    gw = DSA_GROUP * Q_BLOCK
    heads = range(DSA_KV_HEADS)
    n_live = (j + per) // per
    n_pairs = (n_live + 1) // 2
    neg_inf = jnp.float32(-jnp.inf)
    q_scale = DSA_HEAD_DIM ** -0.5 * np.log2(np.e).astype(np.float32)
    cos, sin, qgain = cos_ref[...], sin_ref[...], qgain_ref[...]
    for kv in heads:
        for g in range(DSA_GROUP):
            hh = kv * DSA_GROUP + g
            x = dq_ref[:, hh * d:(hh + 1) * d].astype(jnp.float32)
            qg_ref[kv, g * Q_BLOCK:(g + 1) * Q_BLOCK, :] = (
                _rope_half(_rms_head(x, qgain), cos, sin) * q_scale).astype(qg_ref.dtype)
    acc_ref[...] = jnp.zeros_like(acc_ref)
    p_ref[:, 1] = jnp.zeros((DSA_KV_HEADS,) + p_ref.shape[2:], p_ref.dtype)
    ones_rows = jnp.ones((acc_ref.shape[1] - d, tk), jnp.bfloat16)

    k_max2 = jnp.max(kn_ref[...], axis=0)
    ones8 = jnp.ones((8, d), jnp.bfloat16)
    bound = []
    for kv in heads:
        qf = qg_ref[kv].astype(jnp.float32)
        q_norm2 = lax.dot_general(ones8, (qf * qf).astype(jnp.bfloat16), _NT,
                                  preferred_element_type=jnp.float32)[0:1, :]
        bound.append(jnp.sqrt(q_norm2 * jnp.concatenate([k_max2[kv:kv + 1, :]] * DSA_GROUP, axis=1)) * 1.03 + 1e-3)
    bounded_is_safe = jnp.max(functools.reduce(jnp.maximum, bound)) <= 50.0

    def logits(kv, t):
        r0 = pl.multiple_of(t * tk, tk)
        return lax.dot_general(k_ref[kv, pl.ds(r0, tk), :], qg_ref[kv], _NT, preferred_element_type=jnp.float32)

    def softmax_running_max(kv, sel, slot):
        s = s_ref[kv, slot]
        s = jnp.concatenate([jnp.where(sel, s[:, g * Q_BLOCK:(g + 1) * Q_BLOCK], neg_inf)
                             for g in range(DSA_GROUP)], axis=1)
        m_old = m_ref[kv]
        m_new = jnp.maximum(m_old, jnp.max(s, axis=0, keepdims=True))
        m_safe = jnp.where(m_new == neg_inf, 0.0, m_new)
        p_ref[kv, slot] = jnp.exp2(s - m_safe).astype(p_ref.dtype)
        alpha_ref[kv, slot] = jnp.exp2(m_old - m_safe)
        m_ref[kv] = m_new

    def softmax_bounded(kv, sel, slot):
        s = s_ref[kv, slot]
        parts = []
        for g in range(DSA_GROUP):
            cols = slice(g * Q_BLOCK, (g + 1) * Q_BLOCK)
            shift = jnp.where(sel, -bound[kv][:, cols], neg_inf)
            parts.append(jnp.exp2(s[:, cols] + shift))
        p_ref[kv, slot] = jnp.concatenate(parts, axis=1).astype(p_ref.dtype)

    def pv_stage(kv, t, slot, rescale):
        vt = jnp.concatenate([vt_ref[kv, t * per + i] for i in range(per)], axis=1)
        lhs = jnp.concatenate([vt, ones_rows], axis=0)
        pv = jnp.dot(lhs, p_ref[kv, slot], preferred_element_type=jnp.float32)
        acc_ref[kv] = (acc_ref[kv] * alpha_ref[kv, slot] if rescale else acc_ref[kv]) + pv

    def run(softmax_stage, rescale):
        for kv in heads:
            s_ref[kv, 0] = logits(kv, 0)

        def pair(i, carry):
            for u in range(2):
                t = 2 * i + u
                r0 = pl.multiple_of(t * tk, tk)
                sel = mask_ref[pl.ds(r0, tk), :].astype(jnp.float32) > 0.5
                for kv in heads:
                    s_ref[kv, 1 - u] = logits(kv, jnp.minimum(t + 1, n_key_tiles - 1))
                    softmax_stage(kv, sel, u)
                    pv_stage(kv, jnp.maximum(t - 1, 0), 1 - u, rescale)
            return carry

        lax.fori_loop(0, n_pairs, pair, 0)
        for kv in heads:
            pv_stage(kv, 2 * n_pairs - 1, 1, rescale)

    @pl.when(bounded_is_safe)
    def _():
        run(softmax_bounded, rescale=False)

    @pl.when(jnp.logical_not(bounded_is_safe))
    def _():
        m_ref[...] = jnp.full_like(m_ref, neg_inf)
        alpha_ref[:, 1] = jnp.ones((DSA_KV_HEADS,) + alpha_ref.shape[2:], alpha_ref.dtype)
        run(softmax_running_max, rescale=True)

    for kv in heads:
        out_t = acc_ref[kv, :d, :] / acc_ref[kv, d:d + 1, :]
        for g in range(DSA_GROUP):
            cols = slice(kv * gw + g * Q_BLOCK, kv * gw + (g + 1) * Q_BLOCK)
            gate = dg_ref[:, cols].astype(jnp.float32)
            o_ref[:, cols] = (gate * jax.nn.sigmoid(gate)
                              * out_t[:, g * Q_BLOCK:(g + 1) * Q_BLOCK].T).astype(o_ref.dtype)


def _dsa_attention(k, vt, kn, mask, h, cos_b, sin_b, q_gain, batch, seq):
    nq = seq // Q_BLOCK
    d = DSA_HEAD_DIM
    gw = DSA_GROUP * d
    tk = min(512, seq // 2)
    assert seq % (2 * tk) == 0
    assert SEG["dg"] % BRANCH_W == 0
    kvh = DSA_KV_HEADS
    once = pl.Buffered(1)
    row = lambda b, j: b * nq + j
    tab = pl.BlockSpec((Q_BLOCK, LANES), lambda b, j: (row(b, j), 0))
    return pl.pallas_call(
        functools.partial(_dsa_attn_kernel, tk=tk, n_key_tiles=seq // tk),
        grid=(batch, nq),
        in_specs=[_h_block("dq", BRANCH_W, Q_BLOCK, row), tab, tab, pl.BlockSpec((1, d), lambda b, j: (0, 0)),
                  pl.BlockSpec((None, kvh, seq, d), lambda b, j: (b, 0, 0, 0), pipeline_mode=once),
                  pl.BlockSpec((None, kvh, nq, d, Q_BLOCK), lambda b, j: (b, 0, 0, 0, 0), pipeline_mode=once),
                  pl.BlockSpec((None, nq, kvh, LANES), lambda b, j: (b, 0, 0, 0), pipeline_mode=once),
                  pl.BlockSpec((None, None, seq, Q_BLOCK), lambda b, j: (b, j, 0, 0)),
                  pl.BlockSpec((Q_BLOCK, BRANCH_W), lambda b, j: (b * nq + j, SEG["dg"] // BRANCH_W))],
        out_specs=pl.BlockSpec((Q_BLOCK, BRANCH_W), lambda b, j: (b * nq + j, 0)),
        out_shape=jax.ShapeDtypeStruct((batch * seq, BRANCH_W), jnp.bfloat16),
        scratch_shapes=[pltpu.VMEM((kvh, gw, d), jnp.bfloat16),
                        pltpu.VMEM((kvh, 1, gw), jnp.float32),
                        pltpu.VMEM((kvh, d + 16, gw), jnp.float32),
                        pltpu.VMEM((kvh, 2, tk, gw), jnp.float32),
                        pltpu.VMEM((kvh, 2, tk, gw), jnp.bfloat16),
                        pltpu.VMEM((kvh, 2, 1, gw), jnp.float32)],
        compiler_params=_cparams("parallel", "arbitrary"),
        name="dsa_attn",
    )(h, cos_b, sin_b, q_gain.reshape(1, d), k, vt, kn, mask, h)


def _gelu_tanh(x):
    return 0.5 * x * (1.0 + jnp.tanh(np.sqrt(2.0 / np.pi).astype(np.float32) * (x + 0.044715 * (x * x * x))))


def _gmlp_kernel(u_ref, v_ref, g_ref, gain_ref, w_ref, b_ref, o_ref):
    c_ = GM_CHUNK
    v = _gelu_tanh(v_ref[...].astype(jnp.float32))
    mu = jnp.mean(v, axis=-1, keepdims=True)
    vc = v - mu
    vn = (vc * lax.rsqrt(jnp.mean(vc * vc, axis=-1, keepdims=True) + EPS) * gain_ref[...]).astype(jnp.bfloat16)
    sub = lax.broadcasted_iota(jnp.int32, (c_, c_), 0)
    lane = lax.broadcasted_iota(jnp.int32, (c_, c_), 1)
    tril = sub >= lane
    for g in range(GM_GROUPS):
        cols = slice(g * GM_GROUP_DIM, (g + 1) * GM_GROUP_DIM)
        w = jnp.where(tril, w_ref[g], jnp.zeros((), w_ref.dtype))
        mixed = jnp.dot(w, vn[:, cols], preferred_element_type=jnp.float32) + b_ref[g]
        u = _gelu_tanh(u_ref[:, cols].astype(jnp.float32))
        gate = g_ref[:, cols].astype(jnp.float32)
        o_ref[:, cols] = (gate * jax.nn.sigmoid(gate) * (u * mixed)).astype(o_ref.dtype)


def _gmlp(h, gm_gain, w_spatial, b_spatial, m):
    c_ = GM_CHUNK
    w = BRANCH_W

    def seg(name):
        assert SEG[name] % w == 0
        base = SEG[name] // w
        return pl.BlockSpec((c_, w), lambda i: (i, base))

    b_b = jnp.broadcast_to(b_spatial[:, :, None], (GM_GROUPS, c_, GM_GROUP_DIM))
    whole = lambda shape: pl.BlockSpec(shape, lambda i: (0,) * len(shape))
    return pl.pallas_call(
        _gmlp_kernel,
        grid=(m // c_,),
        in_specs=[seg("gu"), seg("gv"), seg("gg"), whole((1, w)), whole((GM_GROUPS, c_, c_)),
                  whole((GM_GROUPS, c_, GM_GROUP_DIM))],
        out_specs=pl.BlockSpec((c_, w), lambda i: (i, 0)),
        out_shape=jax.ShapeDtypeStruct((m, w), jnp.bfloat16),
        compiler_params=_cparams("parallel"),
        name="gmlp",
    )(h, h, h, gm_gain.reshape(1, w), w_spatial.astype(jnp.bfloat16), b_b)


def _merge_kernel(y0_ref, y1_ref, y2_ref, w_ref, g0_ref, g1_ref, g2_ref, o_ref):
    acc = None
    for b, (y_ref, g_ref) in enumerate(((y0_ref, g0_ref), (y1_ref, g1_ref), (y2_ref, g2_ref))):
        proj = jnp.dot(y_ref[...], w_ref[b], preferred_element_type=jnp.float32)
        term = jax.nn.sigmoid(g_ref[...].astype(jnp.float32)) * proj
        acc = term if acc is None else acc + term
    o_ref[...] = acc.astype(o_ref.dtype)


def _merge(ys, w_branch, layer, h):
    m = h.shape[0]
    tm, tn = min(1024, m), 512
    ybs = pl.BlockSpec((tm, BRANCH_W), lambda i, j: (i, 0))

    def gate(b):
        base = (SEG["mg"] + b * D_MODEL) // tn
        return pl.BlockSpec((tm, tn), lambda i, j: (i, base + j))

    return pl.pallas_call(
        _merge_kernel,
        grid=(m // tm, D_MODEL // tn),
        in_specs=[ybs, ybs, ybs, pl.BlockSpec((None, N_BRANCH, BRANCH_W, tn), lambda i, j: (layer, 0, 0, j)),
                  gate(0), gate(1), gate(2)],
        out_specs=pl.BlockSpec((tm, tn), lambda i, j: (i, j)),
        out_shape=jax.ShapeDtypeStruct((m, D_MODEL), jnp.bfloat16),
        compiler_params=_cparams("parallel", "arbitrary"),
        name="merge",
    )(*ys, w_branch, h, h, h)


def _out_proj_kernel(a_ref, w_ref, x_ref, o_ref):
    o_ref[...] = x_ref[...] + jnp.dot(a_ref[...], w_ref[...], preferred_element_type=jnp.float32)


def _out_proj(merged, w_out, layer, x2):
    m, d = x2.shape
    tm, tn = min(1024, m), 1024
    return pl.pallas_call(
        _out_proj_kernel,
        grid=(m // tm, d // tn),
        in_specs=[pl.BlockSpec((tm, d), lambda i, j: (i, 0)),
                  pl.BlockSpec((None, d, tn), lambda i, j: (layer, 0, j)),
                  pl.BlockSpec((tm, tn), lambda i, j: (i, j))],
        out_specs=pl.BlockSpec((tm, tn), lambda i, j: (i, j)),
        out_shape=jax.ShapeDtypeStruct((m, d), jnp.float32),
        compiler_params=_cparams("parallel", "arbitrary"),
        name="out_proj",
    )(merged, w_out, x2)


def _w_in_runs():
    runs, o = [], 0
    for name in _MY_ORDER:
        off, width = _REF_OFF[name]
        assert SEG[name] == o
        if runs and runs[-1][1] is not None and runs[-1][1] + runs[-1][2] == off and runs[-1][0] + runs[-1][2] == o:
            runs[-1][2] += width
        else:
            runs.append([o, off, width])
        padded = -(-width // LANES) * LANES
        if padded != width:
            runs.append([o + width, None, padded - width])
        o += padded
    if N_IN_PAD != o:
        runs.append([o, None, N_IN_PAD - o])
    return runs


RELAYOUT_COLS = 256


def _relayout_plan():
    src_of = np.full((N_IN_PAD,), -1, np.int64)
    for dst, src, width in _w_in_runs():
        if src is not None:
            src_of[dst:dst + width] = np.arange(src, src + width)
    first, valid = [], []
    for c in range(N_IN_PAD // RELAYOUT_COLS):
        cols = src_of[c * RELAYOUT_COLS:(c + 1) * RELAYOUT_COLS]
        n = int((cols >= 0).sum())
        assert n > 0 and (cols[:n] == cols[0] + np.arange(n)).all() and (cols[n:] < 0).all()
        first.append(int(cols[0]))
        valid.append(n)
    return np.asarray(first, np.int32), np.asarray(valid, np.int32)


def _relayout_kernel(first_ref, valid_ref, wt_ref, o_ref):
    x = wt_ref[0].T
    lane = lax.broadcasted_iota(jnp.int32, x.shape, 1)
    o_ref[...] = jnp.where(lane < valid_ref[pl.program_id(0)], x, 0.0).astype(o_ref.dtype)


def _relayout_w_in(w_in, layer):
    _, d, n = w_in.shape
    first, valid = _relayout_plan()
    assert N_IN_PAD % RELAYOUT_COLS == 0 and int((first + RELAYOUT_COLS).max()) <= n
    wt = jnp.swapaxes(w_in, 1, 2)
    return pl.pallas_call(
        _relayout_kernel,
        grid_spec=pltpu.PrefetchScalarGridSpec(
            num_scalar_prefetch=2,
            grid=(N_IN_PAD // RELAYOUT_COLS,),
            in_specs=[pl.BlockSpec((pl.Element(1), pl.Element(RELAYOUT_COLS), pl.Element(d)),
                                   lambda c, first, valid: (layer, pl.multiple_of(first[c], 8), 0))],
            out_specs=pl.BlockSpec((d, RELAYOUT_COLS), lambda c, first, valid: (0, c)),
        ),
        out_shape=jax.ShapeDtypeStruct((d, N_IN_PAD), jnp.bfloat16),
        compiler_params=_cparams("parallel"),
        name="relayout_w_in",
    )(jnp.asarray(first), jnp.asarray(valid), wt)


def kernel(x, positions, norm_gain, w_in, ret_norm_gain, q_norm_gain, k_norm_gain, gm_norm_gain, w_spatial,
           b_spatial, w_branch, w_out):
    batch, seq, d = x.shape
    assert d == D_MODEL and seq % Q_BLOCK == 0
    depth = w_in.shape[0]
    m = batch * seq
    topk = min(TOPK_MAX, seq // 4)
    cos_a, sin_a, cos_b, sin_b = _rope_tables(positions)
    x2 = x.reshape(m, d)
    w_branch_bf, w_out_bf = w_branch.astype(jnp.bfloat16), w_out.astype(jnp.bfloat16)
    for l in range(depth):
        xn = _rmsnorm(x2, norm_gain[l])
        h = _in_proj(xn, _relayout_w_in(w_in, l))
        y_ret = _retention(h, cos_a, sin_a, ret_norm_gain[l], batch, seq)
        k, vt, kidx, kn = _dsa_prep(h, cos_b, sin_b, k_norm_gain[l], batch, seq)
        mask = _dsa_select(h, cos_b, sin_b, kidx, topk)
        y_dsa = _dsa_attention(k, vt, kn, mask, h, cos_b, sin_b, q_norm_gain[l], batch, seq)
        y_gm = _gmlp(h, gm_norm_gain[l], w_spatial[l], b_spatial[l], m)
        merged = _merge((y_ret, y_dsa, y_gm), w_branch_bf, l, h)
        x2 = _out_proj(merged, w_out_bf, l, x2)
    return x2.reshape(batch, seq, d)
```
